```python
import numpy as np
import jax
import jax.numpy as jnp
from jax import lax

D_MODEL = 4096
BATCH = 1
SEQ = 8192
DEPTH = 2

N_MEM = 256
RMS_EPS = 1e-6
NEG_INF = -1e30
A_HEADS = 8
A_DQK = D_MODEL // 16
A_DV = D_MODEL // A_HEADS
A_CHUNK = 128
B_HEADS = 32
B_KV_HEADS = 4
B_HEAD_DIM = D_MODEL // B_HEADS
B_GROUP = B_HEADS // B_KV_HEADS
CMP_BLOCK = 32
CMP_STRIDE = 16
CMP_HIDDEN = 2 * B_HEAD_DIM
SEL_BLOCK = 64
SEL_TOPK = 16
WINDOW = 512
Q_BLOCK = 128
SEL_FORCE = 1e6
MEM_HEADS = 4
MEM_HEAD_DIM = 128
N_GROUPS = 4
EXPERTS_PER_GROUP = 4
N_EXPERTS = N_GROUPS * EXPERTS_PER_GROUP
EXPERT_TOPK = 2
D_EXPERT = D_MODEL // 4

kernel_name = 'yoco_mlstm_nsa_hmoe_block'

f32 = jnp.float32


def _rmsnorm(x, g):
    xf = x.astype(f32)
    y = xf * lax.rsqrt(jnp.mean(xf * xf, axis=-1, keepdims=True) + RMS_EPS)
    return (y * g.astype(f32)).astype(x.dtype)


def _alibi_slopes(n):
    return np.asarray([2.0 ** (-8.0 * (h + 1) / n) for h in range(n)], dtype=np.float32)


def _masked_softmax(s, ok):
    s = jnp.where(ok, s, NEG_INF)
    return jnp.where(ok, jax.nn.softmax(s, axis=-1), 0.0)


def _mlstm_mixer(h, w_in, b_if, head_g, w_out):
    bsz, t_len, _ = h.shape
    H, dk, dv, L = A_HEADS, A_DQK, A_DV, A_CHUNK
    nc = t_len // L
    proj = h @ w_in
    cuts = [H * dk, 2 * H * dk, 2 * H * dk + H * dv, 2 * H * dk + 2 * H * dv, 2 * H * dk + 2 * H * dv + H]
    q, k, v, og, gi, gf = jnp.split(proj, cuts, axis=-1)

    def heads(z, d):
        return z.reshape(bsz, nc, L, H, d).transpose(1, 0, 3, 2, 4)

    def gate(z):
        return z.astype(f32).reshape(bsz, nc, L, H).transpose(1, 0, 3, 2)

    qc = heads(q, dk) * (dk ** -0.5)
    kc = heads(k, dk)
    vc = heads(v, dv)
    log_i = gate(gi + b_if[:H])
    log_f = jax.nn.log_sigmoid(gate(gf + b_if[H:]))
    causal = jnp.asarray(np.tril(np.ones((L, L), dtype=bool)))

    def step(carry, xs):
        c_mat, n_vec, m_prev = carry
        q_c, k_c, v_c, li, lf = xs
        q_c = q_c.astype(f32)
        k_c = k_c.astype(f32)
        v_c = v_c.astype(f32)
        b = jnp.cumsum(lf, axis=-1)
        dmat = jnp.where(causal, b[..., :, None] - b[..., None, :] + li[..., None, :], -jnp.inf)
        m_inter = b + m_prev[..., None]
        m_t = jnp.maximum(m_inter, jnp.max(dmat, axis=-1))
        s = jnp.einsum('bhtd,bhsd->bhts', q_c, k_c) * jnp.exp(dmat - m_t[..., None])
        w_inter = jnp.exp(m_inter - m_t)
        num = jnp.einsum('bhts,bhsv->bhtv', s, v_c) + w_inter[..., None] * jnp.einsum('bhtd,bhvd->bhtv', q_c, c_mat)
        den = jnp.sum(s, axis=-1) + w_inter * jnp.einsum('bhtd,bhd->bht', q_c, n_vec)
        h_c = num / jnp.maximum(jnp.abs(den), jnp.exp(-m_t))[..., None]
        b_end = b[..., -1]
        w_log = b_end[..., None] - b + li
        m_new = jnp.maximum(b_end + m_prev, jnp.max(w_log, axis=-1))
        decay = jnp.exp(b_end + m_prev - m_new)
        w_wr = jnp.exp(w_log - m_new[..., None])
        c_new = decay[..., None, None] * c_mat + jnp.einsum('bhsv,bhsd->bhvd', v_c * w_wr[..., None], k_c)
        n_new = decay[..., None] * n_vec + jnp.einsum('bhs,bhsd->bhd', w_wr, k_c)
        return (c_new, n_new, m_new), h_c

    init = (jnp.zeros((bsz, H, dv, dk), f32), jnp.zeros((bsz, H, dk), f32), jnp.zeros((bsz, H), f32))
    _, hs = lax.scan(step, init, (qc, kc, vc, log_i, log_f))
    hs = hs.transpose(1, 0, 3, 2, 4).reshape(bsz, t_len, H, dv)
    hn = _rmsnorm(hs, head_g)
    o = jax.nn.sigmoid(og.astype(f32)).reshape(bsz, t_len, H, dv) * hn
    return o.astype(h.dtype).reshape(bsz, t_len, H * dv) @ w_out


def _nsa_shared_kv(u, kv_norm_g, kv_w, cmp_pe, cmp_w1, cmp_w2, k_norm_g):
    bsz, t_len, _ = u.shape
    G, dh = B_KV_HEADS, B_HEAD_DIM
    kv = (_rmsnorm(u, kv_norm_g) @ kv_w).reshape(bsz, t_len, 6, G, dh).transpose(2, 0, 3, 1, 4)
    n_cmp = (t_len - CMP_BLOCK) // CMP_STRIDE + 1
    idx = np.arange(n_cmp)[:, None] * CMP_STRIDE + np.arange(CMP_BLOCK)[None, :]

    def compress(z, pe, w1, w2):
        blocks = (z[:, :, idx, :] + pe).reshape(bsz, G, n_cmp, CMP_BLOCK * dh)
        return jax.nn.gelu(blocks @ w1) @ w2

    k_cmp = _rmsnorm(compress(kv[0], cmp_pe[0], cmp_w1[0], cmp_w2[0]), k_norm_g[0])
    v_cmp = compress(kv[1], cmp_pe[1], cmp_w1[1], cmp_w2[1])
    n_sel = t_len // SEL_BLOCK
    k_sel = _rmsnorm(kv[2], k_norm_g[1]).reshape(bsz, G, n_sel, SEL_BLOCK * dh)
    v_sel = kv[3].reshape(bsz, G, n_sel, SEL_BLOCK * dh)
    pad = ((0, 0), (0, 0), (WINDOW, 0), (0, 0))
    k_win = jnp.pad(_rmsnorm(kv[4], k_norm_g[2]), pad)
    v_win = jnp.pad(kv[5], pad)
    return (k_cmp, v_cmp, k_sel, v_sel, k_win, v_win)


def _nsa_mixer(h, shared, w_in, q_norm_g, w_out):
    k_cmp, v_cmp, k_sel, v_sel, k_win, v_win = shared
    bsz, t_len, _ = h.shape
    H, G, R, dh = B_HEADS, B_KV_HEADS, B_GROUP, B_HEAD_DIM
    proj = h @ w_in
    q = _rmsnorm(proj[..., :H * dh].reshape(bsz, t_len, G, R, dh), q_norm_g) * (dh ** -0.5)
    q = q.transpose(0, 2, 3, 1, 4)
    gates = jax.nn.sigmoid(proj[..., H * dh:].astype(f32)).reshape(bsz, t_len, 3, G, R).transpose(2, 0, 3, 4, 1)
    sl = jnp.asarray(_alibi_slopes(H)).reshape(G, R)
    n_cmp = k_cmp.shape[2]
    n_sel = t_len // SEL_BLOCK
    n_top = min(SEL_TOPK, n_sel)
    cmp_start = np.arange(n_cmp) * CMP_STRIDE
    cmp_end_np = cmp_start + CMP_BLOCK - 1
    sel_start = np.arange(n_sel) * SEL_BLOCK
    overlap = jnp.asarray(((cmp_start[:, None] < sel_start[None, :] + SEL_BLOCK) & (cmp_end_np[:, None] >= sel_start[None, :])).astype(np.float32))
    cmp_end = jnp.asarray(cmp_end_np.astype(np.int32))
    bi = jnp.arange(bsz)[:, None, None, None]
    gi = jnp.arange(G)[None, :, None, None]
    blk = jnp.arange(n_sel)

    def block(c):
        t0 = c * Q_BLOCK
        tq = t0 + jnp.arange(Q_BLOCK)
        qb = lax.dynamic_slice_in_dim(q, t0, Q_BLOCK, axis=3)
        gb = lax.dynamic_slice_in_dim(gates, t0, Q_BLOCK, axis=4)[..., None]
        d_c = tq[:, None] - cmp_end[None, :]
        s_c = jnp.einsum('bgrtd,bgnd->bgrtn', qb, k_cmp, preferred_element_type=f32) - sl[..., None, None] * d_c.astype(f32)
        p_c = _masked_softmax(s_c, d_c >= 0)
        o_c = jnp.einsum('bgrtn,bgnd->bgrtd', p_c, v_cmp.astype(f32))
        imp = jnp.einsum('bgrtn,nj->bgtj', p_c, overlap)
        cur = (tq // SEL_BLOCK)[:, None]
        forced = (blk == 0) | (blk == cur) | (blk == cur - 1)
        imp = jnp.where(forced, SEL_FORCE, imp)
        imp = jnp.where(blk * SEL_BLOCK <= tq[:, None], imp, -1.0)
        top_v, top_i = lax.top_k(imp, n_top)
        sel_ok = top_v >= 0.0
        k_g = k_sel[bi, gi, top_i].reshape(bsz, G, Q_BLOCK, n_top, SEL_BLOCK, dh)
        v_g = v_sel[bi, gi, top_i].reshape(bsz, G, Q_BLOCK, n_top, SEL_BLOCK, dh)
        pos = top_i[..., None] * SEL_BLOCK + jnp.arange(SEL_BLOCK)
        d_s = tq[:, None, None] - pos
        ok_s = (sel_ok[..., None] & (d_s >= 0))[:, :, None]
        s_s = jnp.einsum('bgrtd,bgtksd->bgrtks', qb, k_g, preferred_element_type=f32) - sl[..., None, None, None] * d_s.astype(f32)[:, :, None]
        flat = (bsz, G, R, Q_BLOCK, n_top * SEL_BLOCK)
        p_s = _masked_softmax(s_s.reshape(flat), ok_s.reshape(bsz, G, 1, Q_BLOCK, n_top * SEL_BLOCK)).reshape(s_s.shape)
        o_s = jnp.einsum('bgrtks,bgtksd->bgrtd', p_s, v_g.astype(f32))
        k_wb = lax.dynamic_slice_in_dim(k_win, t0, Q_BLOCK + WINDOW, axis=2)
        v_wb = lax.dynamic_slice_in_dim(v_win, t0, Q_BLOCK + WINDOW, axis=2)
        pos_w = t0 - WINDOW + jnp.arange(Q_BLOCK + WINDOW)
        d_w = tq[:, None] - pos_w[None, :]
        ok_w = (d_w >= 0) & (d_w < WINDOW) & (pos_w >= 0)[None, :]
        s_w = jnp.einsum('bgrtd,bgsd->bgrts', qb, k_wb, preferred_element_type=f32) - sl[..., None, None] * d_w.astype(f32)
        p_w = _masked_softmax(s_w, ok_w)
        o_w = jnp.einsum('bgrts,bgsd->bgrtd', p_w, v_wb.astype(f32))
        return gb[0] * o_c + gb[1] * o_s + gb[2] * o_w

    o = lax.map(block, jnp.arange(t_len // Q_BLOCK))
    o = o.transpose(1, 0, 4, 2, 3, 5).reshape(bsz, t_len, H * dh)
    return o.astype(h.dtype) @ w_out


def _memory_xattn(h, mem, mem_g, wq, wk, wv, wo, q_g, k_g):
    bsz, t_len, _ = h.shape
    m = _rmsnorm(mem, mem_g)
    q = _rmsnorm((h @ wq).reshape(bsz, t_len, MEM_HEADS, MEM_HEAD_DIM), q_g)
    k = _rmsnorm((m @ wk).reshape(bsz, -1, MEM_HEADS, MEM_HEAD_DIM), k_g)
    v = (m @ wv).reshape(bsz, -1, MEM_HEADS, MEM_HEAD_DIM)
    s = jnp.einsum('bthd,bmhd->bhtm', q, k, preferred_element_type=f32) * (MEM_HEAD_DIM ** -0.5)
    p = jax.nn.softmax(s, axis=-1)
    o = jnp.einsum('bhtm,bmhd->bthd', p, v.astype(f32)).astype(h.dtype)
    return o.reshape(bsz, t_len, MEM_HEADS * MEM_HEAD_DIM) @ wo


def _hier_moe(h, wg, bg, we, be, w1, w3, w2):
    bsz, t_len, d = h.shape
    xt = h.reshape(bsz * t_len, d)
    g_prob = jax.nn.softmax((xt @ wg + bg).astype(f32), axis=-1)
    g_top, g_idx = lax.top_k(g_prob, 1)
    e_all = (jnp.einsum('nd,gde->nge', xt, we) + be).astype(f32)
    e_logits = jnp.einsum('nge,ng->ne', e_all, jax.nn.one_hot(g_idx[:, 0], N_GROUPS, dtype=f32))
    e_top, e_idx = lax.top_k(jax.nn.softmax(e_logits, axis=-1), EXPERT_TOPK)
    e_w = e_top / jnp.sum(e_top, axis=-1, keepdims=True) * g_top
    expert = g_idx * EXPERTS_PER_GROUP + e_idx
    combine = jnp.sum(jax.nn.one_hot(expert, N_EXPERTS, dtype=f32) * e_w[..., None], axis=1)
    hid = jax.nn.silu(jnp.einsum('nd,edf->enf', xt, w1)) * jnp.einsum('nd,edf->enf', xt, w3)
    y = jnp.einsum('enf,efd->nd', hid * combine.T[:, :, None].astype(hid.dtype), w2)
    return y.reshape(bsz, t_len, d)


def setup_inputs(seed: int = 0) -> dict:
    key = jax.random.key(seed)
    ks = jax.random.split(key, 32)
    n_a = DEPTH - DEPTH // 2
    n_b = DEPTH // 2
    H, dk, dv = A_HEADS, A_DQK, A_DV
    G, dh = B_KV_HEADS, B_HEAD_DIM
    md = MEM_HEADS * MEM_HEAD_DIM

    def nrm(k, shape, fan_in):
        return jax.random.normal(k, shape, f32) * (fan_in ** -0.5)

    def gain(k, shape):
        return 1.0 + 0.02 * jax.random.normal(k, shape, f32)

    a_in = 2 * H * dk + 2 * H * dv + 2 * H
    a_b_if = jnp.concatenate([0.1 * jax.random.normal(ks[5], (n_a, H), f32),
                              3.0 + 3.0 * jax.random.uniform(ks[6], (n_a, H), f32)], axis=-1)
    return {
        'x': jax.random.normal(ks[0], (BATCH, SEQ, D_MODEL), f32),
        'mem': jax.random.normal(ks[1], (BATCH, N_MEM, D_MODEL), f32),
        'norm_g': gain(ks[2], (DEPTH, 3, D_MODEL)),
        'a_w_in': nrm(ks[3], (n_a, D_MODEL, a_in), D_MODEL),
        'a_b_if': a_b_if,
        'a_head_g': gain(ks[4], (n_a, H, dv)),
        'a_w_out': nrm(ks[7], (n_a, H * dv, D_MODEL), H * dv),
        'kv_norm_g': gain(ks[8], (D_MODEL,)),
        'kv_w': nrm(ks[9], (D_MODEL, 6 * G * dh), D_MODEL),
        'cmp_pe': 0.1 * jax.random.normal(ks[10], (2, CMP_BLOCK, dh), f32),
        'cmp_w1': nrm(ks[11], (2, CMP_BLOCK * dh, CMP_HIDDEN), CMP_BLOCK * dh),
        'cmp_w2': nrm(ks[12], (2, CMP_HIDDEN, dh), CMP_HIDDEN),
        'k_norm_g': gain(ks[13], (3, dh)),
        'b_w_in': nrm(ks[14], (n_b, D_MODEL, B_HEADS * dh + 3 * B_HEADS), D_MODEL),
        'b_q_norm_g': gain(ks[15], (n_b, dh)),
        'b_w_out': nrm(ks[16], (n_b, B_HEADS * dh, D_MODEL), B_HEADS * dh),
        'mem_norm_g': gain(ks[17], (DEPTH, D_MODEL)),
        'mem_wq': nrm(ks[18], (DEPTH, D_MODEL, md), D_MODEL),
        'mem_wk': nrm(ks[19], (DEPTH, D_MODEL, md), D_MODEL),
        'mem_wv': nrm(ks[20], (DEPTH, D_MODEL, md), D_MODEL),
        'mem_wo': nrm(ks[21], (DEPTH, md, D_MODEL), md),
        'mem_q_g': gain(ks[22], (DEPTH, MEM_HEAD_DIM)),
        'mem_k_g': gain(ks[23], (DEPTH, MEM_HEAD_DIM)),
        'moe_wg': nrm(ks[24], (DEPTH, D_MODEL, N_GROUPS), D_MODEL),
        'moe_bg': 0.01 * jax.random.normal(ks[25], (DEPTH, N_GROUPS), f32),
        'moe_we': nrm(ks[26], (DEPTH, N_GROUPS, D_MODEL, EXPERTS_PER_GROUP), D_MODEL),
        'moe_be': 0.01 * jax.random.normal(ks[27], (DEPTH, N_GROUPS, EXPERTS_PER_GROUP), f32),
        'moe_w1': nrm(ks[28], (DEPTH, N_EXPERTS, D_MODEL, D_EXPERT), D_MODEL),
        'moe_w3': nrm(ks[29], (DEPTH, N_EXPERTS, D_MODEL, D_EXPERT), D_MODEL),
        'moe_w2': nrm(ks[30], (DEPTH, N_EXPERTS, D_EXPERT, D_MODEL), D_EXPERT),
    }


def reference(x, mem, norm_g, a_w_in, a_b_if, a_head_g, a_w_out, kv_norm_g, kv_w, cmp_pe, cmp_w1, cmp_w2,
              k_norm_g, b_w_in, b_q_norm_g, b_w_out, mem_norm_g, mem_wq, mem_wk, mem_wv, mem_wo, mem_q_g,
              mem_k_g, moe_wg, moe_bg, moe_we, moe_be, moe_w1, moe_w3, moe_w2):
    n_a = DEPTH - DEPTH // 2
    shared = None
    for layer in range(DEPTH):
        if layer < n_a:
            x = x + _mlstm_mixer(_rmsnorm(x, norm_g[layer, 0]), a_w_in[layer], a_b_if[layer],
                                 a_head_g[layer], a_w_out[layer])
        else:
            if layer == n_a:
                shared = _nsa_shared_kv(x, kv_norm_g, kv_w, cmp_pe, cmp_w1, cmp_w2, k_norm_g)
            j = layer - n_a
            x = x + _nsa_mixer(_rmsnorm(x, norm_g[layer, 0]), shared, b_w_in[j], b_q_norm_g[j], b_w_out[j])
        x = x + _memory_xattn(_rmsnorm(x, norm_g[layer, 1]), mem, mem_norm_g[layer], mem_wq[layer],
                              mem_wk[layer], mem_wv[layer], mem_wo[layer], mem_q_g[layer], mem_k_g[layer])
        x = x + _hier_moe(_rmsnorm(x, norm_g[layer, 2]), moe_wg[layer], moe_bg[layer], moe_we[layer],
                          moe_be[layer], moe_w1[layer], moe_w3[layer], moe_w2[layer])
    return x
```

```python
import functools

import numpy as np
import jax
import jax.numpy as jnp
from jax import lax
from jax.experimental import pallas as pl
from jax.experimental.pallas import tpu as pltpu

f32 = jnp.float32
bf16 = jnp.bfloat16

V7X_VMEM_BYTES = 64 * 1024 * 1024
VMEM_LIMIT = V7X_VMEM_BYTES - 8 * 1024 * 1024
LANES = 128

D_MODEL = 4096
SEQ = 8192
N_MEM = 256
RMS_EPS = 1e-6
NEG_INF = -1e30
A_HEADS = 8
A_DQK = 256
A_DV = 512
A_CHUNK = 128
B_HEADS = 32
B_KV_HEADS = 4
B_HEAD_DIM = 128
B_GROUP = 8
CMP_BLOCK = 32
CMP_STRIDE = 16
CMP_HIDDEN = 256
SEL_BLOCK = 64
SEL_TOPK = 16
WINDOW = 512
Q_BLOCK = 128
SEL_FORCE = 1e6
MEM_HEADS = 4
MEM_HEAD_DIM = 128
N_GROUPS = 4
EXPERTS_PER_GROUP = 4
N_EXPERTS = 16
D_EXPERT = 1024
N_CMP_PAD = SEQ // CMP_STRIDE
N_SEL = SEQ // SEL_BLOCK


def _params(*sem):
    return pltpu.CompilerParams(dimension_semantics=sem, vmem_limit_bytes=VMEM_LIMIT)


def _rms(x, g):
    ms = jnp.mean(x * x, axis=-1, keepdims=True)
    return x * lax.rsqrt(ms + RMS_EPS) * g


def _nt(a, b):
    return lax.dot_general(a, b, (((1,), (1,)), ((), ())), preferred_element_type=f32)


def _dot(a, b):
    return jnp.dot(a, b, preferred_element_type=f32)


def _split3(a):
    a1 = a.astype(bf16)
    r1 = a - a1.astype(f32)
    a2 = r1.astype(bf16)
    a3 = (r1 - a2.astype(f32)).astype(bf16)
    return a1, a2, a3


def _dot_f32_exactrhs(a, b_bf16):
    a1, a2, a3 = _split3(a)
    return _dot(a3, b_bf16) + _dot(a2, b_bf16) + _dot(a1, b_bf16)


def _mm_kernel(*refs, norm, residual):
    it = iter(refs)
    a_ref = next(it)
    g_ref = next(it) if norm else None
    w_ref = next(it)
    r_ref = next(it) if residual else None
    o_ref = next(it)
    h_scr = next(it) if norm else None
    if norm:
        @pl.when(pl.program_id(1) == 0)
        def _():
            h_scr[...] = _rms(a_ref[...], g_ref[...]).astype(bf16)
        a = h_scr[...]
    else:
        a = a_ref[...]
    acc = _dot(a, w_ref[...])
    if residual:
        acc = acc + r_ref[...]
    o_ref[...] = acc.astype(o_ref.dtype)


def _matmul(a, w, *, gain=None, residual=None, out_dtype, tm, tn, name):
    m, k = a.shape
    n = w.shape[1]
    norm = gain is not None
    ins = [a]
    specs = [pl.BlockSpec((tm, k), lambda i, j: (i, 0))]
    if norm:
        ins.append(gain.reshape(1, k))
        specs.append(pl.BlockSpec((1, k), lambda i, j: (0, 0)))
    ins.append(w)
    specs.append(pl.BlockSpec((k, tn), lambda i, j: (0, j)))
    if residual is not None:
        ins.append(residual)
        specs.append(pl.BlockSpec((tm, tn), lambda i, j: (i, j)))
    return pl.pallas_call(
        functools.partial(_mm_kernel, norm=norm, residual=residual is not None),
        grid=(m // tm, n // tn),
        in_specs=specs,
        out_specs=pl.BlockSpec((tm, tn), lambda i, j: (i, j)),
        out_shape=jax.ShapeDtypeStruct((m, n), out_dtype),
        scratch_shapes=[pltpu.VMEM((tm, k), bf16)] if norm else [],
        compiler_params=_params("parallel", "arbitrary"),
        name=name,
    )(*ins)


def _proj_heads_kernel(a_ref, g_ref, w_ref, hg_ref, nf_ref, o_ref, h_scr, *, heads_per_tile, scale):
    @pl.when(pl.program_id(1) == 0)
    def _():
        h_scr[...] = _rms(a_ref[...], g_ref[...]).astype(bf16)
    acc = _dot(h_scr[...], w_ref[...])
    for hd in range(heads_per_tile):
        z = acc[:, hd * LANES:(hd + 1) * LANES]
        zn = _rms(z, hg_ref[hd]) * scale
        o_ref[hd] = jnp.where(nf_ref[hd] > 0.5, zn, z).astype(o_ref.dtype)


def _proj_heads(a, gain, w, head_gain, norm_flag, *, scale, tm, name):
    m, k = a.shape
    n = w.shape[1]
    hpt = 4
    tn = hpt * LANES
    nh = n // LANES
    return pl.pallas_call(
        functools.partial(_proj_heads_kernel, heads_per_tile=hpt, scale=scale),
        grid=(m // tm, n // tn),
        in_specs=[
            pl.BlockSpec((tm, k), lambda i, j: (i, 0)),
            pl.BlockSpec((1, k), lambda i, j: (0, 0)),
            pl.BlockSpec((k, tn), lambda i, j: (0, j)),
            pl.BlockSpec((hpt, 1, LANES), lambda i, j: (j, 0, 0)),
            pl.BlockSpec((hpt, 1, LANES), lambda i, j: (j, 0, 0)),
        ],
        out_specs=pl.BlockSpec((hpt, tm, LANES), lambda i, j: (j, i, 0)),
        out_shape=jax.ShapeDtypeStruct((nh, m, LANES), bf16),
        scratch_shapes=[pltpu.VMEM((tm, k), bf16)],
        compiler_params=_params("parallel", "arbitrary"),
        name=name,
    )(a, gain.reshape(1, k), w, head_gain, norm_flag)


def _mlstm_kernel(q_ref, k_ref, v_ref, og_ref, gt_ref, bif_ref, hg_ref, o_ref, c_scr, n_scr, m_scr):
    c = pl.program_id(0)
    h = pl.program_id(1)
    L, dk = A_CHUNK, A_DQK
    qscale = dk ** -0.5

    @pl.when(c == 0)
    def _():
        c_scr[h] = jnp.zeros(c_scr.shape[1:], f32)
        n_scr[h] = jnp.zeros(n_scr.shape[1:], f32)
        m_scr[h] = jnp.zeros(m_scr.shape[1:], f32)

    q = q_ref[...]
    k = k_ref[...]
    v = v_ref[...]
    li = gt_ref[pl.ds(h, 1), :] + bif_ref[pl.ds(h, 1), :]
    xf = gt_ref[pl.ds(h + A_HEADS, 1), :] + bif_ref[pl.ds(h + A_HEADS, 1), :]
    lf = jnp.minimum(xf, 0.0) - jnp.log1p(jnp.exp(-jnp.abs(xf)))

    ii = lax.broadcasted_iota(jnp.int32, (L, L), 0)
    jj = lax.broadcasted_iota(jnp.int32, (L, L), 1)
    upper = jnp.where(ii <= jj, 1.0, 0.0).astype(bf16)
    b_row = _dot_f32_exactrhs(jnp.broadcast_to(lf, (8, L)), upper)[0:1]
    eye = ii == jj

    def to_col(row):
        return jnp.sum(jnp.where(eye, row, 0.0), axis=1, keepdims=True)

    b_col = to_col(b_row)
    causal = jj <= ii
    dmat = b_col - b_row + li
    m_prev = m_scr[h]
    m_inter = b_col + m_prev
    m_t = jnp.maximum(m_inter, jnp.max(jnp.where(causal, dmat, -jnp.inf), axis=1, keepdims=True))
    dexp = jnp.where(causal, jnp.exp(dmat - m_t), 0.0)
    s = _nt(q, k) * qscale * dexp
    w_inter = jnp.exp(m_inter - m_t)
    ct = c_scr[h]
    n_row = n_scr[h]
    inter = _dot(q, ct.astype(bf16)) * qscale
    num = _dot(s.astype(bf16), v) + w_inter * inter
    qn = jnp.sum(q.astype(f32) * n_row, axis=1, keepdims=True) * qscale
    den = jnp.sum(s, axis=1, keepdims=True) + w_inter * qn
    hc = num / jnp.maximum(jnp.abs(den), jnp.exp(-m_t))

    b_end = b_row[:, L - 1:L]
    w_log = b_end - b_row + li
    m_new = jnp.maximum(b_end + m_prev, jnp.max(w_log, axis=1, keepdims=True))
    decay = jnp.exp(b_end + m_prev - m_new)
    w_wr = jnp.exp(w_log - m_new)
    vw = (v.astype(f32) * to_col(w_wr)).astype(bf16)
    kt = k.astype(f32).T.astype(bf16)
    c_scr[h] = decay * ct + _dot(kt, vw)
    n_scr[h] = decay * n_row + _dot(jnp.broadcast_to(w_wr, (8, L)).astype(bf16), k)[0:1]
    m_scr[h] = m_new

    hn = _rms(hc, hg_ref[...])
    o_ref[...] = (jax.nn.sigmoid(og_ref[...].astype(f32)) * hn).astype(o_ref.dtype)


def _mlstm(proj, gates_t, b_if, head_g):
    t_len = proj.shape[0]
    H, dk, dv, L = A_HEADS, A_DQK, A_DV, A_CHUNK
    nk = H * dk // dk
    return pl.pallas_call(
        _mlstm_kernel,
        grid=(t_len // L, H),
        in_specs=[
            pl.BlockSpec((L, dk), lambda c, h: (c, h)),
            pl.BlockSpec((L, dk), lambda c, h: (c, nk + h)),
            pl.BlockSpec((L, dv), lambda c, h: (c, 2 * H * dk // dv + h)),
            pl.BlockSpec((L, dv), lambda c, h: (c, 2 * H * dk // dv + H + h)),
            pl.BlockSpec((2 * H, L), lambda c, h: (0, c)),
            pl.BlockSpec((2 * H, 1), lambda c, h: (0, 0)),
            pl.BlockSpec((None, 1, dv), lambda c, h: (h, 0, 0)),
        ],
        out_specs=pl.BlockSpec((L, dv), lambda c, h: (c, h)),
        out_shape=jax.ShapeDtypeStruct((t_len, H * dv), bf16),
        scratch_shapes=[pltpu.VMEM((H, dk, dv), f32), pltpu.VMEM((H, 1, dk), f32), pltpu.VMEM((H, 1, 1), f32)],
        compiler_params=_params("arbitrary", "arbitrary"),
        name="mlstm",
    )(proj, proj, proj, proj, gates_t, b_if.reshape(2 * H, 1), head_g.reshape(H, 1, dv))


def _memattn_kernel(x_ref, g_ref, wq_ref, kv_ref, qg_ref, kg_ref, wo_ref, o_ref):
    x = x_ref[...]
    h = _rms(x, g_ref[...]).astype(bf16)
    qp = _dot(h, wq_ref[...])
    dh = MEM_HEAD_DIM
    nkv = MEM_HEADS * dh
    outs = []
    for hd in range(MEM_HEADS):
        qn = _rms(qp[:, hd * dh:(hd + 1) * dh], qg_ref[...]).astype(bf16)
        kn = _rms(kv_ref[:, hd * dh:(hd + 1) * dh], kg_ref[...]).astype(bf16)
        vh = kv_ref[:, nkv + hd * dh:nkv + (hd + 1) * dh].astype(bf16)
        s = _nt(qn, kn) * (dh ** -0.5)
        e = jnp.exp(s - jnp.max(s, axis=1, keepdims=True))
        p = e / jnp.sum(e, axis=1, keepdims=True)
        outs.append(_dot(p.astype(bf16), vh).astype(bf16))
    o = jnp.concatenate(outs, axis=1)
    o_ref[...] = x + _dot(o, wo_ref[...])


def _memattn(x, gain, wq, kv, qg, kg, wo, *, tm):
    t_len, d = x.shape
    md = MEM_HEADS * MEM_HEAD_DIM
    return pl.pallas_call(
        _memattn_kernel,
        grid=(t_len // tm,),
        in_specs=[
            pl.BlockSpec((tm, d), lambda i: (i, 0)),
            pl.BlockSpec((1, d), lambda i: (0, 0)),
            pl.BlockSpec((d, md), lambda i: (0, 0)),
            pl.BlockSpec((N_MEM, 2 * md), lambda i: (0, 0)),
            pl.BlockSpec((1, MEM_HEAD_DIM), lambda i: (0, 0)),
            pl.BlockSpec((1, MEM_HEAD_DIM), lambda i: (0, 0)),
            pl.BlockSpec((md, d), lambda i: (0, 0)),
        ],
        out_specs=pl.BlockSpec((tm, d), lambda i: (i, 0)),
        out_shape=jax.ShapeDtypeStruct((t_len, d), f32),
        compiler_params=_params("parallel"),
        name="memattn",
    )(x, gain.reshape(1, d), wq, kv, qg.reshape(1, -1), kg.reshape(1, -1), wo)


def _router_kernel(x_ref, g_ref, wr_ref, br_ref, h_ref, comb_ref):
    hf = _rms(x_ref[...], g_ref[...])
    h_ref[...] = hf.astype(bf16)
    h1, h2, _ = _split3(hf)
    w1, w2, _ = _split3(wr_ref[...])
    logits = _dot(h2, w1) + _dot(h1, w2) + _dot(h1, w1) + br_ref[...]
    col = [logits[:, i:i + 1] for i in range(N_GROUPS + N_EXPERTS)]
    gl = col[:N_GROUPS]
    gm = functools.reduce(jnp.maximum, gl)
    ge = [jnp.exp(z - gm) for z in gl]
    gs = functools.reduce(jnp.add, ge)
    gp = [z / gs for z in ge]
    g_top = functools.reduce(jnp.maximum, gp)
    sel_g = []
    taken = jnp.zeros(g_top.shape, jnp.bool_)
    for z in gp:
        hit = (z == g_top) & jnp.logical_not(taken)
        sel_g.append(hit)
        taken = taken | hit
    E = EXPERTS_PER_GROUP
    el = []
    for e in range(E):
        acc = jnp.zeros_like(g_top)
        for g in range(N_GROUPS):
            acc = acc + jnp.where(sel_g[g], col[N_GROUPS + g * E + e], 0.0)
        el.append(acc)
    em = functools.reduce(jnp.maximum, el)
    ee = [jnp.exp(z - em) for z in el]
    es = functools.reduce(jnp.add, ee)
    ep = [z / es for z in ee]
    sel_e = []
    for e in range(E):
        rank = jnp.zeros(g_top.shape, jnp.int32)
        for e2 in range(E):
            if e2 == e:
                continue
            ahead = (ep[e2] > ep[e]) | ((ep[e2] == ep[e]) & (e2 < e))
            rank = rank + ahead.astype(jnp.int32)
        sel_e.append(rank < 2)
    top_sum = functools.reduce(jnp.add, [jnp.where(sel_e[e], ep[e], 0.0) for e in range(E)])
    lane = lax.broadcasted_iota(jnp.int32, comb_ref.shape, 1)
    comb = jnp.zeros(comb_ref.shape, f32)
    for g in range(N_GROUPS):
        for e in range(E):
            wge = jnp.where(sel_g[g] & sel_e[e], ep[e] / top_sum * g_top, 0.0)
            comb = jnp.where(lane == g * E + e, wge, comb)
    comb_ref[...] = comb


def _router(x, gain, w_r, b_r, *, tm):
    t_len, d = x.shape
    return pl.pallas_call(
        _router_kernel,
        grid=(t_len // tm,),
        in_specs=[
            pl.BlockSpec((tm, d), lambda i: (i, 0)),
            pl.BlockSpec((1, d), lambda i: (0, 0)),
            pl.BlockSpec((d, LANES), lambda i: (0, 0)),
            pl.BlockSpec((1, LANES), lambda i: (0, 0)),
        ],
        out_specs=[pl.BlockSpec((tm, d), lambda i: (i, 0)), pl.BlockSpec((tm, LANES), lambda i: (i, 0))],
        out_shape=[jax.ShapeDtypeStruct((t_len, d), bf16), jax.ShapeDtypeStruct((t_len, LANES), f32)],
        compiler_params=_params("parallel"),
        name="moe_router",
    )(x, gain.reshape(1, d), w_r, b_r)


def _moe_hid_kernel(h_ref, w1_ref, w3_ref, cw_ref, o_ref):
    a = h_ref[...]
    u = _dot(a, w1_ref[...])
    g = _dot(a, w3_ref[...])
    o_ref[...] = (u * jax.nn.sigmoid(u) * g * cw_ref[...]).astype(o_ref.dtype)


def _moe_hid(h, w1, w3, comb_cols, *, tm, tn):
    t_len, d = h.shape
    ne, _, fdim = w1.shape
    return pl.pallas_call(
        _moe_hid_kernel,
        grid=(t_len // tm, ne, fdim // tn),
        in_specs=[
            pl.BlockSpec((tm, d), lambda i, e, j: (i, 0)),
            pl.BlockSpec((None, d, tn), lambda i, e, j: (e, 0, j)),
            pl.BlockSpec((None, d, tn), lambda i, e, j: (e, 0, j)),
            pl.BlockSpec((None, tm, 1), lambda i, e, j: (e, i, 0)),
        ],
        out_specs=pl.BlockSpec((None, tm, tn), lambda i, e, j: (e, i, j)),
        out_shape=jax.ShapeDtypeStruct((ne, t_len, fdim), bf16),
        compiler_params=_params("parallel", "parallel", "parallel"),
        name="moe_hid",
    )(h, w1, w3, comb_cols)


def _moe_out_kernel(hid_ref, w2_ref, x_ref, o_ref):
    @pl.when(pl.program_id(2) == 0)
    def _():
        o_ref[...] = x_ref[...]
    o_ref[...] += _dot(hid_ref[...], w2_ref[...])


def _moe_out(hid, w2, x, *, tm, tn):
    ne, t_len, fdim = hid.shape
    d = w2.shape[2]
    return pl.pallas_call(
        _moe_out_kernel,
        grid=(t_len // tm, d // tn, ne),
        in_specs=[
            pl.BlockSpec((None, tm, fdim), lambda i, j, e: (e, i, 0)),
            pl.BlockSpec((None, fdim, tn), lambda i, j, e: (e, 0, j)),
            pl.BlockSpec((tm, tn), lambda i, j, e: (i, j)),
        ],
        out_specs=pl.BlockSpec((tm, tn), lambda i, j, e: (i, j)),
        out_shape=jax.ShapeDtypeStruct((t_len, d), f32),
        compiler_params=_params("parallel", "parallel", "arbitrary"),
        name="moe_out",
    )(hid, w2, x)


def _hier_moe(x, gain, wg, bg, we, be, w1, w3, w2):
    d = x.shape[1]
    w_r = jnp.concatenate([wg, we.transpose(1, 0, 2).reshape(d, N_EXPERTS)], axis=1)
    w_r = jnp.pad(w_r, ((0, 0), (0, LANES - w_r.shape[1])))
    b_r = jnp.pad(jnp.concatenate([bg, be.reshape(-1)]), (0, LANES - N_GROUPS - N_EXPERTS)).reshape(1, LANES)
    h, comb = _router(x, gain, w_r, b_r, tm=256)
    comb_cols = comb[:, :N_EXPERTS].T.reshape(N_EXPERTS, x.shape[0], 1)
    hid = _moe_hid(h, w1.astype(bf16), w3.astype(bf16), comb_cols, tm=1024, tn=512)
    return _moe_out(hid, w2.astype(bf16), x, tm=1024, tn=1024)


def _compress_kernel(z_ref, pe_ref, w1_ref, w2_ref, kg_ref, o_ref):
    half = CMP_STRIDE * B_HEAD_DIM
    z = z_ref[...].astype(f32)
    zt = (z + pe_ref[:, :half]).astype(bf16)
    zb = (z + pe_ref[:, half:]).astype(bf16)
    top = _dot(zt, w1_ref[:half, :])
    bot = _dot(zb, w1_ref[half:, :])
    pre = top + pltpu.roll(bot, N_CMP_PAD - 1, axis=0)
    act = pre * (0.5 * (1.0 + jnp.tanh(np.sqrt(2.0 / np.pi).astype(np.float32) * (pre + 0.044715 * (pre * pre * pre)))))
    out = _dot(act.astype(bf16), w2_ref[...])
    normed = _rms(out, kg_ref[...])
    o_ref[...] = jnp.where(pl.program_id(0) == 0, normed, out).astype(o_ref.dtype)


def _compress(zr, pe_flat, w1, w2, kg):
    G = B_KV_HEADS
    return pl.pallas_call(
        _compress_kernel,
        grid=(2, G),
        in_specs=[
            pl.BlockSpec((None, None, N_CMP_PAD, CMP_STRIDE * B_HEAD_DIM), lambda w, g: (w, g, 0, 0)),
            pl.BlockSpec((None, 1, CMP_BLOCK * B_HEAD_DIM), lambda w, g: (w, 0, 0)),
            pl.BlockSpec((None, CMP_BLOCK * B_HEAD_DIM, CMP_HIDDEN), lambda w, g: (w, 0, 0)),
            pl.BlockSpec((None, CMP_HIDDEN, B_HEAD_DIM), lambda w, g: (w, 0, 0)),
            pl.BlockSpec((1, B_HEAD_DIM), lambda w, g: (0, 0)),
        ],
        out_specs=pl.BlockSpec((None, None, N_CMP_PAD, B_HEAD_DIM), lambda w, g: (w, g, 0, 0)),
        out_shape=jax.ShapeDtypeStruct((2, G, N_CMP_PAD, B_HEAD_DIM), bf16),
        compiler_params=_params("arbitrary", "arbitrary"),
        name="nsa_compress",
    )(zr, pe_flat, w1, w2, kg)


def _nsa_kernel(q_ref, gate_ref, sl_ref, kc_ref, vc_ref, ks_ref, vs_ref, kw_ref, vw_ref, o_ref,
                m_scr, l_scr, acc_scr):
    c = pl.program_id(1)
    R, QB, dh = B_GROUP, Q_BLOCK, B_HEAD_DIM
    rows = R * QB
    t0 = c * QB
    q = q_ref[...].reshape(rows, dh)
    sl = sl_ref[...]
    tq_col = t0 + (lax.broadcasted_iota(jnp.int32, (rows, 1), 0) % QB)

    cmp_end = lax.broadcasted_iota(jnp.int32, (1, N_CMP_PAD), 1) * CMP_STRIDE + (CMP_BLOCK - 1)
    d_c = tq_col - cmp_end
    ok_c = d_c >= 0
    s_c = jnp.where(ok_c, _nt(q, kc_ref[...]) - sl * d_c.astype(f32), NEG_INF)
    e_c = jnp.where(ok_c, jnp.exp(s_c - jnp.max(s_c, axis=1, keepdims=True)), 0.0)
    l_c = jnp.sum(e_c, axis=1, keepdims=True)
    p_c = e_c / jnp.where(l_c > 0.0, l_c, 1.0)
    o_c = _dot(p_c.astype(bf16), vc_ref[...])

    p_sum = p_c[0:QB]
    for r in range(1, R):
        p_sum = p_sum + p_c[r * QB:(r + 1) * QB]
    n_i = lax.broadcasted_iota(jnp.int32, (N_CMP_PAD, N_SEL), 0) * CMP_STRIDE
    j_i = lax.broadcasted_iota(jnp.int32, (N_CMP_PAD, N_SEL), 1) * SEL_BLOCK
    overlap = jnp.where((n_i < j_i + SEL_BLOCK) & (n_i + (CMP_BLOCK - 1) >= j_i), 1.0, 0.0).astype(bf16)
    imp = _dot_f32_exactrhs(p_sum, overlap)
    blk = lax.broadcasted_iota(jnp.int32, (QB, N_SEL), 1)
    tq = t0 + lax.broadcasted_iota(jnp.int32, (QB, N_SEL), 0)
    cur = tq // SEL_BLOCK
    forced = (blk == 0) | (blk == cur) | (blk == cur - 1)
    imp = jnp.where(forced, SEL_FORCE, imp)
    imp = jnp.where(blk * SEL_BLOCK <= tq, imp, -1.0)
    sel = jnp.zeros((QB, N_SEL), f32)
    for _ in range(SEL_TOPK):
        mx = jnp.max(imp, axis=1, keepdims=True)
        first = jnp.min(jnp.where(imp == mx, blk, N_SEL), axis=1, keepdims=True)
        pick = blk == first
        sel = jnp.where(pick & (mx >= 0.0), 1.0, sel)
        imp = jnp.where(pick, -jnp.inf, imp)
    sel_b = sel.astype(bf16)

    key_i = lax.broadcasted_iota(jnp.int32, (1, QB), 1)
    eb = lax.broadcasted_iota(jnp.int32, (N_SEL, QB), 0)
    ek = lax.broadcasted_iota(jnp.int32, (N_SEL, QB), 1) // SEL_BLOCK

    def attend(k_ref, v_ref, lo, hi, selected):
        m_scr[...] = jnp.full(m_scr.shape, NEG_INF, f32)
        l_scr[...] = jnp.zeros(l_scr.shape, f32)
        acc_scr[...] = jnp.zeros(acc_scr.shape, f32)

        def body(i, carry):
            off = pl.multiple_of(i * QB, QB)
            kb = k_ref[pl.ds(off, QB), :]
            vb = v_ref[pl.ds(off, QB), :]
            d = tq_col - (off + key_i)
            if selected:
                expand = jnp.where(eb == 2 * i + ek, 1.0, 0.0).astype(bf16)
                selk = _dot(sel_b, expand)
                ok = (jnp.concatenate([selk] * R, axis=0) > 0.5) & (d >= 0)
            else:
                ok = (d >= 0) & (d < WINDOW)
            s = jnp.where(ok, _nt(q, kb) - sl * d.astype(f32), NEG_INF)
            m_old = m_scr[...]
            m_new = jnp.maximum(m_old, jnp.max(s, axis=1, keepdims=True))
            alpha = jnp.exp(m_old - m_new)
            e = jnp.where(ok, jnp.exp(s - m_new), 0.0)
            l_scr[...] = alpha * l_scr[...] + jnp.sum(e, axis=1, keepdims=True)
            acc_scr[...] = alpha * acc_scr[...] + _dot(e.astype(bf16), vb)
            m_scr[...] = m_new
            return carry

        lax.fori_loop(lo, hi, body, 0)
        l = l_scr[...]
        return acc_scr[...] / jnp.where(l > 0.0, l, 1.0)

    o_s = attend(ks_ref, vs_ref, 0, c + 1, True)
    o_w = attend(kw_ref, vw_ref, jnp.maximum(c - WINDOW // QB, 0), c + 1, False)

    gate = jax.nn.sigmoid(gate_ref[...])

    def gate_col(branch):
        return jnp.concatenate([gate[:, branch * R + r:branch * R + r + 1] for r in range(R)], axis=0)

    o = gate_col(0) * o_c + gate_col(1) * o_s + gate_col(2) * o_w
    for r in range(R):
        o_ref[:, r * dh:(r + 1) * dh] = o[r * QB:(r + 1) * QB].astype(o_ref.dtype)


def _nsa_attention(qh, gates, slopes, cmp_kv, kvh):
    t_len = qh.shape[1]
    G, R, QB, dh = B_KV_HEADS, B_GROUP, Q_BLOCK, B_HEAD_DIM
    rows = R * QB

    def kv_spec(base):
        return pl.BlockSpec((None, t_len, dh), lambda g, c: (base + g, 0, 0))

    return pl.pallas_call(
        _nsa_kernel,
        grid=(G, t_len // QB),
        in_specs=[
            pl.BlockSpec((R, QB, dh), lambda g, c: (g, c, 0)),
            pl.BlockSpec((QB, LANES), lambda g, c: (c, g)),
            pl.BlockSpec((None, rows, 1), lambda g, c: (g, 0, 0)),
            pl.BlockSpec((None, None, N_CMP_PAD, dh), lambda g, c: (0, g, 0, 0)),
            pl.BlockSpec((None, None, N_CMP_PAD, dh), lambda g, c: (1, g, 0, 0)),
            kv_spec(2 * G), kv_spec(3 * G), kv_spec(4 * G), kv_spec(5 * G),
        ],
        out_specs=pl.BlockSpec((QB, R * dh), lambda g, c: (c, g)),
        out_shape=jax.ShapeDtypeStruct((t_len, G * R * dh), bf16),
        scratch_shapes=[pltpu.VMEM((rows, 1), f32), pltpu.VMEM((rows, 1), f32), pltpu.VMEM((rows, dh), f32)],
        compiler_params=_params("parallel", "arbitrary"),
        name="nsa_attention",
    )(qh, gates, slopes, cmp_kv, cmp_kv, kvh, kvh, kvh, kvh)


def _mlstm_layer(x, gain, w_in, b_if, head_g, w_out):
    H, dk, dv = A_HEADS, A_DQK, A_DV
    n_main = 2 * H * dk + 2 * H * dv
    w_main = w_in[:, :n_main].astype(bf16)
    w_gate = jnp.pad(w_in[:, n_main:], ((0, 0), (0, LANES - 2 * H))).astype(bf16)
    proj = _matmul(x, w_main, gain=gain, out_dtype=bf16, tm=512, tn=512, name="mlstm_in")
    gates = _matmul(x, w_gate, gain=gain, out_dtype=f32, tm=512, tn=LANES, name="mlstm_gates")
    gates_t = gates[:, :2 * H].T
    o = _mlstm(proj, gates_t, b_if, head_g)
    return _matmul(o, w_out.astype(bf16), residual=x, out_dtype=f32, tm=1024, tn=512, name="mlstm_out")


def _memattn_layer(x, gain, mem, mem_g, wq, wk, wv, wo, q_g, k_g):
    wkv = jnp.concatenate([wk, wv], axis=1).astype(bf16)
    kv = _matmul(mem, wkv, gain=mem_g, out_dtype=f32, tm=N_MEM, tn=512, name="mem_kv")
    return _memattn(x, gain, wq.astype(bf16), kv, q_g, k_g, wo.astype(bf16), tm=256)


def _alibi_slopes(n):
    return np.asarray([2.0 ** (-8.0 * (h + 1) / n) for h in range(n)], dtype=np.float32)


def _nsa_shared_kv(x, kv_norm_g, kv_w, cmp_pe, cmp_w1, cmp_w2, k_norm_g):
    G, dh = B_KV_HEADS, B_HEAD_DIM
    t_len = x.shape[0]
    ones = jnp.ones((G, 1, dh), f32)
    zeros = jnp.zeros((G, 1, dh), f32)
    head_gain = jnp.concatenate([ones, ones, ones * k_norm_g[1], ones, ones * k_norm_g[2], ones], axis=0)
    norm_flag = jnp.concatenate([zeros, zeros, ones, zeros, ones, zeros], axis=0)
    kvh = _proj_heads(x, kv_norm_g, kv_w.astype(bf16), head_gain, norm_flag, scale=1.0, tm=512, name="nsa_kv_proj")
    zr = kvh[:2 * G].reshape(2, G, t_len // CMP_STRIDE, CMP_STRIDE * dh)
    cmp_kv = _compress(zr, cmp_pe.reshape(2, 1, CMP_BLOCK * dh), cmp_w1.astype(bf16), cmp_w2.astype(bf16),
                       k_norm_g[0].reshape(1, dh))
    return cmp_kv, kvh


def _nsa_layer(x, gain, shared, w_in, q_norm_g, w_out):
    cmp_kv, kvh = shared
    H, G, R, dh = B_HEADS, B_KV_HEADS, B_GROUP, B_HEAD_DIM
    ones = jnp.ones((H, 1, dh), f32)
    qh = _proj_heads(x, gain, w_in[:, :H * dh].astype(bf16), ones * q_norm_g, ones, scale=dh ** -0.5, tm=512,
                     name="nsa_q_proj")
    wg = w_in[:, H * dh:].reshape(-1, 3, G, R).transpose(0, 2, 1, 3).reshape(-1, G, 3 * R)
    wg = jnp.pad(wg, ((0, 0), (0, 0), (0, LANES - 3 * R))).reshape(-1, G * LANES).astype(bf16)
    gates = _matmul(x, wg, gain=gain, out_dtype=f32, tm=512, tn=512, name="nsa_gates")
    slopes = jnp.asarray(np.repeat(_alibi_slopes(H).reshape(G, R, 1), Q_BLOCK, axis=2).reshape(G, R * Q_BLOCK, 1))
    o = _nsa_attention(qh, gates, slopes, cmp_kv, kvh)
    return _matmul(o, w_out.astype(bf16), residual=x, out_dtype=f32, tm=1024, tn=512, name="nsa_out")


def kernel(x, mem, norm_g, a_w_in, a_b_if, a_head_g, a_w_out, kv_norm_g, kv_w, cmp_pe, cmp_w1, cmp_w2, k_norm_g,
           b_w_in, b_q_norm_g, b_w_out, mem_norm_g, mem_wq, mem_wk, mem_wv, mem_wo, mem_q_g, mem_k_g, moe_wg,
           moe_bg, moe_we, moe_be, moe_w1, moe_w3, moe_w2):
    bsz, t_len, d = x.shape
    assert (bsz, t_len, d) == (1, SEQ, D_MODEL)
    xs = x.reshape(t_len, d)
    ms = mem.reshape(N_MEM, d)
    depth = norm_g.shape[0]
    n_a = depth - depth // 2
    shared = None
    for layer in range(depth):
        if layer < n_a:
            xs = _mlstm_layer(xs, norm_g[layer, 0], a_w_in[layer], a_b_if[layer], a_head_g[layer], a_w_out[layer])
        else:
            if layer == n_a:
                shared = _nsa_shared_kv(xs, kv_norm_g, kv_w, cmp_pe, cmp_w1, cmp_w2, k_norm_g)
            j = layer - n_a
            xs = _nsa_layer(xs, norm_g[layer, 0], shared, b_w_in[j], b_q_norm_g[j], b_w_out[j])
        xs = _memattn_layer(xs, norm_g[layer, 1], ms, mem_norm_g[layer], mem_wq[layer], mem_wk[layer],
                            mem_wv[layer], mem_wo[layer], mem_q_g[layer], mem_k_g[layer])
        xs = _hier_moe(xs, norm_g[layer, 2], moe_wg[layer], moe_bg[layer], moe_we[layer], moe_be[layer],
                       moe_w1[layer], moe_w3[layer], moe_w2[layer])
    return xs.reshape(bsz, t_len, d)
```

```python
import functools

import numpy as np
import jax
import jax.numpy as jnp
from jax import lax
from jax.experimental import pallas as pl
from jax.experimental.pallas import tpu as pltpu

f32 = jnp.float32
bf16 = jnp.bfloat16

V7X_VMEM_BYTES = 64 * 1024 * 1024
VMEM_LIMIT = V7X_VMEM_BYTES - 8 * 1024 * 1024
LANES = 128
BF16_SUBLANES = 16
LOG2E = 1.4426950408889634

D_MODEL = 4096
SEQ = 8192
N_MEM = 256
RMS_EPS = 1e-6
NEG_INF = -1e30
A_HEADS = 8
A_DQK = 256
A_DV = 512
A_CHUNK = 128
B_HEADS = 32
B_KV_HEADS = 4
B_HEAD_DIM = 128
B_GROUP = 8
CMP_BLOCK = 32
CMP_STRIDE = 16
CMP_HIDDEN = 256
SEL_BLOCK = 64
SEL_TOPK = 16
WINDOW = 512
Q_BLOCK = 128
SEL_FORCE = 1e6
MEM_HEADS = 4
MEM_HEAD_DIM = 128
N_GROUPS = 4
EXPERTS_PER_GROUP = 4
N_EXPERTS = 16
D_EXPERT = 1024
N_CMP_PAD = SEQ // CMP_STRIDE
N_SEL = SEQ // SEL_BLOCK


def _params(*sem):
    return pltpu.CompilerParams(dimension_semantics=sem, vmem_limit_bytes=VMEM_LIMIT)


def _rms(x, g):
    ms = jnp.mean(x * x, axis=-1, keepdims=True)
    return x * lax.rsqrt(ms + RMS_EPS) * g


def _nt(a, b):
    return lax.dot_general(a, b, (((1,), (1,)), ((), ())), preferred_element_type=f32)


def _dot(a, b):
    return jnp.dot(a, b, preferred_element_type=f32)


def _split3(a):
    a1 = a.astype(bf16)
    r1 = a - a1.astype(f32)
    a2 = r1.astype(bf16)
    a3 = (r1 - a2.astype(f32)).astype(bf16)
    return a1, a2, a3


def _dot_f32_exactrhs(a, b_bf16):
    a1, a2, a3 = _split3(a)
    return _dot(a3, b_bf16) + _dot(a2, b_bf16) + _dot(a1, b_bf16)


def _mm_kernel(*refs, norm, residual):
    it = iter(refs)
    a_ref = next(it)
    g_ref = next(it) if norm else None
    w_ref = next(it)
    r_ref = next(it) if residual else None
    o_ref = next(it)
    h_scr = next(it) if norm else None
    if norm:
        @pl.when(pl.program_id(1) == 0)
        def _():
            h_scr[...] = _rms(a_ref[...], g_ref[...]).astype(bf16)
        a = h_scr[...]
    else:
        a = a_ref[...]
    acc = _dot(a, w_ref[...])
    if residual:
        acc = acc + r_ref[...]
    o_ref[...] = acc.astype(o_ref.dtype)


def _matmul(a, w, *, gain=None, residual=None, out_dtype, tm, tn, name):
    m, k = a.shape
    n = w.shape[1]
    norm = gain is not None
    ins = [a]
    specs = [pl.BlockSpec((tm, k), lambda i, j: (i, 0))]
    if norm:
        ins.append(gain.reshape(1, k))
        specs.append(pl.BlockSpec((1, k), lambda i, j: (0, 0)))
    ins.append(w)
    specs.append(pl.BlockSpec((k, tn), lambda i, j: (0, j)))
    if residual is not None:
        ins.append(residual)
        specs.append(pl.BlockSpec((tm, tn), lambda i, j: (i, j)))
    return pl.pallas_call(
        functools.partial(_mm_kernel, norm=norm, residual=residual is not None),
        grid=(m // tm, n // tn),
        in_specs=specs,
        out_specs=pl.BlockSpec((tm, tn), lambda i, j: (i, j)),
        out_shape=jax.ShapeDtypeStruct((m, n), out_dtype),
        scratch_shapes=[pltpu.VMEM((tm, k), bf16)] if norm else [],
        compiler_params=_params("parallel", "arbitrary"),
        name=name,
    )(*ins)


def _proj_heads_kernel(a_ref, g_ref, w_ref, hg_ref, nf_ref, o_ref, h_scr, *, heads_per_tile, scale):
    @pl.when(pl.program_id(1) == 0)
    def _():
        h_scr[...] = _rms(a_ref[...], g_ref[...]).astype(bf16)
    acc = _dot(h_scr[...], w_ref[...])
    for hd in range(heads_per_tile):
        z = acc[:, hd * LANES:(hd + 1) * LANES]
        zn = _rms(z, hg_ref[hd]) * scale
        o_ref[hd] = jnp.where(nf_ref[hd] > 0.5, zn, z).astype(o_ref.dtype)


def _proj_heads(a, gain, w, head_gain, norm_flag, *, scale, tm, name):
    m, k = a.shape
    n = w.shape[1]
    hpt = 4
    tn = hpt * LANES
    nh = n // LANES
    return pl.pallas_call(
        functools.partial(_proj_heads_kernel, heads_per_tile=hpt, scale=scale),
        grid=(m // tm, n // tn),
        in_specs=[
            pl.BlockSpec((tm, k), lambda i, j: (i, 0)),
            pl.BlockSpec((1, k), lambda i, j: (0, 0)),
            pl.BlockSpec((k, tn), lambda i, j: (0, j)),
            pl.BlockSpec((hpt, 1, LANES), lambda i, j: (j, 0, 0)),
            pl.BlockSpec((hpt, 1, LANES), lambda i, j: (j, 0, 0)),
        ],
        out_specs=pl.BlockSpec((hpt, tm, LANES), lambda i, j: (j, i, 0)),
        out_shape=jax.ShapeDtypeStruct((nh, m, LANES), bf16),
        scratch_shapes=[pltpu.VMEM((tm, k), bf16)],
        compiler_params=_params("parallel", "arbitrary"),
        name=name,
    )(a, gain.reshape(1, k), w, head_gain, norm_flag)


def _mlstm_kernel(q_ref, k_ref, v_ref, og_ref, gt_ref, bif_ref, hg_ref, o_ref, c_scr, n_scr, m_scr):
    c = pl.program_id(0)
    h = pl.program_id(1)
    L, dk = A_CHUNK, A_DQK
    qscale = dk ** -0.5

    @pl.when(c == 0)
    def _():
        c_scr[h] = jnp.zeros(c_scr.shape[1:], f32)
        n_scr[h] = jnp.zeros(n_scr.shape[1:], f32)
        m_scr[h] = jnp.zeros(m_scr.shape[1:], f32)

    q = q_ref[...]
    k = k_ref[...]
    v = v_ref[...]
    li = gt_ref[pl.ds(h, 1), :] + bif_ref[pl.ds(h, 1), :]
    xf = gt_ref[pl.ds(h + A_HEADS, 1), :] + bif_ref[pl.ds(h + A_HEADS, 1), :]
    lf = jnp.minimum(xf, 0.0) - jnp.log1p(jnp.exp(-jnp.abs(xf)))

    ii = lax.broadcasted_iota(jnp.int32, (L, L), 0)
    jj = lax.broadcasted_iota(jnp.int32, (L, L), 1)
    upper = jnp.where(ii <= jj, 1.0, 0.0).astype(bf16)
    b_row = _dot_f32_exactrhs(jnp.broadcast_to(lf, (8, L)), upper)[0:1]
    eye = ii == jj

    def to_col(row):
        return jnp.sum(jnp.where(eye, row, 0.0), axis=1, keepdims=True)

    b_col = to_col(b_row)
    causal = jj <= ii
    dmat = b_col - b_row + li
    m_prev = m_scr[h]
    m_inter = b_col + m_prev
    m_t = jnp.maximum(m_inter, jnp.max(jnp.where(causal, dmat, -jnp.inf), axis=1, keepdims=True))
    dexp = jnp.where(causal, jnp.exp(dmat - m_t), 0.0)
    s = _nt(q, k) * qscale * dexp
    w_inter = jnp.exp(m_inter - m_t)
    ct = c_scr[h]
    n_row = n_scr[h]
    inter = _dot(q, ct.astype(bf16)) * qscale
    num = _dot(s.astype(bf16), v) + w_inter * inter
    qn = jnp.sum(q.astype(f32) * n_row, axis=1, keepdims=True) * qscale
    den = jnp.sum(s, axis=1, keepdims=True) + w_inter * qn
    hc = num / jnp.maximum(jnp.abs(den), jnp.exp(-m_t))

    b_end = b_row[:, L - 1:L]
    w_log = b_end - b_row + li
    m_new = jnp.maximum(b_end + m_prev, jnp.max(w_log, axis=1, keepdims=True))
    decay = jnp.exp(b_end + m_prev - m_new)
    w_wr = jnp.exp(w_log - m_new)
    vw = (v.astype(f32) * to_col(w_wr)).astype(bf16)
    kt = k.astype(f32).T.astype(bf16)
    c_scr[h] = decay * ct + _dot(kt, vw)
    n_scr[h] = decay * n_row + _dot(jnp.broadcast_to(w_wr, (8, L)).astype(bf16), k)[0:1]
    m_scr[h] = m_new

    hn = _rms(hc, hg_ref[...])
    o_ref[...] = (jax.nn.sigmoid(og_ref[...].astype(f32)) * hn).astype(o_ref.dtype)


def _mlstm(proj, gates_t, b_if, head_g):
    t_len = proj.shape[0]
    H, dk, dv, L = A_HEADS, A_DQK, A_DV, A_CHUNK
    nk = H * dk // dk
    return pl.pallas_call(
        _mlstm_kernel,
        grid=(t_len // L, H),
        in_specs=[
            pl.BlockSpec((L, dk), lambda c, h: (c, h)),
            pl.BlockSpec((L, dk), lambda c, h: (c, nk + h)),
            pl.BlockSpec((L, dv), lambda c, h: (c, 2 * H * dk // dv + h)),
            pl.BlockSpec((L, dv), lambda c, h: (c, 2 * H * dk // dv + H + h)),
            pl.BlockSpec((2 * H, L), lambda c, h: (0, c)),
            pl.BlockSpec((2 * H, 1), lambda c, h: (0, 0)),
            pl.BlockSpec((None, 1, dv), lambda c, h: (h, 0, 0)),
        ],
        out_specs=pl.BlockSpec((L, dv), lambda c, h: (c, h)),
        out_shape=jax.ShapeDtypeStruct((t_len, H * dv), bf16),
        scratch_shapes=[pltpu.VMEM((H, dk, dv), f32), pltpu.VMEM((H, 1, dk), f32), pltpu.VMEM((H, 1, 1), f32)],
        compiler_params=_params("arbitrary", "arbitrary"),
        name="mlstm",
    )(proj, proj, proj, proj, gates_t, b_if.reshape(2 * H, 1), head_g.reshape(H, 1, dv))


def _memattn_kernel(x_ref, g_ref, wq_ref, kv_ref, qg_ref, kg_ref, wo_ref, o_ref):
    x = x_ref[...]
    h = _rms(x, g_ref[...]).astype(bf16)
    qp = _dot(h, wq_ref[...])
    dh = MEM_HEAD_DIM
    nkv = MEM_HEADS * dh
    outs = []
    for hd in range(MEM_HEADS):
        qn = _rms(qp[:, hd * dh:(hd + 1) * dh], qg_ref[...]).astype(bf16)
        kn = _rms(kv_ref[:, hd * dh:(hd + 1) * dh], kg_ref[...]).astype(bf16)
        vh = kv_ref[:, nkv + hd * dh:nkv + (hd + 1) * dh].astype(bf16)
        s = _nt(qn, kn) * (dh ** -0.5)
        e = jnp.exp(s - jnp.max(s, axis=1, keepdims=True))
        p = e / jnp.sum(e, axis=1, keepdims=True)
        outs.append(_dot(p.astype(bf16), vh).astype(bf16))
    o = jnp.concatenate(outs, axis=1)
    o_ref[...] = x + _dot(o, wo_ref[...])


def _memattn(x, gain, wq, kv, qg, kg, wo, *, tm):
    t_len, d = x.shape
    md = MEM_HEADS * MEM_HEAD_DIM
    return pl.pallas_call(
        _memattn_kernel,
        grid=(t_len // tm,),
        in_specs=[
            pl.BlockSpec((tm, d), lambda i: (i, 0)),
            pl.BlockSpec((1, d), lambda i: (0, 0)),
            pl.BlockSpec((d, md), lambda i: (0, 0)),
            pl.BlockSpec((N_MEM, 2 * md), lambda i: (0, 0)),
            pl.BlockSpec((1, MEM_HEAD_DIM), lambda i: (0, 0)),
            pl.BlockSpec((1, MEM_HEAD_DIM), lambda i: (0, 0)),
            pl.BlockSpec((md, d), lambda i: (0, 0)),
        ],
        out_specs=pl.BlockSpec((tm, d), lambda i: (i, 0)),
        out_shape=jax.ShapeDtypeStruct((t_len, d), f32),
        compiler_params=_params("parallel"),
        name="memattn",
    )(x, gain.reshape(1, d), wq, kv, qg.reshape(1, -1), kg.reshape(1, -1), wo)


def _router_kernel(x_ref, g_ref, wr_ref, br_ref, h_ref, comb_ref):
    hf = _rms(x_ref[...], g_ref[...])
    h_ref[...] = hf.astype(bf16)
    h1, h2, _ = _split3(hf)
    w1, w2, _ = _split3(wr_ref[...])
    logits = _dot(h2, w1) + _dot(h1, w2) + _dot(h1, w1) + br_ref[...]
    col = [logits[:, i:i + 1] for i in range(N_GROUPS + N_EXPERTS)]
    gl = col[:N_GROUPS]
    gm = functools.reduce(jnp.maximum, gl)
    ge = [jnp.exp(z - gm) for z in gl]
    gs = functools.reduce(jnp.add, ge)
    gp = [z / gs for z in ge]
    g_top = functools.reduce(jnp.maximum, gp)
    sel_g = []
    taken = jnp.zeros(g_top.shape, jnp.bool_)
    for z in gp:
        hit = (z == g_top) & jnp.logical_not(taken)
        sel_g.append(hit)
        taken = taken | hit
    E = EXPERTS_PER_GROUP
    el = []
    for e in range(E):
        acc = jnp.zeros_like(g_top)
        for g in range(N_GROUPS):
            acc = acc + jnp.where(sel_g[g], col[N_GROUPS + g * E + e], 0.0)
        el.append(acc)
    em = functools.reduce(jnp.maximum, el)
    ee = [jnp.exp(z - em) for z in el]
    es = functools.reduce(jnp.add, ee)
    ep = [z / es for z in ee]
    sel_e = []
    for e in range(E):
        rank = jnp.zeros(g_top.shape, jnp.int32)
        for e2 in range(E):
            if e2 == e:
                continue
            ahead = (ep[e2] > ep[e]) | ((ep[e2] == ep[e]) & (e2 < e))
            rank = rank + ahead.astype(jnp.int32)
        sel_e.append(rank < 2)
    top_sum = functools.reduce(jnp.add, [jnp.where(sel_e[e], ep[e], 0.0) for e in range(E)])
    lane = lax.broadcasted_iota(jnp.int32, comb_ref.shape, 1)
    comb = jnp.zeros(comb_ref.shape, f32)
    for g in range(N_GROUPS):
        for e in range(E):
            wge = jnp.where(sel_g[g] & sel_e[e], ep[e] / top_sum * g_top, 0.0)
            comb = jnp.where(lane == g * E + e, wge, comb)
    comb_ref[...] = comb


def _router(x, gain, w_r, b_r, *, tm):
    t_len, d = x.shape
    return pl.pallas_call(
        _router_kernel,
        grid=(t_len // tm,),
        in_specs=[
            pl.BlockSpec((tm, d), lambda i: (i, 0)),
            pl.BlockSpec((1, d), lambda i: (0, 0)),
            pl.BlockSpec((d, LANES), lambda i: (0, 0)),
            pl.BlockSpec((1, LANES), lambda i: (0, 0)),
        ],
        out_specs=[pl.BlockSpec((tm, d), lambda i: (i, 0)), pl.BlockSpec((tm, LANES), lambda i: (i, 0))],
        out_shape=[jax.ShapeDtypeStruct((t_len, d), bf16), jax.ShapeDtypeStruct((t_len, LANES), f32)],
        compiler_params=_params("parallel"),
        name="moe_router",
    )(x, gain.reshape(1, d), w_r, b_r)


def _moe_hid_kernel(h_ref, w1_ref, w3_ref, cw_ref, o_ref):
    a = h_ref[...]
    u = _dot(a, w1_ref[...])
    g = _dot(a, w3_ref[...])
    o_ref[...] = (u * jax.nn.sigmoid(u) * g * cw_ref[...]).astype(o_ref.dtype)


def _moe_hid(h, w1, w3, comb_cols, *, tm, tn):
    t_len, d = h.shape
    ne, _, fdim = w1.shape
    return pl.pallas_call(
        _moe_hid_kernel,
        grid=(t_len // tm, ne, fdim // tn),
        in_specs=[
            pl.BlockSpec((tm, d), lambda i, e, j: (i, 0)),
            pl.BlockSpec((None, d, tn), lambda i, e, j: (e, 0, j)),
            pl.BlockSpec((None, d, tn), lambda i, e, j: (e, 0, j)),
            pl.BlockSpec((None, tm, 1), lambda i, e, j: (e, i, 0)),
        ],
        out_specs=pl.BlockSpec((None, tm, tn), lambda i, e, j: (e, i, j)),
        out_shape=jax.ShapeDtypeStruct((ne, t_len, fdim), bf16),
        compiler_params=_params("parallel", "parallel", "parallel"),
        name="moe_hid",
    )(h, w1, w3, comb_cols)


def _moe_out_kernel(hid_ref, w2_ref, x_ref, o_ref):
    @pl.when(pl.program_id(2) == 0)
    def _():
        o_ref[...] = x_ref[...]
    o_ref[...] += _dot(hid_ref[...], w2_ref[...])


def _moe_out(hid, w2, x, *, tm, tn):
    ne, t_len, fdim = hid.shape
    d = w2.shape[2]
    return pl.pallas_call(
        _moe_out_kernel,
        grid=(t_len // tm, d // tn, ne),
        in_specs=[
            pl.BlockSpec((None, tm, fdim), lambda i, j, e: (e, i, 0)),
            pl.BlockSpec((None, fdim, tn), lambda i, j, e: (e, 0, j)),
            pl.BlockSpec((tm, tn), lambda i, j, e: (i, j)),
        ],
        out_specs=pl.BlockSpec((tm, tn), lambda i, j, e: (i, j)),
        out_shape=jax.ShapeDtypeStruct((t_len, d), f32),
        compiler_params=_params("parallel", "parallel", "arbitrary"),
        name="moe_out",
    )(hid, w2, x)


def _hier_moe(x, gain, wg, bg, we, be, w1, w3, w2):
    d = x.shape[1]
    w_r = jnp.concatenate([wg, we.transpose(1, 0, 2).reshape(d, N_EXPERTS)], axis=1)
    w_r = jnp.pad(w_r, ((0, 0), (0, LANES - w_r.shape[1])))
    b_r = jnp.pad(jnp.concatenate([bg, be.reshape(-1)]), (0, LANES - N_GROUPS - N_EXPERTS)).reshape(1, LANES)
    h, comb = _router(x, gain, w_r, b_r, tm=256)
    comb_cols = comb[:, :N_EXPERTS].T.reshape(N_EXPERTS, x.shape[0], 1)
    hid = _moe_hid(h, w1.astype(bf16), w3.astype(bf16), comb_cols, tm=1024, tn=512)
    return _moe_out(hid, w2.astype(bf16), x, tm=1024, tn=1024)


def _compress_kernel(z_ref, pe_ref, w1_ref, w2_ref, kg_ref, o_ref):
    half = CMP_STRIDE * B_HEAD_DIM
    z = z_ref[...].astype(f32)
    zt = (z + pe_ref[:, :half]).astype(bf16)
    zb = (z + pe_ref[:, half:]).astype(bf16)
    top = _dot(zt, w1_ref[:half, :])
    bot = _dot(zb, w1_ref[half:, :])
    pre = top + pltpu.roll(bot, N_CMP_PAD - 1, axis=0)
    act = pre * (0.5 * (1.0 + jnp.tanh(np.sqrt(2.0 / np.pi).astype(np.float32) * (pre + 0.044715 * (pre * pre * pre)))))
    out = _dot(act.astype(bf16), w2_ref[...])
    normed = _rms(out, kg_ref[...])
    o_ref[...] = jnp.where(pl.program_id(0) == 0, normed, out).astype(o_ref.dtype)


def _compress(zr, pe_flat, w1, w2, kg):
    G = B_KV_HEADS
    return pl.pallas_call(
        _compress_kernel,
        grid=(2, G),
        in_specs=[
            pl.BlockSpec((None, None, N_CMP_PAD, CMP_STRIDE * B_HEAD_DIM), lambda w, g: (w, g, 0, 0)),
            pl.BlockSpec((None, 1, CMP_BLOCK * B_HEAD_DIM), lambda w, g: (w, 0, 0)),
            pl.BlockSpec((None, CMP_BLOCK * B_HEAD_DIM, CMP_HIDDEN), lambda w, g: (w, 0, 0)),
            pl.BlockSpec((None, CMP_HIDDEN, B_HEAD_DIM), lambda w, g: (w, 0, 0)),
            pl.BlockSpec((1, B_HEAD_DIM), lambda w, g: (0, 0)),
        ],
        out_specs=pl.BlockSpec((None, None, N_CMP_PAD, B_HEAD_DIM), lambda w, g: (w, g, 0, 0)),
        out_shape=jax.ShapeDtypeStruct((2, G, N_CMP_PAD, B_HEAD_DIM), bf16),
        compiler_params=_params("arbitrary", "arbitrary"),
        name="nsa_compress",
    )(zr, pe_flat, w1, w2, kg)


def _key_extra(pos):
    lane = lax.broadcasted_iota(jnp.int32, pos.shape, 1)
    lo = pos & (LANES - 1)
    return jnp.where(lane < 3, lo.astype(f32), jnp.where(lane < 6, (pos - lo).astype(f32), 0.0)).astype(bf16)


def _nsa_kernel(q_ref, gate_ref, sl_ref, kc_ref, vct_ref, ks_ref, vst_ref, kw_ref, vwt_ref, o_ref,
                qa_scr, selb_scr, m_scr, acc_scr, out_scr):
    c = pl.program_id(1)
    R, QB, dh = B_GROUP, Q_BLOCK, B_HEAD_DIM
    rows = R * QB
    t0 = c * QB
    slabs = [slice(r * QB, (r + 1) * QB) for r in range(R)]

    qa_scr[0:dh, :] = jnp.concatenate([q_ref[r].astype(f32).T for r in range(R)], axis=1).astype(bf16)
    s1, s2, s3 = [p.astype(f32) for p in _split3(sl_ref[...] * LOG2E)]
    rid = lax.broadcasted_iota(jnp.int32, (dh, rows), 0)
    qa_scr[dh:2 * dh, :] = jnp.where((rid == 0) | (rid == 3), s1,
                                     jnp.where((rid == 1) | (rid == 4), s2,
                                               jnp.where((rid == 2) | (rid == 5), s3, 0.0))).astype(bf16)

    kidx = lax.broadcasted_iota(jnp.int32, (QB, QB), 0)
    tidx = lax.broadcasted_iota(jnp.int32, (QB, QB), 1)
    causal = kidx <= tidx

    gate_t = jax.nn.sigmoid(gate_ref[...]).T

    def scores(k_rows, pos):
        return _dot(jnp.concatenate([k_rows, _key_extra(pos)], axis=1), qa_scr[...])

    def first_chunk(s_t, ok, v_t):
        ps = []
        for r in range(R):
            s_r = jnp.where(ok, s_t[:, slabs[r]], NEG_INF)
            m = jnp.max(s_r, axis=0, keepdims=True)
            m_scr[:, slabs[r]] = m
            ps.append(jnp.where(ok, jnp.exp2(s_r - m), 0.0).astype(bf16))
        acc_scr[...] = _dot(v_t, jnp.concatenate(ps, axis=1))

    def next_chunk(s_t, mask_bias, v_t):
        ps, alphas = [], []
        for r in range(R):
            s_r = s_t[:, slabs[r]] + mask_bias
            m_old = m_scr[:, slabs[r]]
            m_new = jnp.maximum(m_old, jnp.max(s_r, axis=0, keepdims=True))
            m_scr[:, slabs[r]] = m_new
            alphas.append(jnp.exp2(m_old - m_new))
            ps.append(jnp.exp2(s_r - m_new).astype(bf16))
        pv = _dot(v_t, jnp.concatenate(ps, axis=1))
        for r in range(R):
            acc_scr[:, slabs[r]] = alphas[r] * acc_scr[:, slabs[r]] + pv[:, slabs[r]]

    def emit(branch):
        for r in range(R):
            w = gate_t[branch * R + r:branch * R + r + 1, :] / acc_scr[dh:dh + 1, slabs[r]]
            out_scr[:, slabs[r]] = out_scr[:, slabs[r]] + acc_scr[0:dh, slabs[r]] * w

    cend = lax.broadcasted_iota(jnp.int32, (N_CMP_PAD, QB), 0) * CMP_STRIDE + (CMP_BLOCK - 1)
    ok_c = cend <= t0 + lax.broadcasted_iota(jnp.int32, (N_CMP_PAD, QB), 1)
    s_c = scores(kc_ref[...], cend)
    p_sum = jnp.zeros((N_CMP_PAD, QB), f32)
    ps = []
    for r in range(R):
        s_r = jnp.where(ok_c, s_c[:, slabs[r]], NEG_INF)
        e = jnp.where(ok_c, jnp.exp2(s_r - jnp.max(s_r, axis=0, keepdims=True)), 0.0)
        l = jnp.sum(e, axis=0, keepdims=True)
        p = e / jnp.where(l > 0.0, l, 1.0)
        p_sum = p_sum + p
        ps.append(p.astype(bf16))
    oc_t = _dot(vct_ref[...], jnp.concatenate(ps, axis=1))
    for r in range(R):
        out_scr[:, slabs[r]] = oc_t[:, slabs[r]] * gate_t[r:r + 1, :]

    b_i = lax.broadcasted_iota(jnp.int32, (N_SEL, N_CMP_PAD), 0) * SEL_BLOCK
    n_i = lax.broadcasted_iota(jnp.int32, (N_SEL, N_CMP_PAD), 1) * CMP_STRIDE
    overlap_t = jnp.where((n_i < b_i + SEL_BLOCK) & (n_i + (CMP_BLOCK - 1) >= b_i), 1.0, 0.0).astype(bf16)
    p1, p2, p3 = _split3(p_sum)
    imp = _dot(overlap_t, p3) + _dot(overlap_t, p2) + _dot(overlap_t, p1)
    tq = t0 + tidx
    cur = tq // SEL_BLOCK
    forced = (kidx == 0) | (kidx == cur) | (kidx == cur - 1)
    imp = jnp.where(forced, SEL_FORCE, imp)
    imp = jnp.where(kidx * SEL_BLOCK <= tq, imp, -1.0)
    blk_f = kidx.astype(f32)
    sel = jnp.zeros((N_SEL, QB), f32)
    for _ in range(SEL_TOPK):
        mx = jnp.max(imp, axis=0, keepdims=True)
        first = jnp.min(jnp.where(imp == mx, blk_f, float(N_SEL)), axis=0, keepdims=True)
        pick = blk_f == first
        sel = jnp.where(pick & (mx >= 0.0), 1.0, sel)
        imp = jnp.where(pick, -jnp.inf, imp)
    selb_scr[...] = jnp.where(sel > 0.5, 0.0, NEG_INF)

    def sel_bias(first_block, n_blocks):
        return jnp.concatenate([jnp.broadcast_to(selb_scr[pl.ds(first_block + b, 1), :], (SEL_BLOCK, QB))
                                for b in range(n_blocks)], axis=0)

    def chunk_pos(i):
        return i * QB + kidx

    off_c = pl.multiple_of(t0, QB)
    first_chunk(scores(ks_ref[pl.ds(off_c, QB), :], chunk_pos(c)), causal & (sel_bias(2 * c, 2) > -1.0),
                vst_ref[c])
    pos2 = lax.broadcasted_iota(jnp.int32, (2 * QB, QB), 0)

    def sel_body(j, carry):
        off = pl.multiple_of(j * 2 * QB, 2 * QB)
        v_t = jnp.concatenate([vst_ref[2 * j], vst_ref[2 * j + 1]], axis=1)
        next_chunk(scores(ks_ref[pl.ds(off, 2 * QB), :], off + pos2), sel_bias(4 * j, 4), v_t)
        return carry

    lax.fori_loop(0, c // 2, sel_body, 0)

    @pl.when(c % 2 == 1)
    def _():
        i = c - 1
        off = pl.multiple_of(i * QB, QB)
        next_chunk(scores(ks_ref[pl.ds(off, QB), :], chunk_pos(i)), sel_bias(2 * i, 2), vst_ref[i])

    emit(1)

    first_chunk(scores(kw_ref[pl.ds(off_c, QB), :], chunk_pos(c)), causal, vwt_ref[c])
    n_back = WINDOW // QB
    oldest_bias = jnp.where(kidx > tidx, 0.0, NEG_INF)

    def win_body(i, carry):
        off = pl.multiple_of(i * QB, QB)
        bias = jnp.where(i == c - n_back, oldest_bias, 0.0)
        next_chunk(scores(kw_ref[pl.ds(off, QB), :], chunk_pos(i)), bias, vwt_ref[i])
        return carry

    lax.fori_loop(jnp.maximum(c - n_back, 0), c, win_body, 0)
    emit(2)

    for r in range(R):
        o_ref[:, r * dh:(r + 1) * dh] = out_scr[:, slabs[r]].T.astype(o_ref.dtype)


def _nsa_attention(qh, gates, slopes, cmp_kv, kvh):
    t_len = qh.shape[1]
    G, R, QB, dh = B_KV_HEADS, B_GROUP, Q_BLOCK, B_HEAD_DIM
    rows = R * QB
    nq = t_len // QB
    kc = cmp_kv[0]
    vc_t = cmp_kv[1].swapaxes(1, 2)
    ones_rows = jnp.zeros((G, nq, BF16_SUBLANES, QB), bf16).at[:, :, 0, :].set(1.0)

    def values_t(v):
        return jnp.concatenate([v.reshape(G, nq, QB, dh).swapaxes(2, 3), ones_rows], axis=2)

    vs_t = values_t(kvh[3 * G:4 * G])
    vw_t = values_t(kvh[5 * G:6 * G])
    dha = dh + BF16_SUBLANES

    def k_spec(base):
        return pl.BlockSpec((None, t_len, dh), lambda g, c: (base + g, 0, 0))

    vt_spec = pl.BlockSpec((None, nq, dha, QB), lambda g, c: (g, 0, 0, 0))
    return pl.pallas_call(
        _nsa_kernel,
        grid=(G, nq),
        in_specs=[
            pl.BlockSpec((R, QB, dh), lambda g, c: (g, c, 0)),
            pl.BlockSpec((QB, LANES), lambda g, c: (c, g)),
            pl.BlockSpec((None, 1, rows), lambda g, c: (g, 0, 0)),
            pl.BlockSpec((None, N_CMP_PAD, dh), lambda g, c: (g, 0, 0)),
            pl.BlockSpec((None, dh, N_CMP_PAD), lambda g, c: (g, 0, 0)),
            k_spec(2 * G), vt_spec, k_spec(4 * G), vt_spec,
        ],
        out_specs=pl.BlockSpec((QB, R * dh), lambda g, c: (c, g)),
        out_shape=jax.ShapeDtypeStruct((t_len, G * R * dh), bf16),
        scratch_shapes=[
            pltpu.VMEM((2 * dh, rows), bf16),
            pltpu.VMEM((N_SEL, QB), f32),
            pltpu.VMEM((1, rows), f32),
            pltpu.VMEM((dha, rows), f32),
            pltpu.VMEM((dh, rows), f32),
        ],
        compiler_params=_params("parallel", "arbitrary"),
        name="nsa_attention",
    )(qh, gates, slopes, kc, vc_t, kvh, vs_t, kvh, vw_t)


def _mlstm_layer(x, gain, w_in, b_if, head_g, w_out):
    H, dk, dv = A_HEADS, A_DQK, A_DV
    n_main = 2 * H * dk + 2 * H * dv
    w_main = w_in[:, :n_main].astype(bf16)
    w_gate = jnp.pad(w_in[:, n_main:], ((0, 0), (0, LANES - 2 * H))).astype(bf16)
    proj = _matmul(x, w_main, gain=gain, out_dtype=bf16, tm=512, tn=512, name="mlstm_in")
    gates = _matmul(x, w_gate, gain=gain, out_dtype=f32, tm=512, tn=LANES, name="mlstm_gates")
    gates_t = gates[:, :2 * H].T
    o = _mlstm(proj, gates_t, b_if, head_g)
    return _matmul(o, w_out.astype(bf16), residual=x, out_dtype=f32, tm=1024, tn=512, name="mlstm_out")


def _memattn_layer(x, gain, mem, mem_g, wq, wk, wv, wo, q_g, k_g):
    wkv = jnp.concatenate([wk, wv], axis=1).astype(bf16)
    kv = _matmul(mem, wkv, gain=mem_g, out_dtype=f32, tm=N_MEM, tn=512, name="mem_kv")
    return _memattn(x, gain, wq.astype(bf16), kv, q_g, k_g, wo.astype(bf16), tm=256)


def _alibi_slopes(n):
    return np.asarray([2.0 ** (-8.0 * (h + 1) / n) for h in range(n)], dtype=np.float32)


def _nsa_shared_kv(x, kv_norm_g, kv_w, cmp_pe, cmp_w1, cmp_w2, k_norm_g):
    G, dh = B_KV_HEADS, B_HEAD_DIM
    t_len = x.shape[0]
    ones = jnp.ones((G, 1, dh), f32)
    zeros = jnp.zeros((G, 1, dh), f32)
    head_gain = jnp.concatenate([ones, ones, ones * k_norm_g[1], ones, ones * k_norm_g[2], ones], axis=0)
    norm_flag = jnp.concatenate([zeros, zeros, ones, zeros, ones, zeros], axis=0)
    kvh = _proj_heads(x, kv_norm_g, kv_w.astype(bf16), head_gain, norm_flag, scale=1.0, tm=512, name="nsa_kv_proj")
    zr = kvh[:2 * G].reshape(2, G, t_len // CMP_STRIDE, CMP_STRIDE * dh)
    cmp_kv = _compress(zr, cmp_pe.reshape(2, 1, CMP_BLOCK * dh), cmp_w1.astype(bf16), cmp_w2.astype(bf16),
                       k_norm_g[0].reshape(1, dh))
    return cmp_kv, kvh


def _nsa_layer(x, gain, shared, w_in, q_norm_g, w_out):
    cmp_kv, kvh = shared
    H, G, R, dh = B_HEADS, B_KV_HEADS, B_GROUP, B_HEAD_DIM
    ones = jnp.ones((H, 1, dh), f32)
    qh = _proj_heads(x, gain, w_in[:, :H * dh].astype(bf16), ones * q_norm_g, ones, scale=dh ** -0.5 * LOG2E,
                     tm=512, name="nsa_q_proj")
    wg = w_in[:, H * dh:].reshape(-1, 3, G, R).transpose(0, 2, 1, 3).reshape(-1, G, 3 * R)
    wg = jnp.pad(wg, ((0, 0), (0, 0), (0, LANES - 3 * R))).reshape(-1, G * LANES).astype(bf16)
    gates = _matmul(x, wg, gain=gain, out_dtype=f32, tm=512, tn=512, name="nsa_gates")
    slopes = jnp.asarray(np.repeat(_alibi_slopes(H).reshape(G, R, 1), Q_BLOCK, axis=2).reshape(G, 1, R * Q_BLOCK))
    o = _nsa_attention(qh, gates, slopes, cmp_kv, kvh)
    return _matmul(o, w_out.astype(bf16), residual=x, out_dtype=f32, tm=1024, tn=512, name="nsa_out")


def kernel(x, mem, norm_g, a_w_in, a_b_if, a_head_g, a_w_out, kv_norm_g, kv_w, cmp_pe, cmp_w1, cmp_w2, k_norm_g,
           b_w_in, b_q_norm_g, b_w_out, mem_norm_g, mem_wq, mem_wk, mem_wv, mem_wo, mem_q_g, mem_k_g, moe_wg,
           moe_bg, moe_we, moe_be, moe_w1, moe_w3, moe_w2):
    bsz, t_len, d = x.shape
    assert (bsz, t_len, d) == (1, SEQ, D_MODEL)
    xs = x.reshape(t_len, d)
    ms = mem.reshape(N_MEM, d)
    depth = norm_g.shape[0]
    n_a = depth - depth // 2
    shared = None
    for layer in range(depth):
        if layer < n_a:
            xs = _mlstm_layer(xs, norm_g[layer, 0], a_w_in[layer], a_b_if[layer], a_head_g[layer], a_w_out[layer])
        else:
            if layer == n_a:
                shared = _nsa_shared_kv(xs, kv_norm_g, kv_w, cmp_pe, cmp_w1, cmp_w2, k_norm_g)
            j = layer - n_a
            xs = _nsa_layer(xs, norm_g[layer, 0], shared, b_w_in[j], b_q_norm_g[j], b_w_out[j])
        xs = _memattn_layer(xs, norm_g[layer, 1], ms, mem_norm_g[layer], mem_wq[layer], mem_wk[layer],
                            mem_wv[layer], mem_wo[layer], mem_q_g[layer], mem_k_g[layer])
        xs = _hier_moe(xs, norm_g[layer, 2], moe_wg[layer], moe_bg[layer], moe_we[layer], moe_be[layer],
                       moe_w1[layer], moe_w3[layer], moe_w2[layer])
    return xs.reshape(bsz, t_len, d)
```

```python
import functools

import numpy as np
import jax
import jax.numpy as jnp
from jax import lax
from jax.experimental import pallas as pl
from jax.experimental.pallas import tpu as pltpu

f32 = jnp.float32
bf16 = jnp.bfloat16

V7X_VMEM_BYTES = 64 * 1024 * 1024
VMEM_LIMIT = V7X_VMEM_BYTES - 8 * 1024 * 1024
LANES = 128
BF16_SUBLANES = 16
LOG2E = 1.4426950408889634

D_MODEL = 4096
SEQ = 8192
N_MEM = 256
RMS_EPS = 1e-6
NEG_INF = -1e30
A_HEADS = 8
A_DQK = 256
A_DV = 512
A_CHUNK = 128
B_HEADS = 32
B_KV_HEADS = 4
B_HEAD_DIM = 128
B_GROUP = 8
CMP_BLOCK = 32
CMP_STRIDE = 16
CMP_HIDDEN = 256
SEL_BLOCK = 64
SEL_TOPK = 16
WINDOW = 512
Q_BLOCK = 128
SEL_FORCE = 1e6
MEM_HEADS = 4
MEM_HEAD_DIM = 128
N_GROUPS = 4
EXPERTS_PER_GROUP = 4
N_EXPERTS = 16
D_EXPERT = 1024
EXPERT_TOPK = 2
N_CMP_PAD = SEQ // CMP_STRIDE
N_SEL = SEQ // SEL_BLOCK
SLABS = D_MODEL // LANES
MOE_TM = 256
MOE_ROWS = EXPERT_TOPK * SEQ + N_EXPERTS * MOE_TM
MOE_TILES = MOE_ROWS // MOE_TM


def _params(*sem):
    return pltpu.CompilerParams(dimension_semantics=sem, vmem_limit_bytes=VMEM_LIMIT)


def _rms(x, g):
    ms = jnp.mean(x * x, axis=-1, keepdims=True)
    return x * lax.rsqrt(ms + RMS_EPS) * g


def _nt(a, b):
    return lax.dot_general(a, b, (((1,), (1,)), ((), ())), preferred_element_type=f32)


def _dot(a, b):
    return jnp.dot(a, b, preferred_element_type=f32)


def _split3(a):
    a1 = a.astype(bf16)
    r1 = a - a1.astype(f32)
    a2 = r1.astype(bf16)
    a3 = (r1 - a2.astype(f32)).astype(bf16)
    return a1, a2, a3


def _dot_f32_exactrhs(a, b_bf16):
    a1, a2, a3 = _split3(a)
    return _dot(a3, b_bf16) + _dot(a2, b_bf16) + _dot(a1, b_bf16)


def _mm_kernel(*refs, norm, residual):
    it = iter(refs)
    a_ref = next(it)
    g_ref = next(it) if norm else None
    w_ref = next(it)
    r_ref = next(it) if residual else None
    o_ref = next(it)
    h_scr = next(it) if norm else None
    if norm:
        @pl.when(pl.program_id(1) == 0)
        def _():
            h_scr[...] = _rms(a_ref[...], g_ref[...]).astype(bf16)
        a = h_scr[...]
    else:
        a = a_ref[...]
    acc = _dot(a, w_ref[...])
    if residual:
        acc = acc + r_ref[...]
    o_ref[...] = acc.astype(o_ref.dtype)


def _matmul(a, w, *, gain=None, residual=None, out_dtype, tm, tn, name):
    m, k = a.shape
    n = w.shape[1]
    norm = gain is not None
    ins = [a]
    specs = [pl.BlockSpec((tm, k), lambda i, j: (i, 0))]
    if norm:
        ins.append(gain.reshape(1, k))
        specs.append(pl.BlockSpec((1, k), lambda i, j: (0, 0)))
    ins.append(w)
    specs.append(pl.BlockSpec((k, tn), lambda i, j: (0, j)))
    if residual is not None:
        ins.append(residual)
        specs.append(pl.BlockSpec((tm, tn), lambda i, j: (i, j)))
    return pl.pallas_call(
        functools.partial(_mm_kernel, norm=norm, residual=residual is not None),
        grid=(m // tm, n // tn),
        in_specs=specs,
        out_specs=pl.BlockSpec((tm, tn), lambda i, j: (i, j)),
        out_shape=jax.ShapeDtypeStruct((m, n), out_dtype),
        scratch_shapes=[pltpu.VMEM((tm, k), bf16)] if norm else [],
        compiler_params=_params("parallel", "arbitrary"),
        name=name,
    )(*ins)


def _proj_heads_kernel(a_ref, g_ref, w_ref, hg_ref, nf_ref, o_ref, h_scr, *, heads_per_tile, scale):
    @pl.when(pl.program_id(1) == 0)
    def _():
        h_scr[...] = _rms(a_ref[...], g_ref[...]).astype(bf16)
    acc = _dot(h_scr[...], w_ref[...])
    for hd in range(heads_per_tile):
        z = acc[:, hd * LANES:(hd + 1) * LANES]
        zn = _rms(z, hg_ref[hd]) * scale
        o_ref[hd] = jnp.where(nf_ref[hd] > 0.5, zn, z).astype(o_ref.dtype)


def _proj_heads(a, gain, w, head_gain, norm_flag, *, scale, tm, name):
    m, k = a.shape
    n = w.shape[1]
    hpt = 4
    tn = hpt * LANES
    nh = n // LANES
    return pl.pallas_call(
        functools.partial(_proj_heads_kernel, heads_per_tile=hpt, scale=scale),
        grid=(m // tm, n // tn),
        in_specs=[
            pl.BlockSpec((tm, k), lambda i, j: (i, 0)),
            pl.BlockSpec((1, k), lambda i, j: (0, 0)),
            pl.BlockSpec((k, tn), lambda i, j: (0, j)),
            pl.BlockSpec((hpt, 1, LANES), lambda i, j: (j, 0, 0)),
            pl.BlockSpec((hpt, 1, LANES), lambda i, j: (j, 0, 0)),
        ],
        out_specs=pl.BlockSpec((hpt, tm, LANES), lambda i, j: (j, i, 0)),
        out_shape=jax.ShapeDtypeStruct((nh, m, LANES), bf16),
        scratch_shapes=[pltpu.VMEM((tm, k), bf16)],
        compiler_params=_params("parallel", "arbitrary"),
        name=name,
    )(a, gain.reshape(1, k), w, head_gain, norm_flag)


def _mlstm_kernel(q_ref, k_ref, v_ref, og_ref, gt_ref, bif_ref, hg_ref, o_ref, c_scr, n_scr, m_scr):
    c = pl.program_id(0)
    h = pl.program_id(1)
    L, dk = A_CHUNK, A_DQK
    qscale = dk ** -0.5

    @pl.when(c == 0)
    def _():
        c_scr[h] = jnp.zeros(c_scr.shape[1:], f32)
        n_scr[h] = jnp.zeros(n_scr.shape[1:], f32)
        m_scr[h] = jnp.zeros(m_scr.shape[1:], f32)

    q = q_ref[...]
    k = k_ref[...]
    v = v_ref[...]
    li = gt_ref[pl.ds(h, 1), :] + bif_ref[pl.ds(h, 1), :]
    xf = gt_ref[pl.ds(h + A_HEADS, 1), :] + bif_ref[pl.ds(h + A_HEADS, 1), :]
    lf = jnp.minimum(xf, 0.0) - jnp.log1p(jnp.exp(-jnp.abs(xf)))

    ii = lax.broadcasted_iota(jnp.int32, (L, L), 0)
    jj = lax.broadcasted_iota(jnp.int32, (L, L), 1)
    upper = jnp.where(ii <= jj, 1.0, 0.0).astype(bf16)
    b_row = _dot_f32_exactrhs(jnp.broadcast_to(lf, (8, L)), upper)[0:1]
    eye = ii == jj

    def to_col(row):
        return jnp.sum(jnp.where(eye, row, 0.0), axis=1, keepdims=True)

    b_col = to_col(b_row)
    causal = jj <= ii
    dmat = b_col - b_row + li
    m_prev = m_scr[h]
    m_inter = b_col + m_prev
    m_t = jnp.maximum(m_inter, jnp.max(jnp.where(causal, dmat, -jnp.inf), axis=1, keepdims=True))
    dexp = jnp.where(causal, jnp.exp(dmat - m_t), 0.0)
    s = _nt(q, k) * qscale * dexp
    w_inter = jnp.exp(m_inter - m_t)
    ct = c_scr[h]
    n_row = n_scr[h]
    inter = _dot(q, ct.astype(bf16)) * qscale
    num = _dot(s.astype(bf16), v) + w_inter * inter
    qn = jnp.sum(q.astype(f32) * n_row, axis=1, keepdims=True) * qscale
    den = jnp.sum(s, axis=1, keepdims=True) + w_inter * qn
    hc = num / jnp.maximum(jnp.abs(den), jnp.exp(-m_t))

    b_end = b_row[:, L - 1:L]
    w_log = b_end - b_row + li
    m_new = jnp.maximum(b_end + m_prev, jnp.max(w_log, axis=1, keepdims=True))
    decay = jnp.exp(b_end + m_prev - m_new)
    w_wr = jnp.exp(w_log - m_new)
    vw = (v.astype(f32) * to_col(w_wr)).astype(bf16)
    kt = k.astype(f32).T.astype(bf16)
    c_scr[h] = decay * ct + _dot(kt, vw)
    n_scr[h] = decay * n_row + _dot(jnp.broadcast_to(w_wr, (8, L)).astype(bf16), k)[0:1]
    m_scr[h] = m_new

    hn = _rms(hc, hg_ref[...])
    o_ref[...] = (jax.nn.sigmoid(og_ref[...].astype(f32)) * hn).astype(o_ref.dtype)


def _mlstm(proj, gates_t, b_if, head_g):
    t_len = proj.shape[0]
    H, dk, dv, L = A_HEADS, A_DQK, A_DV, A_CHUNK
    nk = H * dk // dk
    return pl.pallas_call(
        _mlstm_kernel,
        grid=(t_len // L, H),
        in_specs=[
            pl.BlockSpec((L, dk), lambda c, h: (c, h)),
            pl.BlockSpec((L, dk), lambda c, h: (c, nk + h)),
            pl.BlockSpec((L, dv), lambda c, h: (c, 2 * H * dk // dv + h)),
            pl.BlockSpec((L, dv), lambda c, h: (c, 2 * H * dk // dv + H + h)),
            pl.BlockSpec((2 * H, L), lambda c, h: (0, c)),
            pl.BlockSpec((2 * H, 1), lambda c, h: (0, 0)),
            pl.BlockSpec((None, 1, dv), lambda c, h: (h, 0, 0)),
        ],
        out_specs=pl.BlockSpec((L, dv), lambda c, h: (c, h)),
        out_shape=jax.ShapeDtypeStruct((t_len, H * dv), bf16),
        scratch_shapes=[pltpu.VMEM((H, dk, dv), f32), pltpu.VMEM((H, 1, dk), f32), pltpu.VMEM((H, 1, 1), f32)],
        compiler_params=_params("arbitrary", "arbitrary"),
        name="mlstm",
    )(proj, proj, proj, proj, gates_t, b_if.reshape(2 * H, 1), head_g.reshape(H, 1, dv))


def _memattn_kernel(x_ref, g_ref, wq_ref, kv_ref, qg_ref, kg_ref, wo_ref, o_ref):
    x = x_ref[...]
    h = _rms(x, g_ref[...]).astype(bf16)
    qp = _dot(h, wq_ref[...])
    dh = MEM_HEAD_DIM
    nkv = MEM_HEADS * dh
    outs = []
    for hd in range(MEM_HEADS):
        qn = _rms(qp[:, hd * dh:(hd + 1) * dh], qg_ref[...]).astype(bf16)
        kn = _rms(kv_ref[:, hd * dh:(hd + 1) * dh], kg_ref[...]).astype(bf16)
        vh = kv_ref[:, nkv + hd * dh:nkv + (hd + 1) * dh].astype(bf16)
        s = _nt(qn, kn) * (dh ** -0.5)
        e = jnp.exp(s - jnp.max(s, axis=1, keepdims=True))
        p = e / jnp.sum(e, axis=1, keepdims=True)
        outs.append(_dot(p.astype(bf16), vh).astype(bf16))
    o = jnp.concatenate(outs, axis=1)
    o_ref[...] = x + _dot(o, wo_ref[...])


def _memattn(x, gain, wq, kv, qg, kg, wo, *, tm):
    t_len, d = x.shape
    md = MEM_HEADS * MEM_HEAD_DIM
    return pl.pallas_call(
        _memattn_kernel,
        grid=(t_len // tm,),
        in_specs=[
            pl.BlockSpec((tm, d), lambda i: (i, 0)),
            pl.BlockSpec((1, d), lambda i: (0, 0)),
            pl.BlockSpec((d, md), lambda i: (0, 0)),
            pl.BlockSpec((N_MEM, 2 * md), lambda i: (0, 0)),
            pl.BlockSpec((1, MEM_HEAD_DIM), lambda i: (0, 0)),
            pl.BlockSpec((1, MEM_HEAD_DIM), lambda i: (0, 0)),
            pl.BlockSpec((md, d), lambda i: (0, 0)),
        ],
        out_specs=pl.BlockSpec((tm, d), lambda i: (i, 0)),
        out_shape=jax.ShapeDtypeStruct((t_len, d), f32),
        compiler_params=_params("parallel"),
        name="memattn",
    )(x, gain.reshape(1, d), wq, kv, qg.reshape(1, -1), kg.reshape(1, -1), wo)


def _router_kernel(x_ref, g_ref, wr_ref, br_ref, h8_ref, comb_ref, rank_ref, cnt_ref, carry_scr):
    tm = x_ref.shape[0]

    @pl.when(pl.program_id(0) == 0)
    def _():
        carry_scr[...] = jnp.zeros(carry_scr.shape, f32)

    hf = _rms(x_ref[...], g_ref[...])
    _store_slabs(h8_ref, hf)
    h1, h2, _ = _split3(hf)
    w1, w2, _ = _split3(wr_ref[...])
    logits = _dot(h2, w1) + _dot(h1, w2) + _dot(h1, w1) + br_ref[...]
    col = [logits[:, i:i + 1] for i in range(N_GROUPS + N_EXPERTS)]
    gl = col[:N_GROUPS]
    gm = functools.reduce(jnp.maximum, gl)
    ge = [jnp.exp(z - gm) for z in gl]
    gs = functools.reduce(jnp.add, ge)
    gp = [z / gs for z in ge]
    g_top = functools.reduce(jnp.maximum, gp)
    sel_g = []
    taken = jnp.zeros(g_top.shape, jnp.bool_)
    for z in gp:
        hit = (z == g_top) & jnp.logical_not(taken)
        sel_g.append(hit)
        taken = taken | hit
    E = EXPERTS_PER_GROUP
    el = []
    for e in range(E):
        acc = jnp.zeros_like(g_top)
        for g in range(N_GROUPS):
            acc = acc + jnp.where(sel_g[g], col[N_GROUPS + g * E + e], 0.0)
        el.append(acc)
    em = functools.reduce(jnp.maximum, el)
    ee = [jnp.exp(z - em) for z in el]
    es = functools.reduce(jnp.add, ee)
    ep = [z / es for z in ee]
    sel_e = []
    for e in range(E):
        rank = jnp.zeros(g_top.shape, jnp.int32)
        for e2 in range(E):
            if e2 == e:
                continue
            ahead = (ep[e2] > ep[e]) | ((ep[e2] == ep[e]) & (e2 < e))
            rank = rank + ahead.astype(jnp.int32)
        sel_e.append(rank < 2)
    top_sum = functools.reduce(jnp.add, [jnp.where(sel_e[e], ep[e], 0.0) for e in range(E)])
    lane = lax.broadcasted_iota(jnp.int32, comb_ref.shape, 1)
    comb = jnp.zeros(comb_ref.shape, f32)
    for g in range(N_GROUPS):
        for e in range(E):
            wge = jnp.where(sel_g[g] & sel_e[e], ep[e] / top_sum * g_top, 0.0)
            comb = jnp.where(lane == g * E + e, wge, comb)
    comb_ref[...] = comb

    member = jnp.where(comb > 0.0, 1.0, 0.0)
    r_i = lax.broadcasted_iota(jnp.int32, (tm, tm), 0)
    c_i = lax.broadcasted_iota(jnp.int32, (tm, tm), 1)
    earlier = jnp.where(c_i < r_i, 1.0, 0.0).astype(bf16)
    carry = carry_scr[...]
    rank_ref[...] = _dot(earlier, member.astype(bf16)) + carry
    carry = carry + jnp.sum(member, axis=0, keepdims=True)
    carry_scr[...] = carry
    cnt_ref[...] = carry


def _router(x, gain, w_r, b_r, *, tm):
    t_len, d = x.shape
    row = lambda i: (i, 0)
    return pl.pallas_call(
        _router_kernel,
        grid=(t_len // tm,),
        in_specs=[
            pl.BlockSpec((tm, d), row),
            pl.BlockSpec((1, d), lambda i: (0, 0)),
            pl.BlockSpec((d, LANES), lambda i: (0, 0)),
            pl.BlockSpec((1, LANES), lambda i: (0, 0)),
        ],
        out_specs=[pl.BlockSpec((tm * SLABS, d // SLABS), row), pl.BlockSpec((tm, LANES), row),
                   pl.BlockSpec((tm, LANES), row), pl.BlockSpec((1, LANES), lambda i: (0, 0))],
        out_shape=[jax.ShapeDtypeStruct((t_len * SLABS, d // SLABS), f32), jax.ShapeDtypeStruct((t_len, LANES), f32),
                   jax.ShapeDtypeStruct((t_len, LANES), f32), jax.ShapeDtypeStruct((1, LANES), f32)],
        scratch_shapes=[pltpu.VMEM((1, LANES), f32)],
        compiler_params=_params("arbitrary"),
        name="moe_router",
    )(x, gain.reshape(1, d), w_r, b_r)


def _store_slabs(ref, val):
    rows, d = val.shape
    w = d // SLABS
    for s in range(SLABS):
        ref[pl.ds(s, rows, stride=SLABS), :] = val[:, s * w:(s + 1) * w]


def _load_slabs(ref, rows, dtype):
    return jnp.concatenate([ref[pl.ds(s, rows, stride=SLABS), :].astype(dtype) for s in range(SLABS)], axis=1)


def _meta_kernel(comb_ref, rank_ref, cnt_ref, pos_ref, w_ref, te_ref, nu_ref):
    comb = comb_ref[...]
    lane1 = lax.broadcasted_iota(jnp.int32, (1, LANES), 1)
    cnt = cnt_ref[...]
    gsz = jnp.ceil(cnt * (1.0 / MOE_TM)) * MOE_TM
    e_r = lax.broadcasted_iota(jnp.int32, (LANES, LANES), 0)
    e_c = lax.broadcasted_iota(jnp.int32, (LANES, LANES), 1)
    before = jnp.where(e_r < e_c, 1.0, 0.0).astype(bf16)
    goff = _dot_f32_exactrhs(jnp.broadcast_to(gsz, (8, LANES)), before)[0:1]
    member = comb > 0.0
    pos = jnp.where(member, goff + rank_ref[...], -1.0)
    pa = jnp.max(pos, axis=1, keepdims=True)
    is_a = member & (pos == pa)
    wa = jnp.sum(jnp.where(is_a, comb, 0.0), axis=1, keepdims=True)
    rest = member & jnp.logical_not(is_a)
    pb = jnp.max(jnp.where(rest, pos, -1.0), axis=1, keepdims=True)
    wb = jnp.sum(jnp.where(rest, comb, 0.0), axis=1, keepdims=True)
    lane = lax.broadcasted_iota(jnp.int32, comb.shape, 1)
    pos_ref[...] = jnp.where(lane == 0, pa, jnp.where(lane == 1, pb, -1.0)).astype(jnp.int32)
    w_ref[...] = jnp.where(lane == 0, wa, jnp.where(lane == 1, wb, 0.0))
    valid_e = lane1 < N_EXPERTS
    gend = goff + gsz
    tile_start = lax.broadcasted_iota(jnp.int32, (LANES, LANES), 0).astype(f32) * MOE_TM
    te = jnp.sum(jnp.where(valid_e & (gend <= tile_start), 1.0, 0.0), axis=1, keepdims=True)
    last = jnp.max(jnp.where(valid_e & (gsz > 0.0), lane1.astype(f32), 0.0), axis=1, keepdims=True)
    te_ref[...] = jnp.minimum(te, last).astype(jnp.int32)
    nu_ref[...] = (jnp.sum(jnp.where(valid_e, gsz, 0.0), axis=1, keepdims=True) * (1.0 / MOE_TM)).astype(jnp.int32)


def _moe_meta(comb, rank, cnt, *, tm):
    t_len = comb.shape[0]
    row = lambda i: (i, 0)
    fixed = lambda i: (0, 0)
    return pl.pallas_call(
        _meta_kernel,
        grid=(t_len // tm,),
        in_specs=[pl.BlockSpec((tm, LANES), row), pl.BlockSpec((tm, LANES), row), pl.BlockSpec((1, LANES), fixed)],
        out_specs=[pl.BlockSpec((tm, LANES), row), pl.BlockSpec((tm, LANES), row),
                   pl.BlockSpec((LANES, 1), fixed), pl.BlockSpec((1, 1), fixed)],
        out_shape=[jax.ShapeDtypeStruct((t_len, LANES), jnp.int32), jax.ShapeDtypeStruct((t_len, LANES), f32),
                   jax.ShapeDtypeStruct((LANES, 1), jnp.int32), jax.ShapeDtypeStruct((1, 1), jnp.int32)],
        compiler_params=_params("arbitrary"),
        name="moe_meta",
    )(comb, rank, cnt)


def _invmap_kernel(pos_ref, src_ref):
    def zero(p, carry):
        src_ref[p] = 0
        return carry

    lax.fori_loop(0, src_ref.shape[0], zero, 0)

    def body(k, carry):
        for slot in range(EXPERT_TOPK):
            p = pos_ref[EXPERT_TOPK * k + slot]

            @pl.when(p >= 0)
            def _():
                src_ref[p] = k
        return carry

    lax.fori_loop(0, pos_ref.shape[0] // EXPERT_TOPK, body, 0)


def _moe_invmap(pos_flat):
    return pl.pallas_call(
        _invmap_kernel,
        in_specs=[pl.BlockSpec(memory_space=pltpu.SMEM)],
        out_specs=pl.BlockSpec(memory_space=pltpu.SMEM),
        out_shape=jax.ShapeDtypeStruct((MOE_ROWS,), jnp.int32),
        name="moe_invmap",
    )(pos_flat)


def _token_rows(ref, t):
    return ref.at[pl.ds(pl.multiple_of(t * SLABS, SLABS), SLABS)]


def _gather_kernel(src_ref, h8_ref, xs8_ref, sem):
    base = pl.program_id(0) * MOE_TM

    def copy(k):
        return pltpu.make_async_copy(_token_rows(h8_ref, src_ref[base + k]), _token_rows(xs8_ref, base + k), sem)

    def start(k, carry):
        copy(k).start()
        return carry

    def wait(k, carry):
        copy(k).wait()
        return carry

    lax.fori_loop(0, MOE_TM, start, 0)
    lax.fori_loop(0, MOE_TM, wait, 0)


def _moe_gather(src, h8):
    return pl.pallas_call(
        _gather_kernel,
        grid_spec=pltpu.PrefetchScalarGridSpec(
            num_scalar_prefetch=1,
            grid=(MOE_TILES,),
            in_specs=[pl.BlockSpec(memory_space=pl.ANY)],
            out_specs=pl.BlockSpec(memory_space=pl.ANY),
            scratch_shapes=[pltpu.SemaphoreType.DMA],
        ),
        out_shape=jax.ShapeDtypeStruct((MOE_ROWS * SLABS, h8.shape[1]), f32),
        compiler_params=pltpu.CompilerParams(dimension_semantics=("arbitrary",), disable_bounds_checks=True),
        name="moe_gather",
    )(src, h8)


def _expert_changed(te_ref, i):
    return (i == 0) | (te_ref[i] != te_ref[jnp.maximum(i - 1, 0)])


def _moe_hid_kernel(te_ref, nu_ref, xs_ref, w1_ref, w3_ref, o_ref, w1b, w3b):
    i = pl.program_id(1)

    @pl.when(_expert_changed(te_ref, i))
    def _():
        w1b[...] = w1_ref[...].astype(bf16)
        w3b[...] = w3_ref[...].astype(bf16)

    @pl.when(i < nu_ref[0])
    def _():
        x = _load_slabs(xs_ref, MOE_TM, bf16)
        u = _dot(x, w1b[...])
        g = _dot(x, w3b[...])
        o_ref[...] = (u * jax.nn.sigmoid(u) * g).astype(o_ref.dtype)

    @pl.when(i >= nu_ref[0])
    def _():
        o_ref[...] = jnp.zeros(o_ref.shape, o_ref.dtype)


def _moe_hid(te, nused, xs8, w1, w3, *, tn):
    ne, d, fdim = w1.shape
    used = lambda i, nu: jnp.where(i < nu[0], i, 0)
    return pl.pallas_call(
        _moe_hid_kernel,
        grid_spec=pltpu.PrefetchScalarGridSpec(
            num_scalar_prefetch=2,
            grid=(fdim // tn, MOE_TILES),
            in_specs=[
                pl.BlockSpec((MOE_TM * SLABS, d // SLABS), lambda j, i, te, nu: (used(i, nu), 0)),
                pl.BlockSpec((None, d, tn), lambda j, i, te, nu: (te[i], 0, j)),
                pl.BlockSpec((None, d, tn), lambda j, i, te, nu: (te[i], 0, j)),
            ],
            out_specs=pl.BlockSpec((MOE_TM, tn), lambda j, i, te, nu: (i, j)),
            scratch_shapes=[pltpu.VMEM((d, tn), bf16), pltpu.VMEM((d, tn), bf16)],
        ),
        out_shape=jax.ShapeDtypeStruct((MOE_ROWS, fdim), bf16),
        compiler_params=_params("arbitrary", "arbitrary"),
        name="moe_hid",
    )(te, nused, xs8, w1, w3)


def _moe_w2_kernel(te_ref, nu_ref, hid_ref, w2_ref, o_ref, w2b):
    i = pl.program_id(0)

    @pl.when(_expert_changed(te_ref, i))
    def _():
        w2b[...] = w2_ref[...].astype(bf16)

    @pl.when(i < nu_ref[0])
    def _():
        _store_slabs(o_ref, _dot(hid_ref[...], w2b[...]))

    @pl.when(i >= nu_ref[0])
    def _():
        o_ref[...] = jnp.zeros(o_ref.shape, o_ref.dtype)


def _moe_w2(te, nused, hid, w2):
    ne, fdim, d = w2.shape
    used = lambda i, nu: jnp.where(i < nu[0], i, 0)
    return pl.pallas_call(
        _moe_w2_kernel,
        grid_spec=pltpu.PrefetchScalarGridSpec(
            num_scalar_prefetch=2,
            grid=(MOE_TILES,),
            in_specs=[
                pl.BlockSpec((MOE_TM, fdim), lambda i, te, nu: (used(i, nu), 0)),
                pl.BlockSpec((None, fdim, d), lambda i, te, nu: (te[i], 0, 0), pipeline_mode=pl.Buffered(1)),
            ],
            out_specs=pl.BlockSpec((MOE_TM * SLABS, d // SLABS), lambda i, te, nu: (i, 0)),
            scratch_shapes=[pltpu.VMEM((fdim, d), bf16)],
        ),
        out_shape=jax.ShapeDtypeStruct((MOE_ROWS * SLABS, d // SLABS), f32),
        compiler_params=_params("arbitrary"),
        name="moe_w2",
    )(te, nused, hid, w2)


def _moe_combine_kernel(pos_ref, x_ref, w_ref, ys_ref, o_ref, buf_a, buf_b, sem):
    tm, d = x_ref.shape
    base = pl.program_id(0) * tm

    def copies(k):
        out = []
        for slot, buf in enumerate((buf_a, buf_b)):
            p = jnp.maximum(pos_ref[EXPERT_TOPK * (base + k) + slot], 0)
            out.append(pltpu.make_async_copy(_token_rows(ys_ref, p), _token_rows(buf, k), sem))
        return out

    def start(k, carry):
        for cp in copies(k):
            cp.start()
        return carry

    def wait(k, carry):
        for cp in copies(k):
            cp.wait()
        return carry

    lax.fori_loop(0, tm, start, 0)
    lax.fori_loop(0, tm, wait, 0)
    wa = w_ref[:, 0:1]
    wb = w_ref[:, 1:2]
    w = d // SLABS
    for s in range(SLABS):
        cols = slice(s * w, (s + 1) * w)
        o_ref[:, cols] = (x_ref[:, cols] + wa * buf_a[pl.ds(s, tm, stride=SLABS), :]
                          + wb * buf_b[pl.ds(s, tm, stride=SLABS), :])


def _moe_combine(pos_flat, x, tokw, ys8, *, tm):
    t_len, d = x.shape
    return pl.pallas_call(
        _moe_combine_kernel,
        grid_spec=pltpu.PrefetchScalarGridSpec(
            num_scalar_prefetch=1,
            grid=(t_len // tm,),
            in_specs=[
                pl.BlockSpec((tm, d), lambda i, pos: (i, 0)),
                pl.BlockSpec((tm, LANES), lambda i, pos: (i, 0)),
                pl.BlockSpec(memory_space=pl.ANY),
            ],
            out_specs=pl.BlockSpec((tm, d), lambda i, pos: (i, 0)),
            scratch_shapes=[pltpu.VMEM((tm * SLABS, d // SLABS), f32), pltpu.VMEM((tm * SLABS, d // SLABS), f32),
                            pltpu.SemaphoreType.DMA],
        ),
        out_shape=jax.ShapeDtypeStruct((t_len, d), f32),
        compiler_params=pltpu.CompilerParams(dimension_semantics=("arbitrary",), vmem_limit_bytes=VMEM_LIMIT,
                                             disable_bounds_checks=True),
        name="moe_combine",
    )(pos_flat, x, tokw, ys8)


def _hier_moe(x, gain, wg, bg, we, be, w1, w3, w2):
    d = x.shape[1]
    w_r = jnp.concatenate([wg, we.transpose(1, 0, 2).reshape(d, N_EXPERTS)], axis=1)
    w_r = jnp.pad(w_r, ((0, 0), (0, LANES - w_r.shape[1])))
    b_r = jnp.pad(jnp.concatenate([bg, be.reshape(-1)]), (0, LANES - N_GROUPS - N_EXPERTS)).reshape(1, LANES)
    h8, comb, rank, cnt = _router(x, gain, w_r, b_r, tm=256)
    tokpos, tokw, te, nused = _moe_meta(comb, rank, cnt, tm=512)
    pos_flat = tokpos[:, :EXPERT_TOPK].reshape(-1)
    te = te.reshape(-1)
    nused = nused.reshape(-1)
    src = _moe_invmap(pos_flat)
    xs8 = _moe_gather(src, h8)
    hid = _moe_hid(te, nused, xs8, w1, w3, tn=512)
    ys8 = _moe_w2(te, nused, hid, w2)
    return _moe_combine(pos_flat, x, tokw, ys8, tm=256)


def _compress_kernel(z_ref, pe_ref, w1_ref, w2_ref, kg_ref, o_ref):
    half = CMP_STRIDE * B_HEAD_DIM
    z = z_ref[...].astype(f32)
    zt = (z + pe_ref[:, :half]).astype(bf16)
    zb = (z + pe_ref[:, half:]).astype(bf16)
    top = _dot(zt, w1_ref[:half, :])
    bot = _dot(zb, w1_ref[half:, :])
    pre = top + pltpu.roll(bot, N_CMP_PAD - 1, axis=0)
    act = pre * (0.5 * (1.0 + jnp.tanh(np.sqrt(2.0 / np.pi).astype(np.float32) * (pre + 0.044715 * (pre * pre * pre)))))
    out = _dot(act.astype(bf16), w2_ref[...])
    normed = _rms(out, kg_ref[...])
    o_ref[...] = jnp.where(pl.program_id(0) == 0, normed, out).astype(o_ref.dtype)


def _compress(zr, pe_flat, w1, w2, kg):
    G = B_KV_HEADS
    return pl.pallas_call(
        _compress_kernel,
        grid=(2, G),
        in_specs=[
            pl.BlockSpec((None, None, N_CMP_PAD, CMP_STRIDE * B_HEAD_DIM), lambda w, g: (w, g, 0, 0)),
            pl.BlockSpec((None, 1, CMP_BLOCK * B_HEAD_DIM), lambda w, g: (w, 0, 0)),
            pl.BlockSpec((None, CMP_BLOCK * B_HEAD_DIM, CMP_HIDDEN), lambda w, g: (w, 0, 0)),
            pl.BlockSpec((None, CMP_HIDDEN, B_HEAD_DIM), lambda w, g: (w, 0, 0)),
            pl.BlockSpec((1, B_HEAD_DIM), lambda w, g: (0, 0)),
        ],
        out_specs=pl.BlockSpec((None, None, N_CMP_PAD, B_HEAD_DIM), lambda w, g: (w, g, 0, 0)),
        out_shape=jax.ShapeDtypeStruct((2, G, N_CMP_PAD, B_HEAD_DIM), bf16),
        compiler_params=_params("arbitrary", "arbitrary"),
        name="nsa_compress",
    )(zr, pe_flat, w1, w2, kg)


def _key_extra(pos):
    lane = lax.broadcasted_iota(jnp.int32, pos.shape, 1)
    lo = pos & (LANES - 1)
    return jnp.where(lane < 3, lo.astype(f32), jnp.where(lane < 6, (pos - lo).astype(f32), 0.0)).astype(bf16)


def _nsa_kernel(q_ref, gate_ref, sl_ref, kc_ref, vct_ref, ks_ref, vst_ref, kw_ref, vwt_ref, o_ref,
                qa_scr, selb_scr, m_scr, acc_scr, out_scr):
    c = pl.program_id(1)
    R, QB, dh = B_GROUP, Q_BLOCK, B_HEAD_DIM
    rows = R * QB
    t0 = c * QB
    slabs = [slice(r * QB, (r + 1) * QB) for r in range(R)]

    qa_scr[0:dh, :] = jnp.concatenate([q_ref[r].astype(f32).T for r in range(R)], axis=1).astype(bf16)
    s1, s2, s3 = [p.astype(f32) for p in _split3(sl_ref[...] * LOG2E)]
    rid = lax.broadcasted_iota(jnp.int32, (dh, rows), 0)
    qa_scr[dh:2 * dh, :] = jnp.where((rid == 0) | (rid == 3), s1,
                                     jnp.where((rid == 1) | (rid == 4), s2,
                                               jnp.where((rid == 2) | (rid == 5), s3, 0.0))).astype(bf16)

    kidx = lax.broadcasted_iota(jnp.int32, (QB, QB), 0)
    tidx = lax.broadcasted_iota(jnp.int32, (QB, QB), 1)
    causal = kidx <= tidx

    gate_t = jax.nn.sigmoid(gate_ref[...]).T

    def scores(k_rows, pos):
        return _dot(jnp.concatenate([k_rows, _key_extra(pos)], axis=1), qa_scr[...])

    def first_chunk(s_t, ok, v_t):
        ps = []
        for r in range(R):
            s_r = jnp.where(ok, s_t[:, slabs[r]], NEG_INF)
            m = jnp.max(s_r, axis=0, keepdims=True)
            m_scr[:, slabs[r]] = m
            ps.append(jnp.where(ok, jnp.exp2(s_r - m), 0.0).astype(bf16))
        acc_scr[...] = _dot(v_t, jnp.concatenate(ps, axis=1))

    def next_chunk(s_t, mask_bias, v_t):
        ps, alphas = [], []
        for r in range(R):
            s_r = s_t[:, slabs[r]] + mask_bias
            m_old = m_scr[:, slabs[r]]
            m_new = jnp.maximum(m_old, jnp.max(s_r, axis=0, keepdims=True))
            m_scr[:, slabs[r]] = m_new
            alphas.append(jnp.exp2(m_old - m_new))
            ps.append(jnp.exp2(s_r - m_new).astype(bf16))
        pv = _dot(v_t, jnp.concatenate(ps, axis=1))
        for r in range(R):
            acc_scr[:, slabs[r]] = alphas[r] * acc_scr[:, slabs[r]] + pv[:, slabs[r]]

    def emit(branch):
        for r in range(R):
            w = gate_t[branch * R + r:branch * R + r + 1, :] / acc_scr[dh:dh + 1, slabs[r]]
            out_scr[:, slabs[r]] = out_scr[:, slabs[r]] + acc_scr[0:dh, slabs[r]] * w

    cend = lax.broadcasted_iota(jnp.int32, (N_CMP_PAD, QB), 0) * CMP_STRIDE + (CMP_BLOCK - 1)
    ok_c = cend <= t0 + lax.broadcasted_iota(jnp.int32, (N_CMP_PAD, QB), 1)
    s_c = scores(kc_ref[...], cend)
    p_sum = jnp.zeros((N_CMP_PAD, QB), f32)
    ps = []
    for r in range(R):
        s_r = jnp.where(ok_c, s_c[:, slabs[r]], NEG_INF)
        e = jnp.where(ok_c, jnp.exp2(s_r - jnp.max(s_r, axis=0, keepdims=True)), 0.0)
        l = jnp.sum(e, axis=0, keepdims=True)
        p = e / jnp.where(l > 0.0, l, 1.0)
        p_sum = p_sum + p
        ps.append(p.astype(bf16))
    oc_t = _dot(vct_ref[...], jnp.concatenate(ps, axis=1))
    for r in range(R):
        out_scr[:, slabs[r]] = oc_t[:, slabs[r]] * gate_t[r:r + 1, :]

    b_i = lax.broadcasted_iota(jnp.int32, (N_SEL, N_CMP_PAD), 0) * SEL_BLOCK
    n_i = lax.broadcasted_iota(jnp.int32, (N_SEL, N_CMP_PAD), 1) * CMP_STRIDE
    overlap_t = jnp.where((n_i < b_i + SEL_BLOCK) & (n_i + (CMP_BLOCK - 1) >= b_i), 1.0, 0.0).astype(bf16)
    p1, p2, p3 = _split3(p_sum)
    imp = _dot(overlap_t, p3) + _dot(overlap_t, p2) + _dot(overlap_t, p1)
    tq = t0 + tidx
    cur = tq // SEL_BLOCK
    forced = (kidx == 0) | (kidx == cur) | (kidx == cur - 1)
    imp = jnp.where(forced, SEL_FORCE, imp)
    imp = jnp.where(kidx * SEL_BLOCK <= tq, imp, -1.0)
    blk_f = kidx.astype(f32)
    sel = jnp.zeros((N_SEL, QB), f32)
    for _ in range(SEL_TOPK):
        mx = jnp.max(imp, axis=0, keepdims=True)
        first = jnp.min(jnp.where(imp == mx, blk_f, float(N_SEL)), axis=0, keepdims=True)
        pick = blk_f == first
        sel = jnp.where(pick & (mx >= 0.0), 1.0, sel)
        imp = jnp.where(pick, -jnp.inf, imp)
    selb_scr[...] = jnp.where(sel > 0.5, 0.0, NEG_INF)

    def sel_bias(first_block, n_blocks):
        return jnp.concatenate([jnp.broadcast_to(selb_scr[pl.ds(first_block + b, 1), :], (SEL_BLOCK, QB))
                                for b in range(n_blocks)], axis=0)

    def chunk_pos(i):
        return i * QB + kidx

    off_c = pl.multiple_of(t0, QB)
    first_chunk(scores(ks_ref[pl.ds(off_c, QB), :], chunk_pos(c)), causal & (sel_bias(2 * c, 2) > -1.0),
                vst_ref[c])
    pos2 = lax.broadcasted_iota(jnp.int32, (2 * QB, QB), 0)

    def sel_body(j, carry):
        off = pl.multiple_of(j * 2 * QB, 2 * QB)
        v_t = jnp.concatenate([vst_ref[2 * j], vst_ref[2 * j + 1]], axis=1)
        next_chunk(scores(ks_ref[pl.ds(off, 2 * QB), :], off + pos2), sel_bias(4 * j, 4), v_t)
        return carry

    lax.fori_loop(0, c // 2, sel_body, 0)

    @pl.when(c % 2 == 1)
    def _():
        i = c - 1
        off = pl.multiple_of(i * QB, QB)
        next_chunk(scores(ks_ref[pl.ds(off, QB), :], chunk_pos(i)), sel_bias(2 * i, 2), vst_ref[i])

    emit(1)

    first_chunk(scores(kw_ref[pl.ds(off_c, QB), :], chunk_pos(c)), causal, vwt_ref[c])
    n_back = WINDOW // QB
    oldest_bias = jnp.where(kidx > tidx, 0.0, NEG_INF)

    def win_body(i, carry):
        off = pl.multiple_of(i * QB, QB)
        bias = jnp.where(i == c - n_back, oldest_bias, 0.0)
        next_chunk(scores(kw_ref[pl.ds(off, QB), :], chunk_pos(i)), bias, vwt_ref[i])
        return carry

    lax.fori_loop(jnp.maximum(c - n_back, 0), c, win_body, 0)
    emit(2)

    for r in range(R):
        o_ref[:, r * dh:(r + 1) * dh] = out_scr[:, slabs[r]].T.astype(o_ref.dtype)


def _nsa_attention(qh, gates, slopes, cmp_kv, kvh):
    t_len = qh.shape[1]
    G, R, QB, dh = B_KV_HEADS, B_GROUP, Q_BLOCK, B_HEAD_DIM
    rows = R * QB
    nq = t_len // QB
    kc = cmp_kv[0]
    vc_t = cmp_kv[1].swapaxes(1, 2)
    ones_rows = jnp.zeros((G, nq, BF16_SUBLANES, QB), bf16).at[:, :, 0, :].set(1.0)

    def values_t(v):
        return jnp.concatenate([v.reshape(G, nq, QB, dh).swapaxes(2, 3), ones_rows], axis=2)

    vs_t = values_t(kvh[3 * G:4 * G])
    vw_t = values_t(kvh[5 * G:6 * G])
    dha = dh + BF16_SUBLANES

    def k_spec(base):
        return pl.BlockSpec((None, t_len, dh), lambda g, c: (base + g, 0, 0))

    vt_spec = pl.BlockSpec((None, nq, dha, QB), lambda g, c: (g, 0, 0, 0))
    return pl.pallas_call(
        _nsa_kernel,
        grid=(G, nq),
        in_specs=[
            pl.BlockSpec((R, QB, dh), lambda g, c: (g, c, 0)),
            pl.BlockSpec((QB, LANES), lambda g, c: (c, g)),
            pl.BlockSpec((None, 1, rows), lambda g, c: (g, 0, 0)),
            pl.BlockSpec((None, N_CMP_PAD, dh), lambda g, c: (g, 0, 0)),
            pl.BlockSpec((None, dh, N_CMP_PAD), lambda g, c: (g, 0, 0)),
            k_spec(2 * G), vt_spec, k_spec(4 * G), vt_spec,
        ],
        out_specs=pl.BlockSpec((QB, R * dh), lambda g, c: (c, g)),
        out_shape=jax.ShapeDtypeStruct((t_len, G * R * dh), bf16),
        scratch_shapes=[
            pltpu.VMEM((2 * dh, rows), bf16),
            pltpu.VMEM((N_SEL, QB), f32),
            pltpu.VMEM((1, rows), f32),
            pltpu.VMEM((dha, rows), f32),
            pltpu.VMEM((dh, rows), f32),
        ],
        compiler_params=_params("parallel", "arbitrary"),
        name="nsa_attention",
    )(qh, gates, slopes, kc, vc_t, kvh, vs_t, kvh, vw_t)


def _mlstm_layer(x, gain, w_in, b_if, head_g, w_out):
    H, dk, dv = A_HEADS, A_DQK, A_DV
    n_main = 2 * H * dk + 2 * H * dv
    w_main = w_in[:, :n_main].astype(bf16)
    w_gate = jnp.pad(w_in[:, n_main:], ((0, 0), (0, LANES - 2 * H))).astype(bf16)
    proj = _matmul(x, w_main, gain=gain, out_dtype=bf16, tm=512, tn=512, name="mlstm_in")
    gates = _matmul(x, w_gate, gain=gain, out_dtype=f32, tm=512, tn=LANES, name="mlstm_gates")
    gates_t = gates[:, :2 * H].T
    o = _mlstm(proj, gates_t, b_if, head_g)
    return _matmul(o, w_out.astype(bf16), residual=x, out_dtype=f32, tm=1024, tn=512, name="mlstm_out")


def _memattn_layer(x, gain, mem, mem_g, wq, wk, wv, wo, q_g, k_g):
    wkv = jnp.concatenate([wk, wv], axis=1).astype(bf16)
    kv = _matmul(mem, wkv, gain=mem_g, out_dtype=f32, tm=N_MEM, tn=512, name="mem_kv")
    return _memattn(x, gain, wq.astype(bf16), kv, q_g, k_g, wo.astype(bf16), tm=256)


def _alibi_slopes(n):
    return np.asarray([2.0 ** (-8.0 * (h + 1) / n) for h in range(n)], dtype=np.float32)


def _nsa_shared_kv(x, kv_norm_g, kv_w, cmp_pe, cmp_w1, cmp_w2, k_norm_g):
    G, dh = B_KV_HEADS, B_HEAD_DIM
    t_len = x.shape[0]
    ones = jnp.ones((G, 1, dh), f32)
    zeros = jnp.zeros((G, 1, dh), f32)
    head_gain = jnp.concatenate([ones, ones, ones * k_norm_g[1], ones, ones * k_norm_g[2], ones], axis=0)
    norm_flag = jnp.concatenate([zeros, zeros, ones, zeros, ones, zeros], axis=0)
    kvh = _proj_heads(x, kv_norm_g, kv_w.astype(bf16), head_gain, norm_flag, scale=1.0, tm=512, name="nsa_kv_proj")
    zr = kvh[:2 * G].reshape(2, G, t_len // CMP_STRIDE, CMP_STRIDE * dh)
    cmp_kv = _compress(zr, cmp_pe.reshape(2, 1, CMP_BLOCK * dh), cmp_w1.astype(bf16), cmp_w2.astype(bf16),
                       k_norm_g[0].reshape(1, dh))
    return cmp_kv, kvh


def _nsa_layer(x, gain, shared, w_in, q_norm_g, w_out):
    cmp_kv, kvh = shared
    H, G, R, dh = B_HEADS, B_KV_HEADS, B_GROUP, B_HEAD_DIM
    ones = jnp.ones((H, 1, dh), f32)
    qh = _proj_heads(x, gain, w_in[:, :H * dh].astype(bf16), ones * q_norm_g, ones, scale=dh ** -0.5 * LOG2E,
                     tm=512, name="nsa_q_proj")
    wg = w_in[:, H * dh:].reshape(-1, 3, G, R).transpose(0, 2, 1, 3).reshape(-1, G, 3 * R)
    wg = jnp.pad(wg, ((0, 0), (0, 0), (0, LANES - 3 * R))).reshape(-1, G * LANES).astype(bf16)
    gates = _matmul(x, wg, gain=gain, out_dtype=f32, tm=512, tn=512, name="nsa_gates")
    slopes = jnp.asarray(np.repeat(_alibi_slopes(H).reshape(G, R, 1), Q_BLOCK, axis=2).reshape(G, 1, R * Q_BLOCK))
    o = _nsa_attention(qh, gates, slopes, cmp_kv, kvh)
    return _matmul(o, w_out.astype(bf16), residual=x, out_dtype=f32, tm=1024, tn=512, name="nsa_out")


def kernel(x, mem, norm_g, a_w_in, a_b_if, a_head_g, a_w_out, kv_norm_g, kv_w, cmp_pe, cmp_w1, cmp_w2, k_norm_g,
           b_w_in, b_q_norm_g, b_w_out, mem_norm_g, mem_wq, mem_wk, mem_wv, mem_wo, mem_q_g, mem_k_g, moe_wg,
           moe_bg, moe_we, moe_be, moe_w1, moe_w3, moe_w2):
    bsz, t_len, d = x.shape
    assert (bsz, t_len, d) == (1, SEQ, D_MODEL)
    xs = x.reshape(t_len, d)
    ms = mem.reshape(N_MEM, d)
    depth = norm_g.shape[0]
    n_a = depth - depth // 2
    shared = None
    for layer in range(depth):
        if layer < n_a:
            xs = _mlstm_layer(xs, norm_g[layer, 0], a_w_in[layer], a_b_if[layer], a_head_g[layer], a_w_out[layer])
        else:
            if layer == n_a:
                shared = _nsa_shared_kv(xs, kv_norm_g, kv_w, cmp_pe, cmp_w1, cmp_w2, k_norm_g)
            j = layer - n_a
            xs = _nsa_layer(xs, norm_g[layer, 0], shared, b_w_in[j], b_q_norm_g[j], b_w_out[j])
        xs = _memattn_layer(xs, norm_g[layer, 1], ms, mem_norm_g[layer], mem_wq[layer], mem_wk[layer],
                            mem_wv[layer], mem_wo[layer], mem_q_g[layer], mem_k_g[layer])
        xs = _hier_moe(xs, norm_g[layer, 2], moe_wg[layer], moe_bg[layer], moe_we[layer], moe_be[layer],
                       moe_w1[layer], moe_w3[layer], moe_w2[layer])
    return xs.reshape(bsz, t_len, d)
```

```python
import functools

import numpy as np
import jax
import jax.numpy as jnp
from jax import lax
from jax.experimental import pallas as pl
from jax.experimental.pallas import tpu as pltpu

f32 = jnp.float32
bf16 = jnp.bfloat16

V7X_VMEM_BYTES = 64 * 1024 * 1024
VMEM_LIMIT = V7X_VMEM_BYTES - 8 * 1024 * 1024
LANES = 128
BF16_SUBLANES = 16
LOG2E = 1.4426950408889634

D_MODEL = 4096
SEQ = 8192
N_MEM = 256
RMS_EPS = 1e-6
NEG_INF = -1e30
A_HEADS = 8
A_DQK = 256
A_DV = 512
A_CHUNK = 128
B_HEADS = 32
B_KV_HEADS = 4
B_HEAD_DIM = 128
B_GROUP = 8
CMP_BLOCK = 32
CMP_STRIDE = 16
CMP_HIDDEN = 256
SEL_BLOCK = 64
SEL_TOPK = 16
WINDOW = 512
Q_BLOCK = 128
SEL_FORCE = 1e6
MEM_HEADS = 4
MEM_HEAD_DIM = 128
N_GROUPS = 4
EXPERTS_PER_GROUP = 4
N_EXPERTS = 16
D_EXPERT = 1024
EXPERT_TOPK = 2
N_CMP_PAD = SEQ // CMP_STRIDE
N_SEL = SEQ // SEL_BLOCK
SLABS = D_MODEL // LANES
MOE_TM = 256
MOE_ROWS = EXPERT_TOPK * SEQ + N_EXPERTS * MOE_TM
MOE_TILES = MOE_ROWS // MOE_TM
SEL_WIDE_CHUNKS = 4


def _params(*sem):
    return pltpu.CompilerParams(dimension_semantics=sem, vmem_limit_bytes=VMEM_LIMIT)


def _rms(x, g):
    ms = jnp.mean(x * x, axis=-1, keepdims=True)
    return x * lax.rsqrt(ms + RMS_EPS) * g


def _nt(a, b):
    return lax.dot_general(a, b, (((1,), (1,)), ((), ())), preferred_element_type=f32)


def _dot(a, b):
    return jnp.dot(a, b, preferred_element_type=f32)


def _split3(a):
    a1 = a.astype(bf16)
    r1 = a - a1.astype(f32)
    a2 = r1.astype(bf16)
    a3 = (r1 - a2.astype(f32)).astype(bf16)
    return a1, a2, a3


def _dot_f32_exactrhs(a, b_bf16):
    a1, a2, a3 = _split3(a)
    return _dot(a3, b_bf16) + _dot(a2, b_bf16) + _dot(a1, b_bf16)


def _mm_kernel(*refs, norm, residual):
    it = iter(refs)
    a_ref = next(it)
    g_ref = next(it) if norm else None
    w_ref = next(it)
    r_ref = next(it) if residual else None
    o_ref = next(it)
    h_scr = next(it) if norm else None
    if norm:
        @pl.when(pl.program_id(1) == 0)
        def _():
            h_scr[...] = _rms(a_ref[...], g_ref[...]).astype(bf16)
        a = h_scr[...]
    else:
        a = a_ref[...]
    acc = _dot(a, w_ref[...])
    if residual:
        acc = acc + r_ref[...]
    o_ref[...] = acc.astype(o_ref.dtype)


def _matmul(a, w, *, gain=None, residual=None, out_dtype, tm, tn, name):
    m, k = a.shape
    n = w.shape[1]
    norm = gain is not None
    ins = [a]
    specs = [pl.BlockSpec((tm, k), lambda i, j: (i, 0))]
    if norm:
        ins.append(gain.reshape(1, k))
        specs.append(pl.BlockSpec((1, k), lambda i, j: (0, 0)))
    ins.append(w)
    specs.append(pl.BlockSpec((k, tn), lambda i, j: (0, j)))
    if residual is not None:
        ins.append(residual)
        specs.append(pl.BlockSpec((tm, tn), lambda i, j: (i, j)))
    return pl.pallas_call(
        functools.partial(_mm_kernel, norm=norm, residual=residual is not None),
        grid=(m // tm, n // tn),
        in_specs=specs,
        out_specs=pl.BlockSpec((tm, tn), lambda i, j: (i, j)),
        out_shape=jax.ShapeDtypeStruct((m, n), out_dtype),
        scratch_shapes=[pltpu.VMEM((tm, k), bf16)] if norm else [],
        compiler_params=_params("parallel", "arbitrary"),
        name=name,
    )(*ins)


def _rmsnorm_kernel(x_ref, g_ref, *o_refs):
    x = x_ref[...]
    y = x * lax.rsqrt(jnp.mean(x * x, axis=-1, keepdims=True) + RMS_EPS)
    for n, o_ref in enumerate(o_refs):
        o_ref[...] = (y * g_ref[n:n + 1, :]).astype(o_ref.dtype)


def _rmsnorm_bf16(x, gains, *, tm, name):
    m, k = x.shape
    n = len(gains)
    row = lambda i: (i, 0)
    return pl.pallas_call(
        _rmsnorm_kernel,
        grid=(m // tm,),
        in_specs=[pl.BlockSpec((tm, k), row), pl.BlockSpec((n, k), lambda i: (0, 0))],
        out_specs=[pl.BlockSpec((tm, k), row)] * n,
        out_shape=[jax.ShapeDtypeStruct((m, k), bf16)] * n,
        compiler_params=_params("parallel"),
        name=name,
    )(x, jnp.stack(gains))


def _mm_ws_kernel(*refs, residual):
    it = iter(refs)
    a_ref = next(it)
    w_ref = next(it)
    r_ref = next(it) if residual else None
    o_ref = next(it)
    wb_scr = next(it)

    @pl.when(pl.program_id(1) == 0)
    def _():
        wb_scr[...] = w_ref[...].astype(bf16)

    acc = _dot(a_ref[...], wb_scr[...])
    if residual:
        acc = acc + r_ref[...]
    o_ref[...] = acc.astype(o_ref.dtype)


def _matmul_ws(a, w, layer, n_cols, *, residual=None, out_dtype, tm, tn, name):
    m, k = a.shape
    ins = [a, w]
    specs = [pl.BlockSpec((tm, k), lambda j, i: (i, 0)), pl.BlockSpec((None, k, tn), lambda j, i: (layer, 0, j))]
    if residual is not None:
        ins.append(residual)
        specs.append(pl.BlockSpec((tm, tn), lambda j, i: (i, j)))
    return pl.pallas_call(
        functools.partial(_mm_ws_kernel, residual=residual is not None),
        grid=(n_cols // tn, m // tm),
        in_specs=specs,
        out_specs=pl.BlockSpec((tm, tn), lambda j, i: (i, j)),
        out_shape=jax.ShapeDtypeStruct((m, n_cols), out_dtype),
        scratch_shapes=[pltpu.VMEM((k, tn), bf16)],
        compiler_params=_params("arbitrary", "arbitrary"),
        name=name,
    )(*ins)


def _proj_heads_kernel(a_ref, w_ref, hg_ref, nf_ref, o_ref, wb_scr, *, heads_per_tile, scale):
    @pl.when(pl.program_id(1) == 0)
    def _():
        wb_scr[...] = w_ref[...].astype(bf16)

    acc = _dot(a_ref[...], wb_scr[...])
    for hd in range(heads_per_tile):
        z = acc[:, hd * LANES:(hd + 1) * LANES]
        zn = _rms(z, hg_ref[hd]) * scale
        o_ref[hd] = jnp.where(nf_ref[hd] > 0.5, zn, z).astype(o_ref.dtype)


def _proj_heads(a, w, layer, n_cols, head_gain, norm_flag, *, scale, tm, name):
    m, k = a.shape
    hpt = 4
    tn = hpt * LANES
    nh = n_cols // LANES
    return pl.pallas_call(
        functools.partial(_proj_heads_kernel, heads_per_tile=hpt, scale=scale),
        grid=(n_cols // tn, m // tm),
        in_specs=[
            pl.BlockSpec((tm, k), lambda j, i: (i, 0)),
            pl.BlockSpec((None, k, tn), lambda j, i: (layer, 0, j)),
            pl.BlockSpec((hpt, 1, LANES), lambda j, i: (j, 0, 0)),
            pl.BlockSpec((hpt, 1, LANES), lambda j, i: (j, 0, 0)),
        ],
        out_specs=pl.BlockSpec((hpt, tm, LANES), lambda j, i: (j, i, 0)),
        out_shape=jax.ShapeDtypeStruct((nh, m, LANES), bf16),
        scratch_shapes=[pltpu.VMEM((k, tn), bf16)],
        compiler_params=_params("arbitrary", "arbitrary"),
        name=name,
    )(a, w, head_gain, norm_flag)


def _mlstm_kernel(q_ref, k_ref, v_ref, og_ref, gt_ref, bif_ref, hg_ref, o_ref, c_scr, n_scr, m_scr):
    c = pl.program_id(0)
    h = pl.program_id(1)
    L, dk = A_CHUNK, A_DQK
    qscale = dk ** -0.5

    @pl.when(c == 0)
    def _():
        c_scr[h] = jnp.zeros(c_scr.shape[1:], f32)
        n_scr[h] = jnp.zeros(n_scr.shape[1:], f32)
        m_scr[h] = jnp.zeros(m_scr.shape[1:], f32)

    q = q_ref[...]
    k = k_ref[...]
    v = v_ref[...]
    li = gt_ref[pl.ds(h, 1), :] + bif_ref[pl.ds(h, 1), :]
    xf = gt_ref[pl.ds(h + A_HEADS, 1), :] + bif_ref[pl.ds(h + A_HEADS, 1), :]
    lf = jnp.minimum(xf, 0.0) - jnp.log1p(jnp.exp(-jnp.abs(xf)))

    ii = lax.broadcasted_iota(jnp.int32, (L, L), 0)
    jj = lax.broadcasted_iota(jnp.int32, (L, L), 1)
    upper = jnp.where(ii <= jj, 1.0, 0.0).astype(bf16)
    b_row = _dot_f32_exactrhs(jnp.broadcast_to(lf, (8, L)), upper)[0:1]
    eye = ii == jj

    def to_col(row):
        return jnp.sum(jnp.where(eye, row, 0.0), axis=1, keepdims=True)

    b_col = to_col(b_row)
    causal = jj <= ii
    dmat = b_col - b_row + li
    m_prev = m_scr[h]
    m_inter = b_col + m_prev
    m_t = jnp.maximum(m_inter, jnp.max(jnp.where(causal, dmat, -jnp.inf), axis=1, keepdims=True))
    dexp = jnp.where(causal, jnp.exp(dmat - m_t), 0.0)
    s = _nt(q, k) * qscale * dexp
    w_inter = jnp.exp(m_inter - m_t)
    ct = c_scr[h]
    n_row = n_scr[h]
    inter = _dot(q, ct.astype(bf16)) * qscale
    num = _dot(s.astype(bf16), v) + w_inter * inter
    qn = jnp.sum(q.astype(f32) * n_row, axis=1, keepdims=True) * qscale
    den = jnp.sum(s, axis=1, keepdims=True) + w_inter * qn
    hc = num / jnp.maximum(jnp.abs(den), jnp.exp(-m_t))

    b_end = b_row[:, L - 1:L]
    w_log = b_end - b_row + li
    m_new = jnp.maximum(b_end + m_prev, jnp.max(w_log, axis=1, keepdims=True))
    decay = jnp.exp(b_end + m_prev - m_new)
    w_wr = jnp.exp(w_log - m_new)
    vw = (v.astype(f32) * to_col(w_wr)).astype(bf16)
    kt = k.astype(f32).T.astype(bf16)
    c_scr[h] = decay * ct + _dot(kt, vw)
    n_scr[h] = decay * n_row + _dot(jnp.broadcast_to(w_wr, (8, L)).astype(bf16), k)[0:1]
    m_scr[h] = m_new

    hn = _rms(hc, hg_ref[...])
    o_ref[...] = (jax.nn.sigmoid(og_ref[...].astype(f32)) * hn).astype(o_ref.dtype)


def _mlstm(proj, gates_t, b_if, head_g):
    t_len = proj.shape[0]
    H, dk, dv, L = A_HEADS, A_DQK, A_DV, A_CHUNK
    nk = H * dk // dk
    return pl.pallas_call(
        _mlstm_kernel,
        grid=(t_len // L, H),
        in_specs=[
            pl.BlockSpec((L, dk), lambda c, h: (c, h)),
            pl.BlockSpec((L, dk), lambda c, h: (c, nk + h)),
            pl.BlockSpec((L, dv), lambda c, h: (c, 2 * H * dk // dv + h)),
            pl.BlockSpec((L, dv), lambda c, h: (c, 2 * H * dk // dv + H + h)),
            pl.BlockSpec((2 * H, L), lambda c, h: (0, c)),
            pl.BlockSpec((2 * H, 1), lambda c, h: (0, 0)),
            pl.BlockSpec((None, 1, dv), lambda c, h: (h, 0, 0)),
        ],
        out_specs=pl.BlockSpec((L, dv), lambda c, h: (c, h)),
        out_shape=jax.ShapeDtypeStruct((t_len, H * dv), bf16),
        scratch_shapes=[pltpu.VMEM((H, dk, dv), f32), pltpu.VMEM((H, 1, dk), f32), pltpu.VMEM((H, 1, 1), f32)],
        compiler_params=_params("arbitrary", "arbitrary"),
        name="mlstm",
    )(proj, proj, proj, proj, gates_t, b_if.reshape(2 * H, 1), head_g.reshape(H, 1, dv))


def _memattn_kernel(x_ref, g_ref, wq_ref, kv_ref, qg_ref, kg_ref, wo_ref, o_ref):
    x = x_ref[...]
    h = _rms(x, g_ref[...]).astype(bf16)
    qp = _dot(h, wq_ref[...])
    dh = MEM_HEAD_DIM
    nkv = MEM_HEADS * dh
    outs = []
    for hd in range(MEM_HEADS):
        qn = _rms(qp[:, hd * dh:(hd + 1) * dh], qg_ref[...]).astype(bf16)
        kn = _rms(kv_ref[:, hd * dh:(hd + 1) * dh], kg_ref[...]).astype(bf16)
        vh = kv_ref[:, nkv + hd * dh:nkv + (hd + 1) * dh].astype(bf16)
        s = _nt(qn, kn) * (dh ** -0.5)
        e = jnp.exp(s - jnp.max(s, axis=1, keepdims=True))
        p = e / jnp.sum(e, axis=1, keepdims=True)
        outs.append(_dot(p.astype(bf16), vh).astype(bf16))
    o = jnp.concatenate(outs, axis=1)
    o_ref[...] = x + _dot(o, wo_ref[...])


def _memattn(x, gain, wq, kv, qg, kg, wo, *, tm):
    t_len, d = x.shape
    md = MEM_HEADS * MEM_HEAD_DIM
    return pl.pallas_call(
        _memattn_kernel,
        grid=(t_len // tm,),
        in_specs=[
            pl.BlockSpec((tm, d), lambda i: (i, 0)),
            pl.BlockSpec((1, d), lambda i: (0, 0)),
            pl.BlockSpec((d, md), lambda i: (0, 0)),
            pl.BlockSpec((N_MEM, 2 * md), lambda i: (0, 0)),
            pl.BlockSpec((1, MEM_HEAD_DIM), lambda i: (0, 0)),
            pl.BlockSpec((1, MEM_HEAD_DIM), lambda i: (0, 0)),
            pl.BlockSpec((md, d), lambda i: (0, 0)),
        ],
        out_specs=pl.BlockSpec((tm, d), lambda i: (i, 0)),
        out_shape=jax.ShapeDtypeStruct((t_len, d), f32),
        compiler_params=_params("parallel"),
        name="memattn",
    )(x, gain.reshape(1, d), wq, kv, qg.reshape(1, -1), kg.reshape(1, -1), wo)


def _router_kernel(x_ref, g_ref, wr_ref, br_ref, h8_ref, comb_ref, rank_ref, cnt_ref, carry_scr):
    tm = x_ref.shape[0]

    @pl.when(pl.program_id(0) == 0)
    def _():
        carry_scr[...] = jnp.zeros(carry_scr.shape, f32)

    hf = _rms(x_ref[...], g_ref[...])
    _store_slabs(h8_ref, hf)
    h1, h2, _ = _split3(hf)
    w1, w2, _ = _split3(wr_ref[...])
    logits = _dot(h2, w1) + _dot(h1, w2) + _dot(h1, w1) + br_ref[...]
    col = [logits[:, i:i + 1] for i in range(N_GROUPS + N_EXPERTS)]
    gl = col[:N_GROUPS]
    gm = functools.reduce(jnp.maximum, gl)
    ge = [jnp.exp(z - gm) for z in gl]
    gs = functools.reduce(jnp.add, ge)
    gp = [z / gs for z in ge]
    g_top = functools.reduce(jnp.maximum, gp)
    sel_g = []
    taken = jnp.zeros(g_top.shape, jnp.bool_)
    for z in gp:
        hit = (z == g_top) & jnp.logical_not(taken)
        sel_g.append(hit)
        taken = taken | hit
    E = EXPERTS_PER_GROUP
    el = []
    for e in range(E):
        acc = jnp.zeros_like(g_top)
        for g in range(N_GROUPS):
            acc = acc + jnp.where(sel_g[g], col[N_GROUPS + g * E + e], 0.0)
        el.append(acc)
    em = functools.reduce(jnp.maximum, el)
    ee = [jnp.exp(z - em) for z in el]
    es = functools.reduce(jnp.add, ee)
    ep = [z / es for z in ee]
    sel_e = []
    for e in range(E):
        rank = jnp.zeros(g_top.shape, jnp.int32)
        for e2 in range(E):
            if e2 == e:
                continue
            ahead = (ep[e2] > ep[e]) | ((ep[e2] == ep[e]) & (e2 < e))
            rank = rank + ahead.astype(jnp.int32)
        sel_e.append(rank < 2)
    top_sum = functools.reduce(jnp.add, [jnp.where(sel_e[e], ep[e], 0.0) for e in range(E)])
    lane = lax.broadcasted_iota(jnp.int32, comb_ref.shape, 1)
    comb = jnp.zeros(comb_ref.shape, f32)
    for g in range(N_GROUPS):
        for e in range(E):
            wge = jnp.where(sel_g[g] & sel_e[e], ep[e] / top_sum * g_top, 0.0)
            comb = jnp.where(lane == g * E + e, wge, comb)
    comb_ref[...] = comb

    member = jnp.where(comb > 0.0, 1.0, 0.0)
    r_i = lax.broadcasted_iota(jnp.int32, (tm, tm), 0)
    c_i = lax.broadcasted_iota(jnp.int32, (tm, tm), 1)
    earlier = jnp.where(c_i < r_i, 1.0, 0.0).astype(bf16)
    carry = carry_scr[...]
    rank_ref[...] = _dot(earlier, member.astype(bf16)) + carry
    carry = carry + jnp.sum(member, axis=0, keepdims=True)
    carry_scr[...] = carry
    cnt_ref[...] = carry


def _router(x, gain, w_r, b_r, *, tm):
    t_len, d = x.shape
    row = lambda i: (i, 0)
    return pl.pallas_call(
        _router_kernel,
        grid=(t_len // tm,),
        in_specs=[
            pl.BlockSpec((tm, d), row),
            pl.BlockSpec((1, d), lambda i: (0, 0)),
            pl.BlockSpec((d, LANES), lambda i: (0, 0)),
            pl.BlockSpec((1, LANES), lambda i: (0, 0)),
        ],
        out_specs=[pl.BlockSpec((tm * SLABS, d // SLABS), row), pl.BlockSpec((tm, LANES), row),
                   pl.BlockSpec((tm, LANES), row), pl.BlockSpec((1, LANES), lambda i: (0, 0))],
        out_shape=[jax.ShapeDtypeStruct((t_len * SLABS, d // SLABS), f32), jax.ShapeDtypeStruct((t_len, LANES), f32),
                   jax.ShapeDtypeStruct((t_len, LANES), f32), jax.ShapeDtypeStruct((1, LANES), f32)],
        scratch_shapes=[pltpu.VMEM((1, LANES), f32)],
        compiler_params=_params("arbitrary"),
        name="moe_router",
    )(x, gain.reshape(1, d), w_r, b_r)


def _store_slabs(ref, val):
    rows, d = val.shape
    w = d // SLABS
    for s in range(SLABS):
        ref[pl.ds(s, rows, stride=SLABS), :] = val[:, s * w:(s + 1) * w]


def _load_slabs(ref, rows, dtype):
    return jnp.concatenate([ref[pl.ds(s, rows, stride=SLABS), :].astype(dtype) for s in range(SLABS)], axis=1)


def _meta_kernel(comb_ref, rank_ref, cnt_ref, pos_ref, w_ref, te_ref, nu_ref):
    comb = comb_ref[...]
    lane1 = lax.broadcasted_iota(jnp.int32, (1, LANES), 1)
    cnt = cnt_ref[...]
    gsz = jnp.ceil(cnt * (1.0 / MOE_TM)) * MOE_TM
    e_r = lax.broadcasted_iota(jnp.int32, (LANES, LANES), 0)
    e_c = lax.broadcasted_iota(jnp.int32, (LANES, LANES), 1)
    before = jnp.where(e_r < e_c, 1.0, 0.0).astype(bf16)
    goff = _dot_f32_exactrhs(jnp.broadcast_to(gsz, (8, LANES)), before)[0:1]
    member = comb > 0.0
    pos = jnp.where(member, goff + rank_ref[...], -1.0)
    pa = jnp.max(pos, axis=1, keepdims=True)
    is_a = member & (pos == pa)
    wa = jnp.sum(jnp.where(is_a, comb, 0.0), axis=1, keepdims=True)
    rest = member & jnp.logical_not(is_a)
    pb = jnp.max(jnp.where(rest, pos, -1.0), axis=1, keepdims=True)
    wb = jnp.sum(jnp.where(rest, comb, 0.0), axis=1, keepdims=True)
    lane = lax.broadcasted_iota(jnp.int32, comb.shape, 1)
    pos_ref[...] = jnp.where(lane == 0, pa, jnp.where(lane == 1, pb, -1.0)).astype(jnp.int32)
    w_ref[...] = jnp.where(lane == 0, wa, jnp.where(lane == 1, wb, 0.0))
    valid_e = lane1 < N_EXPERTS
    gend = goff + gsz
    tile_start = lax.broadcasted_iota(jnp.int32, (LANES, LANES), 0).astype(f32) * MOE_TM
    te = jnp.sum(jnp.where(valid_e & (gend <= tile_start), 1.0, 0.0), axis=1, keepdims=True)
    last = jnp.max(jnp.where(valid_e & (gsz > 0.0), lane1.astype(f32), 0.0), axis=1, keepdims=True)
    te_ref[...] = jnp.minimum(te, last).astype(jnp.int32)
    nu_ref[...] = (jnp.sum(jnp.where(valid_e, gsz, 0.0), axis=1, keepdims=True) * (1.0 / MOE_TM)).astype(jnp.int32)


def _moe_meta(comb, rank, cnt, *, tm):
    t_len = comb.shape[0]
    row = lambda i: (i, 0)
    fixed = lambda i: (0, 0)
    return pl.pallas_call(
        _meta_kernel,
        grid=(t_len // tm,),
        in_specs=[pl.BlockSpec((tm, LANES), row), pl.BlockSpec((tm, LANES), row), pl.BlockSpec((1, LANES), fixed)],
        out_specs=[pl.BlockSpec((tm, LANES), row), pl.BlockSpec((tm, LANES), row),
                   pl.BlockSpec((LANES, 1), fixed), pl.BlockSpec((1, 1), fixed)],
        out_shape=[jax.ShapeDtypeStruct((t_len, LANES), jnp.int32), jax.ShapeDtypeStruct((t_len, LANES), f32),
                   jax.ShapeDtypeStruct((LANES, 1), jnp.int32), jax.ShapeDtypeStruct((1, 1), jnp.int32)],
        compiler_params=_params("arbitrary"),
        name="moe_meta",
    )(comb, rank, cnt)


def _invmap_kernel(pos_ref, src_ref):
    def zero(p, carry):
        src_ref[p] = 0
        return carry

    lax.fori_loop(0, src_ref.shape[0], zero, 0, unroll=16)

    def body(k, carry):
        for slot in range(EXPERT_TOPK):
            p = pos_ref[EXPERT_TOPK * k + slot]

            @pl.when(p >= 0)
            def _():
                src_ref[p] = k
        return carry

    lax.fori_loop(0, pos_ref.shape[0] // EXPERT_TOPK, body, 0, unroll=8)


def _moe_invmap(pos_flat):
    return pl.pallas_call(
        _invmap_kernel,
        in_specs=[pl.BlockSpec(memory_space=pltpu.SMEM)],
        out_specs=pl.BlockSpec(memory_space=pltpu.SMEM),
        out_shape=jax.ShapeDtypeStruct((MOE_ROWS,), jnp.int32),
        name="moe_invmap",
    )(pos_flat)


def _token_rows(ref, t):
    return ref.at[pl.ds(pl.multiple_of(t * SLABS, SLABS), SLABS)]


def _gather_kernel(src_ref, h8_ref, xs8_ref, sem):
    base = pl.program_id(0) * MOE_TM

    def copy(k):
        return pltpu.make_async_copy(_token_rows(h8_ref, src_ref[base + k]), _token_rows(xs8_ref, k), sem)

    def start(k, carry):
        copy(k).start()
        return carry

    def wait(k, carry):
        copy(k).wait()
        return carry

    lax.fori_loop(0, MOE_TM, start, 0)
    lax.fori_loop(0, MOE_TM, wait, 0)


def _moe_gather(src, h8):
    return pl.pallas_call(
        _gather_kernel,
        grid_spec=pltpu.PrefetchScalarGridSpec(
            num_scalar_prefetch=1,
            grid=(MOE_TILES,),
            in_specs=[pl.BlockSpec(memory_space=pl.ANY)],
            out_specs=pl.BlockSpec((MOE_TM * SLABS, h8.shape[1]), lambda i, src: (i, 0)),
            scratch_shapes=[pltpu.SemaphoreType.DMA],
        ),
        out_shape=jax.ShapeDtypeStruct((MOE_ROWS * SLABS, h8.shape[1]), f32),
        compiler_params=pltpu.CompilerParams(dimension_semantics=("arbitrary",), vmem_limit_bytes=VMEM_LIMIT,
                                             disable_bounds_checks=True),
        name="moe_gather",
    )(src, h8)


def _expert_changed(te_ref, i):
    return (i == 0) | (te_ref[i] != te_ref[jnp.maximum(i - 1, 0)])


def _moe_hid_kernel(te_ref, nu_ref, xs_ref, w1_ref, w3_ref, o_ref, w1b, w3b):
    i = pl.program_id(1)

    @pl.when(_expert_changed(te_ref, i))
    def _():
        w1b[...] = w1_ref[...].astype(bf16)
        w3b[...] = w3_ref[...].astype(bf16)

    @pl.when(i < nu_ref[0])
    def _():
        x = _load_slabs(xs_ref, MOE_TM, bf16)
        u = _dot(x, w1b[...])
        g = _dot(x, w3b[...])
        o_ref[...] = (u * jax.nn.sigmoid(u) * g).astype(o_ref.dtype)

    @pl.when(i >= nu_ref[0])
    def _():
        o_ref[...] = jnp.zeros(o_ref.shape, o_ref.dtype)


def _moe_hid(te, nused, xs8, w1, w3, layer, *, tn):
    _, ne, d, fdim = w1.shape
    used = lambda i, nu: jnp.where(i < nu[0], i, 0)
    return pl.pallas_call(
        _moe_hid_kernel,
        grid_spec=pltpu.PrefetchScalarGridSpec(
            num_scalar_prefetch=2,
            grid=(fdim // tn, MOE_TILES),
            in_specs=[
                pl.BlockSpec((MOE_TM * SLABS, d // SLABS), lambda j, i, te, nu: (used(i, nu), 0)),
                pl.BlockSpec((None, None, d, tn), lambda j, i, te, nu: (layer, te[i], 0, j)),
                pl.BlockSpec((None, None, d, tn), lambda j, i, te, nu: (layer, te[i], 0, j)),
            ],
            out_specs=pl.BlockSpec((MOE_TM, tn), lambda j, i, te, nu: (i, j)),
            scratch_shapes=[pltpu.VMEM((d, tn), bf16), pltpu.VMEM((d, tn), bf16)],
        ),
        out_shape=jax.ShapeDtypeStruct((MOE_ROWS, fdim), bf16),
        compiler_params=_params("arbitrary", "arbitrary"),
        name="moe_hid",
    )(te, nused, xs8, w1, w3)


def _moe_w2_kernel(te_ref, nu_ref, hid_ref, w2_ref, o_ref, w2b):
    i = pl.program_id(0)

    @pl.when(_expert_changed(te_ref, i))
    def _():
        w2b[...] = w2_ref[...].astype(bf16)

    @pl.when(i < nu_ref[0])
    def _():
        _store_slabs(o_ref, _dot(hid_ref[...], w2b[...]))

    @pl.when(i >= nu_ref[0])
    def _():
        o_ref[...] = jnp.zeros(o_ref.shape, o_ref.dtype)


def _moe_w2(te, nused, hid, w2, layer):
    _, ne, fdim, d = w2.shape
    used = lambda i, nu: jnp.where(i < nu[0], i, 0)
    return pl.pallas_call(
        _moe_w2_kernel,
        grid_spec=pltpu.PrefetchScalarGridSpec(
            num_scalar_prefetch=2,
            grid=(MOE_TILES,),
            in_specs=[
                pl.BlockSpec((MOE_TM, fdim), lambda i, te, nu: (used(i, nu), 0)),
                pl.BlockSpec((None, None, fdim, d), lambda i, te, nu: (layer, te[i], 0, 0),
                             pipeline_mode=pl.Buffered(1)),
            ],
            out_specs=pl.BlockSpec((MOE_TM * SLABS, d // SLABS), lambda i, te, nu: (i, 0)),
            scratch_shapes=[pltpu.VMEM((fdim, d), bf16)],
        ),
        out_shape=jax.ShapeDtypeStruct((MOE_ROWS * SLABS, d // SLABS), f32),
        compiler_params=_params("arbitrary"),
        name="moe_w2",
    )(te, nused, hid, w2)


def _moe_combine_kernel(pos_ref, x_ref, w_ref, ys_ref, o_ref, buf_a, buf_b, sem):
    tm, d = x_ref.shape
    base = pl.program_id(0) * tm

    def copies(k):
        out = []
        for slot, buf in enumerate((buf_a, buf_b)):
            p = jnp.maximum(pos_ref[EXPERT_TOPK * (base + k) + slot], 0)
            out.append(pltpu.make_async_copy(_token_rows(ys_ref, p), _token_rows(buf, k), sem))
        return out

    def start(k, carry):
        for cp in copies(k):
            cp.start()
        return carry

    def wait(k, carry):
        for cp in copies(k):
            cp.wait()
        return carry

    lax.fori_loop(0, tm, start, 0)
    lax.fori_loop(0, tm, wait, 0)
    wa = w_ref[:, 0:1]
    wb = w_ref[:, 1:2]
    w = d // SLABS
    for s in range(SLABS):
        cols = slice(s * w, (s + 1) * w)
        o_ref[:, cols] = (x_ref[:, cols] + wa * buf_a[pl.ds(s, tm, stride=SLABS), :]
                          + wb * buf_b[pl.ds(s, tm, stride=SLABS), :])


def _moe_combine(pos_flat, x, tokw, ys8, *, tm):
    t_len, d = x.shape
    return pl.pallas_call(
        _moe_combine_kernel,
        grid_spec=pltpu.PrefetchScalarGridSpec(
            num_scalar_prefetch=1,
            grid=(t_len // tm,),
            in_specs=[
                pl.BlockSpec((tm, d), lambda i, pos: (i, 0)),
                pl.BlockSpec((tm, LANES), lambda i, pos: (i, 0)),
                pl.BlockSpec(memory_space=pl.ANY),
            ],
            out_specs=pl.BlockSpec((tm, d), lambda i, pos: (i, 0)),
            scratch_shapes=[pltpu.VMEM((tm * SLABS, d // SLABS), f32), pltpu.VMEM((tm * SLABS, d // SLABS), f32),
                            pltpu.SemaphoreType.DMA],
        ),
        out_shape=jax.ShapeDtypeStruct((t_len, d), f32),
        compiler_params=pltpu.CompilerParams(dimension_semantics=("arbitrary",), vmem_limit_bytes=VMEM_LIMIT,
                                             disable_bounds_checks=True),
        name="moe_combine",
    )(pos_flat, x, tokw, ys8)


def _hier_moe(x, gain, wg, bg, we, be, w1, w3, w2, layer):
    d = x.shape[1]
    w_r = jnp.concatenate([wg, we.transpose(1, 0, 2).reshape(d, N_EXPERTS)], axis=1)
    w_r = jnp.pad(w_r, ((0, 0), (0, LANES - w_r.shape[1])))
    b_r = jnp.pad(jnp.concatenate([bg, be.reshape(-1)]), (0, LANES - N_GROUPS - N_EXPERTS)).reshape(1, LANES)
    h8, comb, rank, cnt = _router(x, gain, w_r, b_r, tm=256)
    tokpos, tokw, te, nused = _moe_meta(comb, rank, cnt, tm=512)
    pos_flat = tokpos[:, :EXPERT_TOPK].reshape(-1)
    te = te.reshape(-1)
    nused = nused.reshape(-1)
    src = _moe_invmap(pos_flat)
    xs8 = _moe_gather(src, h8)
    hid = _moe_hid(te, nused, xs8, w1, w3, layer, tn=512)
    ys8 = _moe_w2(te, nused, hid, w2, layer)
    return _moe_combine(pos_flat, x, tokw, ys8, tm=256)


def _compress_kernel(z_ref, pe_ref, w1_ref, w2_ref, kg_ref, o_ref):
    half = CMP_STRIDE * B_HEAD_DIM
    z = z_ref[...].astype(f32)
    zt = (z + pe_ref[:, :half]).astype(bf16)
    zb = (z + pe_ref[:, half:]).astype(bf16)
    top = _dot(zt, w1_ref[:half, :])
    bot = _dot(zb, w1_ref[half:, :])
    pre = top + pltpu.roll(bot, N_CMP_PAD - 1, axis=0)
    act = pre * (0.5 * (1.0 + jnp.tanh(np.sqrt(2.0 / np.pi).astype(np.float32) * (pre + 0.044715 * (pre * pre * pre)))))
    out = _dot(act.astype(bf16), w2_ref[...])
    normed = _rms(out, kg_ref[...])
    o_ref[...] = jnp.where(pl.program_id(0) == 0, normed, out).astype(o_ref.dtype)


def _compress(zr, pe_flat, w1, w2, kg):
    G = B_KV_HEADS
    return pl.pallas_call(
        _compress_kernel,
        grid=(2, G),
        in_specs=[
            pl.BlockSpec((None, None, N_CMP_PAD, CMP_STRIDE * B_HEAD_DIM), lambda w, g: (w, g, 0, 0)),
            pl.BlockSpec((None, 1, CMP_BLOCK * B_HEAD_DIM), lambda w, g: (w, 0, 0)),
            pl.BlockSpec((None, CMP_BLOCK * B_HEAD_DIM, CMP_HIDDEN), lambda w, g: (w, 0, 0)),
            pl.BlockSpec((None, CMP_HIDDEN, B_HEAD_DIM), lambda w, g: (w, 0, 0)),
            pl.BlockSpec((1, B_HEAD_DIM), lambda w, g: (0, 0)),
        ],
        out_specs=pl.BlockSpec((None, None, N_CMP_PAD, B_HEAD_DIM), lambda w, g: (w, g, 0, 0)),
        out_shape=jax.ShapeDtypeStruct((2, G, N_CMP_PAD, B_HEAD_DIM), bf16),
        compiler_params=_params("arbitrary", "arbitrary"),
        name="nsa_compress",
    )(zr, pe_flat, w1, w2, kg)


def _key_extra(pos):
    lane = lax.broadcasted_iota(jnp.int32, pos.shape, 1)
    lo = pos & (LANES - 1)
    return jnp.where(lane < 3, lo.astype(f32), jnp.where(lane < 6, (pos - lo).astype(f32), 0.0)).astype(bf16)


def _nsa_kernel(q_ref, gate_ref, sl_ref, kc_ref, vct_ref, ks_ref, vst_ref, kw_ref, vwt_ref, o_ref,
                qa_scr, selb_scr, m_scr, acc_scr, out_scr):
    c = pl.program_id(1)
    R, QB, dh = B_GROUP, Q_BLOCK, B_HEAD_DIM
    rows = R * QB
    t0 = c * QB
    slabs = [slice(r * QB, (r + 1) * QB) for r in range(R)]

    qa_scr[0:dh, :] = jnp.concatenate([q_ref[r].astype(f32).T for r in range(R)], axis=1).astype(bf16)
    s1, s2, s3 = [p.astype(f32) for p in _split3(sl_ref[...] * LOG2E)]
    rid = lax.broadcasted_iota(jnp.int32, (dh, rows), 0)
    qa_scr[dh:2 * dh, :] = jnp.where((rid == 0) | (rid == 3), s1,
                                     jnp.where((rid == 1) | (rid == 4), s2,
                                               jnp.where((rid == 2) | (rid == 5), s3, 0.0))).astype(bf16)

    kidx = lax.broadcasted_iota(jnp.int32, (QB, QB), 0)
    tidx = lax.broadcasted_iota(jnp.int32, (QB, QB), 1)
    causal = kidx <= tidx

    gate_t = jax.nn.sigmoid(gate_ref[...]).T

    def scores(k_rows, pos):
        return _dot(jnp.concatenate([k_rows, _key_extra(pos)], axis=1), qa_scr[...])

    def first_chunk(s_t, ok, v_t):
        ps = []
        for r in range(R):
            s_r = jnp.where(ok, s_t[:, slabs[r]], NEG_INF)
            m = jnp.max(s_r, axis=0, keepdims=True)
            m_scr[:, slabs[r]] = m
            ps.append(jnp.where(ok, jnp.exp2(s_r - m), 0.0).astype(bf16))
        acc_scr[...] = _dot(v_t, jnp.concatenate(ps, axis=1))

    def next_chunk(s_t, mask_bias, v_t):
        ps, alphas = [], []
        for r in range(R):
            s_r = s_t[:, slabs[r]] + mask_bias
            m_old = m_scr[:, slabs[r]]
            m_new = jnp.maximum(m_old, jnp.max(s_r, axis=0, keepdims=True))
            m_scr[:, slabs[r]] = m_new
            alphas.append(jnp.exp2(m_old - m_new))
            ps.append(jnp.exp2(s_r - m_new).astype(bf16))
        pv = _dot(v_t, jnp.concatenate(ps, axis=1))
        for r in range(R):
            acc_scr[:, slabs[r]] = alphas[r] * acc_scr[:, slabs[r]] + pv[:, slabs[r]]

    def emit(branch):
        for r in range(R):
            w = gate_t[branch * R + r:branch * R + r + 1, :] / acc_scr[dh:dh + 1, slabs[r]]
            out_scr[:, slabs[r]] = out_scr[:, slabs[r]] + acc_scr[0:dh, slabs[r]] * w

    cend = lax.broadcasted_iota(jnp.int32, (N_CMP_PAD, QB), 0) * CMP_STRIDE + (CMP_BLOCK - 1)
    ok_c = cend <= t0 + lax.broadcasted_iota(jnp.int32, (N_CMP_PAD, QB), 1)
    s_c = scores(kc_ref[...], cend)
    p_sum = jnp.zeros((N_CMP_PAD, QB), f32)
    ps = []
    for r in range(R):
        s_r = jnp.where(ok_c, s_c[:, slabs[r]], NEG_INF)
        e = jnp.where(ok_c, jnp.exp2(s_r - jnp.max(s_r, axis=0, keepdims=True)), 0.0)
        l = jnp.sum(e, axis=0, keepdims=True)
        p = e / jnp.where(l > 0.0, l, 1.0)
        p_sum = p_sum + p
        ps.append(p.astype(bf16))
    oc_t = _dot(vct_ref[...], jnp.concatenate(ps, axis=1))
    for r in range(R):
        out_scr[:, slabs[r]] = oc_t[:, slabs[r]] * gate_t[r:r + 1, :]

    b_i = lax.broadcasted_iota(jnp.int32, (N_SEL, N_CMP_PAD), 0) * SEL_BLOCK
    n_i = lax.broadcasted_iota(jnp.int32, (N_SEL, N_CMP_PAD), 1) * CMP_STRIDE
    overlap_t = jnp.where((n_i < b_i + SEL_BLOCK) & (n_i + (CMP_BLOCK - 1) >= b_i), 1.0, 0.0).astype(bf16)
    p1, p2, p3 = _split3(p_sum)
    imp = _dot(overlap_t, p3) + _dot(overlap_t, p2) + _dot(overlap_t, p1)
    tq = t0 + tidx
    cur = tq // SEL_BLOCK
    forced = (kidx == 0) | (kidx == cur) | (kidx == cur - 1)
    imp = jnp.where(forced, SEL_FORCE, imp)
    imp = jnp.where(kidx * SEL_BLOCK <= tq, imp, -1.0)
    blk_f = kidx.astype(f32)
    sel = jnp.zeros((N_SEL, QB), f32)
    for _ in range(SEL_TOPK):
        mx = jnp.max(imp, axis=0, keepdims=True)
        first = jnp.min(jnp.where(imp == mx, blk_f, float(N_SEL)), axis=0, keepdims=True)
        pick = blk_f == first
        sel = jnp.where(pick & (mx >= 0.0), 1.0, sel)
        imp = jnp.where(pick, -jnp.inf, imp)
    selb_scr[...] = jnp.where(sel > 0.5, 0.0, NEG_INF)

    def sel_bias(first_block, n_blocks):
        return jnp.concatenate([jnp.broadcast_to(selb_scr[pl.ds(first_block + b, 1), :], (SEL_BLOCK, QB))
                                for b in range(n_blocks)], axis=0)

    def chunk_pos(i):
        return i * QB + kidx

    off_c = pl.multiple_of(t0, QB)
    first_chunk(scores(ks_ref[pl.ds(off_c, QB), :], chunk_pos(c)), causal & (sel_bias(2 * c, 2) > -1.0),
                vst_ref[c])
    wide = SEL_WIDE_CHUNKS
    pos_w = lax.broadcasted_iota(jnp.int32, (wide * QB, QB), 0)
    blocks_per_chunk = QB // SEL_BLOCK

    def sel_wide(j, carry):
        off = pl.multiple_of(j * wide * QB, wide * QB)
        v_t = jnp.concatenate([vst_ref[wide * j + n] for n in range(wide)], axis=1)
        next_chunk(scores(ks_ref[pl.ds(off, wide * QB), :], off + pos_w),
                   sel_bias(wide * blocks_per_chunk * j, wide * blocks_per_chunk), v_t)
        return carry

    def sel_narrow(i, carry):
        off = pl.multiple_of(i * QB, QB)
        next_chunk(scores(ks_ref[pl.ds(off, QB), :], chunk_pos(i)), sel_bias(blocks_per_chunk * i, blocks_per_chunk),
                   vst_ref[i])
        return carry

    lax.fori_loop(0, c // wide, sel_wide, 0)
    lax.fori_loop((c // wide) * wide, c, sel_narrow, 0)
    emit(1)

    first_chunk(scores(kw_ref[pl.ds(off_c, QB), :], chunk_pos(c)), causal, vwt_ref[c])
    n_back = WINDOW // QB

    @pl.when(c >= n_back)
    def _():
        first = c - n_back
        off = pl.multiple_of(first * QB, QB)
        pos = off + lax.broadcasted_iota(jnp.int32, (n_back * QB, QB), 0)
        bias = jnp.concatenate([jnp.where(kidx > tidx, 0.0, NEG_INF), jnp.zeros(((n_back - 1) * QB, QB), f32)], axis=0)
        v_t = jnp.concatenate([vwt_ref[first + n] for n in range(n_back)], axis=1)
        next_chunk(scores(kw_ref[pl.ds(off, n_back * QB), :], pos), bias, v_t)

    @pl.when(c < n_back)
    def _():
        def win_body(i, carry):
            off = pl.multiple_of(i * QB, QB)
            next_chunk(scores(kw_ref[pl.ds(off, QB), :], chunk_pos(i)), jnp.zeros((QB, QB), f32), vwt_ref[i])
            return carry

        lax.fori_loop(0, c, win_body, 0)

    emit(2)

    for r in range(R):
        o_ref[:, r * dh:(r + 1) * dh] = out_scr[:, slabs[r]].T.astype(o_ref.dtype)


def _nsa_attention(qh, gates, slopes, cmp_kv, kvh):
    t_len = qh.shape[1]
    G, R, QB, dh = B_KV_HEADS, B_GROUP, Q_BLOCK, B_HEAD_DIM
    rows = R * QB
    nq = t_len // QB
    kc = cmp_kv[0]
    vc_t = cmp_kv[1].swapaxes(1, 2)
    ones_rows = jnp.zeros((G, nq, BF16_SUBLANES, QB), bf16).at[:, :, 0, :].set(1.0)

    def values_t(v):
        return jnp.concatenate([v.reshape(G, nq, QB, dh).swapaxes(2, 3), ones_rows], axis=2)

    vs_t = values_t(kvh[3 * G:4 * G])
    vw_t = values_t(kvh[5 * G:6 * G])
    dha = dh + BF16_SUBLANES

    def k_spec(base):
        return pl.BlockSpec((None, t_len, dh), lambda g, c: (base + g, 0, 0))

    vt_spec = pl.BlockSpec((None, nq, dha, QB), lambda g, c: (g, 0, 0, 0))
    return pl.pallas_call(
        _nsa_kernel,
        grid=(G, nq),
        in_specs=[
            pl.BlockSpec((R, QB, dh), lambda g, c: (g, c, 0)),
            pl.BlockSpec((QB, LANES), lambda g, c: (c, g)),
            pl.BlockSpec((None, 1, rows), lambda g, c: (g, 0, 0)),
            pl.BlockSpec((None, N_CMP_PAD, dh), lambda g, c: (g, 0, 0)),
            pl.BlockSpec((None, dh, N_CMP_PAD), lambda g, c: (g, 0, 0)),
            k_spec(2 * G), vt_spec, k_spec(4 * G), vt_spec,
        ],
        out_specs=pl.BlockSpec((QB, R * dh), lambda g, c: (c, g)),
        out_shape=jax.ShapeDtypeStruct((t_len, G * R * dh), bf16),
        scratch_shapes=[
            pltpu.VMEM((2 * dh, rows), bf16),
            pltpu.VMEM((N_SEL, QB), f32),
            pltpu.VMEM((1, rows), f32),
            pltpu.VMEM((dha, rows), f32),
            pltpu.VMEM((dh, rows), f32),
        ],
        compiler_params=_params("parallel", "arbitrary"),
        name="nsa_attention",
    )(qh, gates, slopes, kc, vc_t, kvh, vs_t, kvh, vw_t)


def _mlstm_layer(x, gain, w_in, layer, b_if, head_g, w_out):
    H, dk, dv = A_HEADS, A_DQK, A_DV
    n_main = 2 * H * dk + 2 * H * dv
    w_gate = jnp.pad(w_in[layer, :, n_main:], ((0, 0), (0, LANES - 2 * H))).astype(bf16)
    h, = _rmsnorm_bf16(x, [gain], tm=512, name="mlstm_norm")
    proj = _matmul_ws(h, w_in, layer, n_main, out_dtype=bf16, tm=1024, tn=512, name="mlstm_in")
    gates = _matmul(h, w_gate, out_dtype=f32, tm=1024, tn=LANES, name="mlstm_gates")
    gates_t = gates[:, :2 * H].T
    o = _mlstm(proj, gates_t, b_if, head_g)
    return _matmul_ws(o, w_out, layer, w_out.shape[2], residual=x, out_dtype=f32, tm=1024, tn=512, name="mlstm_out")


def _memattn_layer(x, gain, mem, mem_g, wq, wk, wv, wo, q_g, k_g):
    wkv = jnp.concatenate([wk, wv], axis=1).astype(bf16)
    kv = _matmul(mem, wkv, gain=mem_g, out_dtype=f32, tm=N_MEM, tn=512, name="mem_kv")
    return _memattn(x, gain, wq.astype(bf16), kv, q_g, k_g, wo.astype(bf16), tm=256)


def _alibi_slopes(n):
    return np.asarray([2.0 ** (-8.0 * (h + 1) / n) for h in range(n)], dtype=np.float32)


def _nsa_shared_kv(x, kv_norm_g, kv_w, cmp_pe, cmp_w1, cmp_w2, k_norm_g):
    G, dh = B_KV_HEADS, B_HEAD_DIM
    t_len = x.shape[0]
    ones = jnp.ones((G, 1, dh), f32)
    zeros = jnp.zeros((G, 1, dh), f32)
    head_gain = jnp.concatenate([ones, ones, ones * k_norm_g[1], ones, ones * k_norm_g[2], ones], axis=0)
    norm_flag = jnp.concatenate([zeros, zeros, ones, zeros, ones, zeros], axis=0)
    h, = _rmsnorm_bf16(x, [kv_norm_g], tm=512, name="nsa_kv_norm")
    kvh = _proj_heads(h, kv_w[None], 0, kv_w.shape[1], head_gain, norm_flag, scale=1.0, tm=1024, name="nsa_kv_proj")
    zr = kvh[:2 * G].reshape(2, G, t_len // CMP_STRIDE, CMP_STRIDE * dh)
    cmp_kv = _compress(zr, cmp_pe.reshape(2, 1, CMP_BLOCK * dh), cmp_w1.astype(bf16), cmp_w2.astype(bf16),
                       k_norm_g[0].reshape(1, dh))
    return cmp_kv, kvh


def _nsa_layer(x, gain, shared, w_in, layer, q_norm_g, w_out):
    cmp_kv, kvh = shared
    H, G, R, dh = B_HEADS, B_KV_HEADS, B_GROUP, B_HEAD_DIM
    ones = jnp.ones((H, 1, dh), f32)
    h, = _rmsnorm_bf16(x, [gain], tm=512, name="nsa_norm")
    qh = _proj_heads(h, w_in, layer, H * dh, ones * q_norm_g, ones, scale=dh ** -0.5 * LOG2E, tm=1024,
                     name="nsa_q_proj")
    wg = w_in[layer, :, H * dh:].reshape(-1, 3, G, R).transpose(0, 2, 1, 3).reshape(-1, G, 3 * R)
    wg = jnp.pad(wg, ((0, 0), (0, 0), (0, LANES - 3 * R))).reshape(-1, G * LANES).astype(bf16)
    gates = _matmul(h, wg, out_dtype=f32, tm=1024, tn=512, name="nsa_gates")
    slopes = jnp.asarray(np.repeat(_alibi_slopes(H).reshape(G, R, 1), Q_BLOCK, axis=2).reshape(G, 1, R * Q_BLOCK))
    o = _nsa_attention(qh, gates, slopes, cmp_kv, kvh)
    return _matmul_ws(o, w_out, layer, w_out.shape[2], residual=x, out_dtype=f32, tm=1024, tn=512, name="nsa_out")


def kernel(x, mem, norm_g, a_w_in, a_b_if, a_head_g, a_w_out, kv_norm_g, kv_w, cmp_pe, cmp_w1, cmp_w2, k_norm_g,
           b_w_in, b_q_norm_g, b_w_out, mem_norm_g, mem_wq, mem_wk, mem_wv, mem_wo, mem_q_g, mem_k_g, moe_wg,
           moe_bg, moe_we, moe_be, moe_w1, moe_w3, moe_w2):
    bsz, t_len, d = x.shape
    assert (bsz, t_len, d) == (1, SEQ, D_MODEL)
    xs = x.reshape(t_len, d)
    ms = mem.reshape(N_MEM, d)
    depth = norm_g.shape[0]
    n_a = depth - depth // 2
    shared = None
    for layer in range(depth):
        if layer < n_a:
            xs = _mlstm_layer(xs, norm_g[layer, 0], a_w_in, layer, a_b_if[layer], a_head_g[layer], a_w_out)
        else:
            if layer == n_a:
                shared = _nsa_shared_kv(xs, kv_norm_g, kv_w, cmp_pe, cmp_w1, cmp_w2, k_norm_g)
            j = layer - n_a
            xs = _nsa_layer(xs, norm_g[layer, 0], shared, b_w_in, j, b_q_norm_g[j], b_w_out)
        xs = _memattn_layer(xs, norm_g[layer, 1], ms, mem_norm_g[layer], mem_wq[layer], mem_wk[layer],
                            mem_wv[layer], mem_wo[layer], mem_q_g[layer], mem_k_g[layer])
        xs = _hier_moe(xs, norm_g[layer, 2], moe_wg[layer], moe_bg[layer], moe_we[layer], moe_be[layer],
                       moe_w1, moe_w3, moe_w2, layer)
    return xs.reshape(bsz, t_len, d)
```

```python
import functools

import numpy as np
import jax
import jax.numpy as jnp
from jax import lax
from jax.experimental import pallas as pl
from jax.experimental.pallas import tpu as pltpu

f32 = jnp.float32
bf16 = jnp.bfloat16

V7X_VMEM_BYTES = 64 * 1024 * 1024
VMEM_LIMIT = V7X_VMEM_BYTES - 8 * 1024 * 1024
LANES = 128
BF16_SUBLANES = 16
LOG2E = 1.4426950408889634

D_MODEL = 4096
SEQ = 8192
N_MEM = 256
RMS_EPS = 1e-6
NEG_INF = -1e30
A_HEADS = 8
A_DQK = 256
A_DV = 512
A_CHUNK = 128
B_HEADS = 32
B_KV_HEADS = 4
B_HEAD_DIM = 128
B_GROUP = 8
CMP_BLOCK = 32
CMP_STRIDE = 16
CMP_HIDDEN = 256
SEL_BLOCK = 64
SEL_TOPK = 16
WINDOW = 512
Q_BLOCK = 128
SEL_FORCE = 1e6
MEM_HEADS = 4
MEM_HEAD_DIM = 128
N_GROUPS = 4
EXPERTS_PER_GROUP = 4
N_EXPERTS = 16
D_EXPERT = 1024
EXPERT_TOPK = 2
N_CMP_PAD = SEQ // CMP_STRIDE
N_SEL = SEQ // SEL_BLOCK
MOE_TM = 256
MOE_ROWS = EXPERT_TOPK * SEQ + N_EXPERTS * MOE_TM
MOE_TILES = MOE_ROWS // MOE_TM
SEL_WIDE_CHUNKS = 4


def _params(*sem):
    return pltpu.CompilerParams(dimension_semantics=sem, vmem_limit_bytes=VMEM_LIMIT)


def _rms(x, g):
    ms = jnp.mean(x * x, axis=-1, keepdims=True)
    return x * lax.rsqrt(ms + RMS_EPS) * g


def _nt(a, b):
    return lax.dot_general(a, b, (((1,), (1,)), ((), ())), preferred_element_type=f32)


def _dot(a, b):
    return jnp.dot(a, b, preferred_element_type=f32)


def _split3(a):
    a1 = a.astype(bf16)
    r1 = a - a1.astype(f32)
    a2 = r1.astype(bf16)
    a3 = (r1 - a2.astype(f32)).astype(bf16)
    return a1, a2, a3


def _dot_f32_exactrhs(a, b_bf16):
    a1, a2, a3 = _split3(a)
    return _dot(a3, b_bf16) + _dot(a2, b_bf16) + _dot(a1, b_bf16)


def _mm_kernel(*refs, norm, residual):
    it = iter(refs)
    a_ref = next(it)
    g_ref = next(it) if norm else None
    w_ref = next(it)
    r_ref = next(it) if residual else None
    o_ref = next(it)
    h_scr = next(it) if norm else None
    if norm:
        @pl.when(pl.program_id(1) == 0)
        def _():
            h_scr[...] = _rms(a_ref[...], g_ref[...]).astype(bf16)
        a = h_scr[...]
    else:
        a = a_ref[...]
    acc = _dot(a, w_ref[...])
    if residual:
        acc = acc + r_ref[...]
    o_ref[...] = acc.astype(o_ref.dtype)


def _matmul(a, w, *, gain=None, residual=None, out_dtype, tm, tn, name):
    m, k = a.shape
    n = w.shape[1]
    norm = gain is not None
    ins = [a]
    specs = [pl.BlockSpec((tm, k), lambda i, j: (i, 0))]
    if norm:
        ins.append(gain.reshape(1, k))
        specs.append(pl.BlockSpec((1, k), lambda i, j: (0, 0)))
    ins.append(w)
    specs.append(pl.BlockSpec((k, tn), lambda i, j: (0, j)))
    if residual is not None:
        ins.append(residual)
        specs.append(pl.BlockSpec((tm, tn), lambda i, j: (i, j)))
    return pl.pallas_call(
        functools.partial(_mm_kernel, norm=norm, residual=residual is not None),
        grid=(m // tm, n // tn),
        in_specs=specs,
        out_specs=pl.BlockSpec((tm, tn), lambda i, j: (i, j)),
        out_shape=jax.ShapeDtypeStruct((m, n), out_dtype),
        scratch_shapes=[pltpu.VMEM((tm, k), bf16)] if norm else [],
        compiler_params=_params("parallel", "arbitrary"),
        name=name,
    )(*ins)


def _rmsnorm_kernel(x_ref, g_ref, *o_refs):
    x = x_ref[...]
    y = x * lax.rsqrt(jnp.mean(x * x, axis=-1, keepdims=True) + RMS_EPS)
    for n, o_ref in enumerate(o_refs):
        o_ref[...] = (y * g_ref[n:n + 1, :]).astype(o_ref.dtype)


def _rmsnorm_bf16(x, gains, *, tm, name):
    m, k = x.shape
    n = len(gains)
    row = lambda i: (i, 0)
    return pl.pallas_call(
        _rmsnorm_kernel,
        grid=(m // tm,),
        in_specs=[pl.BlockSpec((tm, k), row), pl.BlockSpec((n, k), lambda i: (0, 0))],
        out_specs=[pl.BlockSpec((tm, k), row)] * n,
        out_shape=[jax.ShapeDtypeStruct((m, k), bf16)] * n,
        compiler_params=_params("parallel"),
        name=name,
    )(x, jnp.stack(gains))


def _mm_ws_kernel(*refs, residual):
    it = iter(refs)
    a_ref = next(it)
    w_ref = next(it)
    r_ref = next(it) if residual else None
    o_ref = next(it)
    wb_scr = next(it)

    @pl.when(pl.program_id(1) == 0)
    def _():
        wb_scr[...] = w_ref[...].astype(bf16)

    acc = _dot(a_ref[...], wb_scr[...])
    if residual:
        acc = acc + r_ref[...]
    o_ref[...] = acc.astype(o_ref.dtype)


def _matmul_ws(a, w, layer, n_cols, *, residual=None, out_dtype, tm, tn, name):
    m, k = a.shape
    ins = [a, w]
    specs = [pl.BlockSpec((tm, k), lambda j, i: (i, 0)), pl.BlockSpec((None, k, tn), lambda j, i: (layer, 0, j))]
    if residual is not None:
        ins.append(residual)
        specs.append(pl.BlockSpec((tm, tn), lambda j, i: (i, j)))
    return pl.pallas_call(
        functools.partial(_mm_ws_kernel, residual=residual is not None),
        grid=(n_cols // tn, m // tm),
        in_specs=specs,
        out_specs=pl.BlockSpec((tm, tn), lambda j, i: (i, j)),
        out_shape=jax.ShapeDtypeStruct((m, n_cols), out_dtype),
        scratch_shapes=[pltpu.VMEM((k, tn), bf16)],
        compiler_params=_params("arbitrary", "arbitrary"),
        name=name,
    )(*ins)


def _proj_heads_kernel(a_ref, w_ref, hg_ref, nf_ref, o_ref, wb_scr, *, heads_per_tile, scale):
    @pl.when(pl.program_id(1) == 0)
    def _():
        wb_scr[...] = w_ref[...].astype(bf16)

    acc = _dot(a_ref[...], wb_scr[...])
    for hd in range(heads_per_tile):
        z = acc[:, hd * LANES:(hd + 1) * LANES]
        zn = _rms(z, hg_ref[hd]) * scale
        o_ref[hd] = jnp.where(nf_ref[hd] > 0.5, zn, z).astype(o_ref.dtype)


def _proj_heads(a, w, layer, n_cols, head_gain, norm_flag, *, scale, tm, name):
    m, k = a.shape
    hpt = 4
    tn = hpt * LANES
    nh = n_cols // LANES
    return pl.pallas_call(
        functools.partial(_proj_heads_kernel, heads_per_tile=hpt, scale=scale),
        grid=(n_cols // tn, m // tm),
        in_specs=[
            pl.BlockSpec((tm, k), lambda j, i: (i, 0)),
            pl.BlockSpec((None, k, tn), lambda j, i: (layer, 0, j)),
            pl.BlockSpec((hpt, 1, LANES), lambda j, i: (j, 0, 0)),
            pl.BlockSpec((hpt, 1, LANES), lambda j, i: (j, 0, 0)),
        ],
        out_specs=pl.BlockSpec((hpt, tm, LANES), lambda j, i: (j, i, 0)),
        out_shape=jax.ShapeDtypeStruct((nh, m, LANES), bf16),
        scratch_shapes=[pltpu.VMEM((k, tn), bf16)],
        compiler_params=_params("arbitrary", "arbitrary"),
        name=name,
    )(a, w, head_gain, norm_flag)


def _mlstm_kernel(q_ref, k_ref, v_ref, og_ref, gt_ref, bif_ref, hg_ref, o_ref, c_scr, n_scr, m_scr):
    c = pl.program_id(0)
    h = pl.program_id(1)
    L, dk = A_CHUNK, A_DQK
    qscale = dk ** -0.5

    @pl.when(c == 0)
    def _():
        c_scr[h] = jnp.zeros(c_scr.shape[1:], f32)
        n_scr[h] = jnp.zeros(n_scr.shape[1:], f32)
        m_scr[h] = jnp.zeros(m_scr.shape[1:], f32)

    q = q_ref[...]
    k = k_ref[...]
    v = v_ref[...]
    li = gt_ref[pl.ds(h, 1), :] + bif_ref[pl.ds(h, 1), :]
    xf = gt_ref[pl.ds(h + A_HEADS, 1), :] + bif_ref[pl.ds(h + A_HEADS, 1), :]
    lf = jnp.minimum(xf, 0.0) - jnp.log1p(jnp.exp(-jnp.abs(xf)))

    ii = lax.broadcasted_iota(jnp.int32, (L, L), 0)
    jj = lax.broadcasted_iota(jnp.int32, (L, L), 1)
    upper = jnp.where(ii <= jj, 1.0, 0.0).astype(bf16)
    b_row = _dot_f32_exactrhs(jnp.broadcast_to(lf, (8, L)), upper)[0:1]
    eye = ii == jj

    def to_col(row):
        return jnp.sum(jnp.where(eye, row, 0.0), axis=1, keepdims=True)

    b_col = to_col(b_row)
    causal = jj <= ii
    dmat = b_col - b_row + li
    m_prev = m_scr[h]
    m_inter = b_col + m_prev
    m_t = jnp.maximum(m_inter, jnp.max(jnp.where(causal, dmat, -jnp.inf), axis=1, keepdims=True))
    dexp = jnp.where(causal, jnp.exp(dmat - m_t), 0.0)
    s = _nt(q, k) * qscale * dexp
    w_inter = jnp.exp(m_inter - m_t)
    ct = c_scr[h]
    n_row = n_scr[h]
    inter = _dot(q, ct.astype(bf16)) * qscale
    num = _dot(s.astype(bf16), v) + w_inter * inter
    qn = jnp.sum(q.astype(f32) * n_row, axis=1, keepdims=True) * qscale
    den = jnp.sum(s, axis=1, keepdims=True) + w_inter * qn
    hc = num / jnp.maximum(jnp.abs(den), jnp.exp(-m_t))

    b_end = b_row[:, L - 1:L]
    w_log = b_end - b_row + li
    m_new = jnp.maximum(b_end + m_prev, jnp.max(w_log, axis=1, keepdims=True))
    decay = jnp.exp(b_end + m_prev - m_new)
    w_wr = jnp.exp(w_log - m_new)
    vw = (v.astype(f32) * to_col(w_wr)).astype(bf16)
    kt = k.astype(f32).T.astype(bf16)
    c_scr[h] = decay * ct + _dot(kt, vw)
    n_scr[h] = decay * n_row + _dot(jnp.broadcast_to(w_wr, (8, L)).astype(bf16), k)[0:1]
    m_scr[h] = m_new

    hn = _rms(hc, hg_ref[...])
    o_ref[...] = (jax.nn.sigmoid(og_ref[...].astype(f32)) * hn).astype(o_ref.dtype)


def _mlstm(proj, gates_t, b_if, head_g):
    t_len = proj.shape[0]
    H, dk, dv, L = A_HEADS, A_DQK, A_DV, A_CHUNK
    nk = H * dk // dk
    return pl.pallas_call(
        _mlstm_kernel,
        grid=(t_len // L, H),
        in_specs=[
            pl.BlockSpec((L, dk), lambda c, h: (c, h)),
            pl.BlockSpec((L, dk), lambda c, h: (c, nk + h)),
            pl.BlockSpec((L, dv), lambda c, h: (c, 2 * H * dk // dv + h)),
            pl.BlockSpec((L, dv), lambda c, h: (c, 2 * H * dk // dv + H + h)),
            pl.BlockSpec((2 * H, L), lambda c, h: (0, c)),
            pl.BlockSpec((2 * H, 1), lambda c, h: (0, 0)),
            pl.BlockSpec((None, 1, dv), lambda c, h: (h, 0, 0)),
        ],
        out_specs=pl.BlockSpec((L, dv), lambda c, h: (c, h)),
        out_shape=jax.ShapeDtypeStruct((t_len, H * dv), bf16),
        scratch_shapes=[pltpu.VMEM((H, dk, dv), f32), pltpu.VMEM((H, 1, dk), f32), pltpu.VMEM((H, 1, 1), f32)],
        compiler_params=_params("arbitrary", "arbitrary"),
        name="mlstm",
    )(proj, proj, proj, proj, gates_t, b_if.reshape(2 * H, 1), head_g.reshape(H, 1, dv))


def _memattn_kernel(x_ref, g_ref, wq_ref, kv_ref, qg_ref, kg_ref, wo_ref, o_ref):
    x = x_ref[...]
    h = _rms(x, g_ref[...]).astype(bf16)
    qp = _dot(h, wq_ref[...])
    dh = MEM_HEAD_DIM
    nkv = MEM_HEADS * dh
    outs = []
    for hd in range(MEM_HEADS):
        qn = _rms(qp[:, hd * dh:(hd + 1) * dh], qg_ref[...]).astype(bf16)
        kn = _rms(kv_ref[:, hd * dh:(hd + 1) * dh], kg_ref[...]).astype(bf16)
        vh = kv_ref[:, nkv + hd * dh:nkv + (hd + 1) * dh].astype(bf16)
        s = _nt(qn, kn) * (dh ** -0.5)
        e = jnp.exp(s - jnp.max(s, axis=1, keepdims=True))
        p = e / jnp.sum(e, axis=1, keepdims=True)
        outs.append(_dot(p.astype(bf16), vh).astype(bf16))
    o = jnp.concatenate(outs, axis=1)
    o_ref[...] = x + _dot(o, wo_ref[...])


def _memattn(x, gain, wq, kv, qg, kg, wo, *, tm):
    t_len, d = x.shape
    md = MEM_HEADS * MEM_HEAD_DIM
    return pl.pallas_call(
        _memattn_kernel,
        grid=(t_len // tm,),
        in_specs=[
            pl.BlockSpec((tm, d), lambda i: (i, 0)),
            pl.BlockSpec((1, d), lambda i: (0, 0)),
            pl.BlockSpec((d, md), lambda i: (0, 0)),
            pl.BlockSpec((N_MEM, 2 * md), lambda i: (0, 0)),
            pl.BlockSpec((1, MEM_HEAD_DIM), lambda i: (0, 0)),
            pl.BlockSpec((1, MEM_HEAD_DIM), lambda i: (0, 0)),
            pl.BlockSpec((md, d), lambda i: (0, 0)),
        ],
        out_specs=pl.BlockSpec((tm, d), lambda i: (i, 0)),
        out_shape=jax.ShapeDtypeStruct((t_len, d), f32),
        compiler_params=_params("parallel"),
        name="memattn",
    )(x, gain.reshape(1, d), wq, kv, qg.reshape(1, -1), kg.reshape(1, -1), wo)


def _router_kernel(x_ref, g_ref, wr_ref, br_ref, h_ref, comb_ref, rank_ref, cnt_ref, carry_scr):
    tm = x_ref.shape[0]

    @pl.when(pl.program_id(0) == 0)
    def _():
        carry_scr[...] = jnp.zeros(carry_scr.shape, f32)

    hf = _rms(x_ref[...], g_ref[...])
    h_ref[...] = hf
    h1, h2, _ = _split3(hf)
    w1, w2, _ = _split3(wr_ref[...])
    logits = _dot(h2, w1) + _dot(h1, w2) + _dot(h1, w1) + br_ref[...]
    col = [logits[:, i:i + 1] for i in range(N_GROUPS + N_EXPERTS)]
    gl = col[:N_GROUPS]
    gm = functools.reduce(jnp.maximum, gl)
    ge = [jnp.exp(z - gm) for z in gl]
    gs = functools.reduce(jnp.add, ge)
    gp = [z / gs for z in ge]
    g_top = functools.reduce(jnp.maximum, gp)
    sel_g = []
    taken = jnp.zeros(g_top.shape, jnp.bool_)
    for z in gp:
        hit = (z == g_top) & jnp.logical_not(taken)
        sel_g.append(hit)
        taken = taken | hit
    E = EXPERTS_PER_GROUP
    el = []
    for e in range(E):
        acc = jnp.zeros_like(g_top)
        for g in range(N_GROUPS):
            acc = acc + jnp.where(sel_g[g], col[N_GROUPS + g * E + e], 0.0)
        el.append(acc)
    em = functools.reduce(jnp.maximum, el)
    ee = [jnp.exp(z - em) for z in el]
    es = functools.reduce(jnp.add, ee)
    ep = [z / es for z in ee]
    sel_e = []
    for e in range(E):
        rank = jnp.zeros(g_top.shape, jnp.int32)
        for e2 in range(E):
            if e2 == e:
                continue
            ahead = (ep[e2] > ep[e]) | ((ep[e2] == ep[e]) & (e2 < e))
            rank = rank + ahead.astype(jnp.int32)
        sel_e.append(rank < 2)
    top_sum = functools.reduce(jnp.add, [jnp.where(sel_e[e], ep[e], 0.0) for e in range(E)])
    lane = lax.broadcasted_iota(jnp.int32, comb_ref.shape, 1)
    comb = jnp.zeros(comb_ref.shape, f32)
    for g in range(N_GROUPS):
        for e in range(E):
            wge = jnp.where(sel_g[g] & sel_e[e], ep[e] / top_sum * g_top, 0.0)
            comb = jnp.where(lane == g * E + e, wge, comb)
    comb_ref[...] = comb

    member = jnp.where(comb > 0.0, 1.0, 0.0)
    r_i = lax.broadcasted_iota(jnp.int32, (tm, tm), 0)
    c_i = lax.broadcasted_iota(jnp.int32, (tm, tm), 1)
    earlier = jnp.where(c_i < r_i, 1.0, 0.0).astype(bf16)
    carry = carry_scr[...]
    rank_ref[...] = _dot(earlier, member.astype(bf16)) + carry
    carry = carry + jnp.sum(member, axis=0, keepdims=True)
    carry_scr[...] = carry
    cnt_ref[...] = carry


def _router(x, gain, w_r, b_r, *, tm):
    t_len, d = x.shape
    row = lambda i: (i, 0)
    return pl.pallas_call(
        _router_kernel,
        grid=(t_len // tm,),
        in_specs=[
            pl.BlockSpec((tm, d), row),
            pl.BlockSpec((1, d), lambda i: (0, 0)),
            pl.BlockSpec((d, LANES), lambda i: (0, 0)),
            pl.BlockSpec((1, LANES), lambda i: (0, 0)),
        ],
        out_specs=[pl.BlockSpec((tm, d), row), pl.BlockSpec((tm, LANES), row),
                   pl.BlockSpec((tm, LANES), row), pl.BlockSpec((1, LANES), lambda i: (0, 0))],
        out_shape=[jax.ShapeDtypeStruct((t_len, d), f32), jax.ShapeDtypeStruct((t_len, LANES), f32),
                   jax.ShapeDtypeStruct((t_len, LANES), f32), jax.ShapeDtypeStruct((1, LANES), f32)],
        scratch_shapes=[pltpu.VMEM((1, LANES), f32)],
        compiler_params=_params("arbitrary"),
        name="moe_router",
    )(x, gain.reshape(1, d), w_r, b_r)


def _row(ref, r):
    return ref.at[pl.ds(r, 1)]


def _meta_kernel(comb_ref, rank_ref, cnt_ref, pos_ref, w_ref, te_ref, nu_ref):
    comb = comb_ref[...]
    lane1 = lax.broadcasted_iota(jnp.int32, (1, LANES), 1)
    cnt = cnt_ref[...]
    gsz = jnp.ceil(cnt * (1.0 / MOE_TM)) * MOE_TM
    e_r = lax.broadcasted_iota(jnp.int32, (LANES, LANES), 0)
    e_c = lax.broadcasted_iota(jnp.int32, (LANES, LANES), 1)
    before = jnp.where(e_r < e_c, 1.0, 0.0).astype(bf16)
    goff = _dot_f32_exactrhs(jnp.broadcast_to(gsz, (8, LANES)), before)[0:1]
    member = comb > 0.0
    pos = jnp.where(member, goff + rank_ref[...], -1.0)
    pa = jnp.max(pos, axis=1, keepdims=True)
    is_a = member & (pos == pa)
    wa = jnp.sum(jnp.where(is_a, comb, 0.0), axis=1, keepdims=True)
    rest = member & jnp.logical_not(is_a)
    pb = jnp.max(jnp.where(rest, pos, -1.0), axis=1, keepdims=True)
    wb = jnp.sum(jnp.where(rest, comb, 0.0), axis=1, keepdims=True)
    lane = lax.broadcasted_iota(jnp.int32, comb.shape, 1)
    pos_ref[...] = jnp.where(lane == 0, pa, jnp.where(lane == 1, pb, -1.0)).astype(jnp.int32)
    w_ref[...] = jnp.where(lane == 0, wa, jnp.where(lane == 1, wb, 0.0))
    valid_e = lane1 < N_EXPERTS
    gend = goff + gsz
    tile_start = lax.broadcasted_iota(jnp.int32, (LANES, LANES), 0).astype(f32) * MOE_TM
    te = jnp.sum(jnp.where(valid_e & (gend <= tile_start), 1.0, 0.0), axis=1, keepdims=True)
    last = jnp.max(jnp.where(valid_e & (gsz > 0.0), lane1.astype(f32), 0.0), axis=1, keepdims=True)
    te_ref[...] = jnp.minimum(te, last).astype(jnp.int32)
    nu_ref[...] = (jnp.sum(jnp.where(valid_e, gsz, 0.0), axis=1, keepdims=True) * (1.0 / MOE_TM)).astype(jnp.int32)


def _moe_meta(comb, rank, cnt, *, tm):
    t_len = comb.shape[0]
    row = lambda i: (i, 0)
    fixed = lambda i: (0, 0)
    return pl.pallas_call(
        _meta_kernel,
        grid=(t_len // tm,),
        in_specs=[pl.BlockSpec((tm, LANES), row), pl.BlockSpec((tm, LANES), row), pl.BlockSpec((1, LANES), fixed)],
        out_specs=[pl.BlockSpec((tm, LANES), row), pl.BlockSpec((tm, LANES), row),
                   pl.BlockSpec((LANES, 1), fixed), pl.BlockSpec((1, 1), fixed)],
        out_shape=[jax.ShapeDtypeStruct((t_len, LANES), jnp.int32), jax.ShapeDtypeStruct((t_len, LANES), f32),
                   jax.ShapeDtypeStruct((LANES, 1), jnp.int32), jax.ShapeDtypeStruct((1, 1), jnp.int32)],
        compiler_params=_params("arbitrary"),
        name="moe_meta",
    )(comb, rank, cnt)


def _invmap_kernel(pos_ref, src_ref):
    def zero(p, carry):
        src_ref[p] = 0
        return carry

    lax.fori_loop(0, src_ref.shape[0], zero, 0, unroll=16)

    def body(k, carry):
        for slot in range(EXPERT_TOPK):
            p = pos_ref[EXPERT_TOPK * k + slot]

            @pl.when(p >= 0)
            def _():
                src_ref[p] = k
        return carry

    lax.fori_loop(0, pos_ref.shape[0] // EXPERT_TOPK, body, 0, unroll=8)


def _moe_invmap(pos_flat):
    return pl.pallas_call(
        _invmap_kernel,
        in_specs=[pl.BlockSpec(memory_space=pltpu.SMEM)],
        out_specs=pl.BlockSpec(memory_space=pltpu.SMEM),
        out_shape=jax.ShapeDtypeStruct((MOE_ROWS,), jnp.int32),
        name="moe_invmap",
    )(pos_flat)


def _gather_kernel(src_ref, h_ref, xs_ref, buf, sems):
    i = pl.program_id(0)

    def copy(tile, slot, k):
        return pltpu.make_async_copy(_row(h_ref, src_ref[tile * MOE_TM + k]), _row(buf.at[slot], k), sems.at[slot])

    def start_tile(tile, slot):
        def body(k, carry):
            copy(tile, slot, k).start()
            return carry

        lax.fori_loop(0, MOE_TM, body, 0, unroll=8)

    @pl.when(i == 0)
    def _():
        start_tile(0, 0)

    for slot in range(2):
        @pl.when(i % 2 == slot)
        def _():
            @pl.when(i + 1 < pl.num_programs(0))
            def _():
                start_tile(i + 1, 1 - slot)

            def wait(k, carry):
                copy(i, slot, k).wait()
                return carry

            lax.fori_loop(0, MOE_TM, wait, 0, unroll=8)
            xs_ref[...] = buf[slot].astype(xs_ref.dtype)


def _moe_gather(src, h):
    d = h.shape[1]
    return pl.pallas_call(
        _gather_kernel,
        grid_spec=pltpu.PrefetchScalarGridSpec(
            num_scalar_prefetch=1,
            grid=(MOE_TILES,),
            in_specs=[pl.BlockSpec(memory_space=pl.ANY)],
            out_specs=pl.BlockSpec((MOE_TM, d), lambda i, src: (i, 0)),
            scratch_shapes=[pltpu.VMEM((2, MOE_TM, d), f32), pltpu.SemaphoreType.DMA((2,))],
        ),
        out_shape=jax.ShapeDtypeStruct((MOE_ROWS, d), bf16),
        compiler_params=pltpu.CompilerParams(dimension_semantics=("arbitrary",), vmem_limit_bytes=VMEM_LIMIT,
                                             disable_bounds_checks=True),
        name="moe_gather",
    )(src, h)


def _expert_changed(te_ref, i):
    return (i == 0) | (te_ref[i] != te_ref[jnp.maximum(i - 1, 0)])


def _moe_hid_kernel(te_ref, nu_ref, xs_ref, w1_ref, w3_ref, o_ref, w1b, w3b):
    i = pl.program_id(1)

    @pl.when(_expert_changed(te_ref, i))
    def _():
        w1b[...] = w1_ref[...].astype(bf16)
        w3b[...] = w3_ref[...].astype(bf16)

    @pl.when(i < nu_ref[0])
    def _():
        x = xs_ref[...]
        u = _dot(x, w1b[...])
        g = _dot(x, w3b[...])
        o_ref[...] = (u * jax.nn.sigmoid(u) * g).astype(o_ref.dtype)

    @pl.when(i >= nu_ref[0])
    def _():
        o_ref[...] = jnp.zeros(o_ref.shape, o_ref.dtype)


def _moe_hid(te, nused, xs, w1, w3, layer, *, tn):
    _, ne, d, fdim = w1.shape
    used = lambda i, nu: jnp.where(i < nu[0], i, 0)
    return pl.pallas_call(
        _moe_hid_kernel,
        grid_spec=pltpu.PrefetchScalarGridSpec(
            num_scalar_prefetch=2,
            grid=(fdim // tn, MOE_TILES),
            in_specs=[
                pl.BlockSpec((MOE_TM, d), lambda j, i, te, nu: (used(i, nu), 0)),
                pl.BlockSpec((None, None, d, tn), lambda j, i, te, nu: (layer, te[i], 0, j)),
                pl.BlockSpec((None, None, d, tn), lambda j, i, te, nu: (layer, te[i], 0, j)),
            ],
            out_specs=pl.BlockSpec((MOE_TM, tn), lambda j, i, te, nu: (i, j)),
            scratch_shapes=[pltpu.VMEM((d, tn), bf16), pltpu.VMEM((d, tn), bf16)],
        ),
        out_shape=jax.ShapeDtypeStruct((MOE_ROWS, fdim), bf16),
        compiler_params=_params("arbitrary", "arbitrary"),
        name="moe_hid",
    )(te, nused, xs, w1, w3)


def _moe_w2_kernel(te_ref, nu_ref, hid_ref, w2_ref, o_ref, w2b):
    i = pl.program_id(0)

    @pl.when(_expert_changed(te_ref, i))
    def _():
        w2b[...] = w2_ref[...].astype(bf16)

    @pl.when(i < nu_ref[0])
    def _():
        o_ref[...] = _dot(hid_ref[...], w2b[...])

    @pl.when(i >= nu_ref[0])
    def _():
        o_ref[...] = jnp.zeros(o_ref.shape, o_ref.dtype)


def _moe_w2(te, nused, hid, w2, layer):
    _, ne, fdim, d = w2.shape
    used = lambda i, nu: jnp.where(i < nu[0], i, 0)
    return pl.pallas_call(
        _moe_w2_kernel,
        grid_spec=pltpu.PrefetchScalarGridSpec(
            num_scalar_prefetch=2,
            grid=(MOE_TILES,),
            in_specs=[
                pl.BlockSpec((MOE_TM, fdim), lambda i, te, nu: (used(i, nu), 0)),
                pl.BlockSpec((None, None, fdim, d), lambda i, te, nu: (layer, te[i], 0, 0),
                             pipeline_mode=pl.Buffered(1)),
            ],
            out_specs=pl.BlockSpec((MOE_TM, d), lambda i, te, nu: (i, 0)),
            scratch_shapes=[pltpu.VMEM((fdim, d), bf16)],
        ),
        out_shape=jax.ShapeDtypeStruct((MOE_ROWS, d), f32),
        compiler_params=_params("arbitrary"),
        name="moe_w2",
    )(te, nused, hid, w2)


def _moe_combine_kernel(pos_ref, x_ref, w_ref, ys_ref, o_ref, buf, sems):
    tm, d = x_ref.shape
    i = pl.program_id(0)

    def copies(tile, slot, k):
        out = []
        for which in range(EXPERT_TOPK):
            p = jnp.maximum(pos_ref[EXPERT_TOPK * (tile * tm + k) + which], 0)
            out.append(pltpu.make_async_copy(_row(ys_ref, p), _row(buf.at[slot, which], k), sems.at[slot]))
        return out

    def start_tile(tile, slot):
        def body(k, carry):
            for cp in copies(tile, slot, k):
                cp.start()
            return carry

        lax.fori_loop(0, tm, body, 0, unroll=8)

    @pl.when(i == 0)
    def _():
        start_tile(0, 0)

    for slot in range(2):
        @pl.when(i % 2 == slot)
        def _():
            @pl.when(i + 1 < pl.num_programs(0))
            def _():
                start_tile(i + 1, 1 - slot)

            def wait(k, carry):
                for cp in copies(i, slot, k):
                    cp.wait()
                return carry

            lax.fori_loop(0, tm, wait, 0, unroll=8)
            o_ref[...] = x_ref[...] + w_ref[:, 0:1] * buf[slot, 0] + w_ref[:, 1:2] * buf[slot, 1]


def _moe_combine(pos_flat, x, tokw, ys, *, tm):
    t_len, d = x.shape
    return pl.pallas_call(
        _moe_combine_kernel,
        grid_spec=pltpu.PrefetchScalarGridSpec(
            num_scalar_prefetch=1,
            grid=(t_len // tm,),
            in_specs=[
                pl.BlockSpec((tm, d), lambda i, pos: (i, 0)),
                pl.BlockSpec((tm, LANES), lambda i, pos: (i, 0)),
                pl.BlockSpec(memory_space=pl.ANY),
            ],
            out_specs=pl.BlockSpec((tm, d), lambda i, pos: (i, 0)),
            scratch_shapes=[pltpu.VMEM((2, EXPERT_TOPK, tm, d), f32), pltpu.SemaphoreType.DMA((2,))],
        ),
        out_shape=jax.ShapeDtypeStruct((t_len, d), f32),
        compiler_params=pltpu.CompilerParams(dimension_semantics=("arbitrary",), vmem_limit_bytes=VMEM_LIMIT,
                                             disable_bounds_checks=True),
        name="moe_combine",
    )(pos_flat, x, tokw, ys)


def _hier_moe(x, gain, wg, bg, we, be, w1, w3, w2, layer):
    d = x.shape[1]
    w_r = jnp.concatenate([wg, we.transpose(1, 0, 2).reshape(d, N_EXPERTS)], axis=1)
    w_r = jnp.pad(w_r, ((0, 0), (0, LANES - w_r.shape[1])))
    b_r = jnp.pad(jnp.concatenate([bg, be.reshape(-1)]), (0, LANES - N_GROUPS - N_EXPERTS)).reshape(1, LANES)
    h, comb, rank, cnt = _router(x, gain, w_r, b_r, tm=256)
    tokpos, tokw, te, nused = _moe_meta(comb, rank, cnt, tm=512)
    pos_flat = tokpos[:, :EXPERT_TOPK].reshape(-1)
    te = te.reshape(-1)
    nused = nused.reshape(-1)
    src = _moe_invmap(pos_flat)
    xs = _moe_gather(src, h)
    hid = _moe_hid(te, nused, xs, w1, w3, layer, tn=512)
    ys = _moe_w2(te, nused, hid, w2, layer)
    return _moe_combine(pos_flat, x, tokw, ys, tm=256)


def _compress_kernel(z_ref, pe_ref, w1_ref, w2_ref, kg_ref, o_ref):
    half = CMP_STRIDE * B_HEAD_DIM
    z = z_ref[...].astype(f32)
    zt = (z + pe_ref[:, :half]).astype(bf16)
    zb = (z + pe_ref[:, half:]).astype(bf16)
    top = _dot(zt, w1_ref[:half, :])
    bot = _dot(zb, w1_ref[half:, :])
    pre = top + pltpu.roll(bot, N_CMP_PAD - 1, axis=0)
    act = pre * (0.5 * (1.0 + jnp.tanh(np.sqrt(2.0 / np.pi).astype(np.float32) * (pre + 0.044715 * (pre * pre * pre)))))
    out = _dot(act.astype(bf16), w2_ref[...])
    normed = _rms(out, kg_ref[...])
    o_ref[...] = jnp.where(pl.program_id(0) == 0, normed, out).astype(o_ref.dtype)


def _compress(zr, pe_flat, w1, w2, kg):
    G = B_KV_HEADS
    return pl.pallas_call(
        _compress_kernel,
        grid=(2, G),
        in_specs=[
            pl.BlockSpec((None, None, N_CMP_PAD, CMP_STRIDE * B_HEAD_DIM), lambda w, g: (w, g, 0, 0)),
            pl.BlockSpec((None, 1, CMP_BLOCK * B_HEAD_DIM), lambda w, g: (w, 0, 0)),
            pl.BlockSpec((None, CMP_BLOCK * B_HEAD_DIM, CMP_HIDDEN), lambda w, g: (w, 0, 0)),
            pl.BlockSpec((None, CMP_HIDDEN, B_HEAD_DIM), lambda w, g: (w, 0, 0)),
            pl.BlockSpec((1, B_HEAD_DIM), lambda w, g: (0, 0)),
        ],
        out_specs=pl.BlockSpec((None, None, N_CMP_PAD, B_HEAD_DIM), lambda w, g: (w, g, 0, 0)),
        out_shape=jax.ShapeDtypeStruct((2, G, N_CMP_PAD, B_HEAD_DIM), bf16),
        compiler_params=_params("arbitrary", "arbitrary"),
        name="nsa_compress",
    )(zr, pe_flat, w1, w2, kg)


def _key_extra(pos):
    lane = lax.broadcasted_iota(jnp.int32, pos.shape, 1)
    lo = pos & (LANES - 1)
    return jnp.where(lane < 3, lo.astype(f32), jnp.where(lane < 6, (pos - lo).astype(f32), 0.0)).astype(bf16)


def _nsa_kernel(q_ref, gate_ref, sl_ref, kc_ref, vct_ref, ks_ref, vst_ref, kw_ref, vwt_ref, o_ref,
                qa_scr, selb_scr, m_scr, acc_scr, out_scr):
    c = pl.program_id(1)
    R, QB, dh = B_GROUP, Q_BLOCK, B_HEAD_DIM
    rows = R * QB
    t0 = c * QB
    slabs = [slice(r * QB, (r + 1) * QB) for r in range(R)]

    qa_scr[0:dh, :] = jnp.concatenate([q_ref[r].astype(f32).T for r in range(R)], axis=1).astype(bf16)
    s1, s2, s3 = [p.astype(f32) for p in _split3(sl_ref[...] * LOG2E)]
    rid = lax.broadcasted_iota(jnp.int32, (dh, rows), 0)
    qa_scr[dh:2 * dh, :] = jnp.where((rid == 0) | (rid == 3), s1,
                                     jnp.where((rid == 1) | (rid == 4), s2,
                                               jnp.where((rid == 2) | (rid == 5), s3, 0.0))).astype(bf16)

    kidx = lax.broadcasted_iota(jnp.int32, (QB, QB), 0)
    tidx = lax.broadcasted_iota(jnp.int32, (QB, QB), 1)
    causal = kidx <= tidx

    gate_t = jax.nn.sigmoid(gate_ref[...]).T

    def scores(k_rows, pos):
        return _dot(jnp.concatenate([k_rows, _key_extra(pos)], axis=1), qa_scr[...])

    def first_chunk(s_t, ok, v_t):
        ps = []
        for r in range(R):
            s_r = jnp.where(ok, s_t[:, slabs[r]], NEG_INF)
            m = jnp.max(s_r, axis=0, keepdims=True)
            m_scr[:, slabs[r]] = m
            ps.append(jnp.where(ok, jnp.exp2(s_r - m), 0.0).astype(bf16))
        acc_scr[...] = _dot(v_t, jnp.concatenate(ps, axis=1))

    def next_chunk(s_t, mask_bias, v_t):
        ps, alphas = [], []
        for r in range(R):
            s_r = s_t[:, slabs[r]] + mask_bias
            m_old = m_scr[:, slabs[r]]
            m_new = jnp.maximum(m_old, jnp.max(s_r, axis=0, keepdims=True))
            m_scr[:, slabs[r]] = m_new
            alphas.append(jnp.exp2(m_old - m_new))
            ps.append(jnp.exp2(s_r - m_new).astype(bf16))
        pv = _dot(v_t, jnp.concatenate(ps, axis=1))
        for r in range(R):
            acc_scr[:, slabs[r]] = alphas[r] * acc_scr[:, slabs[r]] + pv[:, slabs[r]]

    def emit(branch):
        for r in range(R):
            w = gate_t[branch * R + r:branch * R + r + 1, :] / acc_scr[dh:dh + 1, slabs[r]]
            out_scr[:, slabs[r]] = out_scr[:, slabs[r]] + acc_scr[0:dh, slabs[r]] * w

    cend = lax.broadcasted_iota(jnp.int32, (N_CMP_PAD, QB), 0) * CMP_STRIDE + (CMP_BLOCK - 1)
    ok_c = cend <= t0 + lax.broadcasted_iota(jnp.int32, (N_CMP_PAD, QB), 1)
    s_c = scores(kc_ref[...], cend)
    p_sum = jnp.zeros((N_CMP_PAD, QB), f32)
    ps = []
    for r in range(R):
        s_r = jnp.where(ok_c, s_c[:, slabs[r]], NEG_INF)
        e = jnp.where(ok_c, jnp.exp2(s_r - jnp.max(s_r, axis=0, keepdims=True)), 0.0)
        l = jnp.sum(e, axis=0, keepdims=True)
        p = e / jnp.where(l > 0.0, l, 1.0)
        p_sum = p_sum + p
        ps.append(p.astype(bf16))
    oc_t = _dot(vct_ref[...], jnp.concatenate(ps, axis=1))
    for r in range(R):
        out_scr[:, slabs[r]] = oc_t[:, slabs[r]] * gate_t[r:r + 1, :]

    b_i = lax.broadcasted_iota(jnp.int32, (N_SEL, N_CMP_PAD), 0) * SEL_BLOCK
    n_i = lax.broadcasted_iota(jnp.int32, (N_SEL, N_CMP_PAD), 1) * CMP_STRIDE
    overlap_t = jnp.where((n_i < b_i + SEL_BLOCK) & (n_i + (CMP_BLOCK - 1) >= b_i), 1.0, 0.0).astype(bf16)
    p1, p2, p3 = _split3(p_sum)
    imp = _dot(overlap_t, p3) + _dot(overlap_t, p2) + _dot(overlap_t, p1)
    tq = t0 + tidx
    cur = tq // SEL_BLOCK
    forced = (kidx == 0) | (kidx == cur) | (kidx == cur - 1)
    imp = jnp.where(forced, SEL_FORCE, imp)
    imp = jnp.where(kidx * SEL_BLOCK <= tq, imp, -1.0)
    blk_f = kidx.astype(f32)
    sel = jnp.zeros((N_SEL, QB), f32)
    for _ in range(SEL_TOPK):
        mx = jnp.max(imp, axis=0, keepdims=True)
        first = jnp.min(jnp.where(imp == mx, blk_f, float(N_SEL)), axis=0, keepdims=True)
        pick = blk_f == first
        sel = jnp.where(pick & (mx >= 0.0), 1.0, sel)
        imp = jnp.where(pick, -jnp.inf, imp)
    selb_scr[...] = jnp.where(sel > 0.5, 0.0, NEG_INF)

    def sel_bias(first_block, n_blocks):
        return jnp.concatenate([jnp.broadcast_to(selb_scr[pl.ds(first_block + b, 1), :], (SEL_BLOCK, QB))
                                for b in range(n_blocks)], axis=0)

    def chunk_pos(i):
        return i * QB + kidx

    off_c = pl.multiple_of(t0, QB)
    first_chunk(scores(ks_ref[pl.ds(off_c, QB), :], chunk_pos(c)), causal & (sel_bias(2 * c, 2) > -1.0),
                vst_ref[c])
    wide = SEL_WIDE_CHUNKS
    pos_w = lax.broadcasted_iota(jnp.int32, (wide * QB, QB), 0)
    blocks_per_chunk = QB // SEL_BLOCK

    def sel_wide(j, carry):
        off = pl.multiple_of(j * wide * QB, wide * QB)
        v_t = jnp.concatenate([vst_ref[wide * j + n] for n in range(wide)], axis=1)
        next_chunk(scores(ks_ref[pl.ds(off, wide * QB), :], off + pos_w),
                   sel_bias(wide * blocks_per_chunk * j, wide * blocks_per_chunk), v_t)
        return carry

    def sel_narrow(i, carry):
        off = pl.multiple_of(i * QB, QB)
        next_chunk(scores(ks_ref[pl.ds(off, QB), :], chunk_pos(i)), sel_bias(blocks_per_chunk * i, blocks_per_chunk),
                   vst_ref[i])
        return carry

    lax.fori_loop(0, c // wide, sel_wide, 0)
    lax.fori_loop((c // wide) * wide, c, sel_narrow, 0)
    emit(1)

    first_chunk(scores(kw_ref[pl.ds(off_c, QB), :], chunk_pos(c)), causal, vwt_ref[c])
    n_back = WINDOW // QB

    @pl.when(c >= n_back)
    def _():
        first = c - n_back
        off = pl.multiple_of(first * QB, QB)
        pos = off + lax.broadcasted_iota(jnp.int32, (n_back * QB, QB), 0)
        bias = jnp.concatenate([jnp.where(kidx > tidx, 0.0, NEG_INF), jnp.zeros(((n_back - 1) * QB, QB), f32)], axis=0)
        v_t = jnp.concatenate([vwt_ref[first + n] for n in range(n_back)], axis=1)
        next_chunk(scores(kw_ref[pl.ds(off, n_back * QB), :], pos), bias, v_t)

    @pl.when(c < n_back)
    def _():
        def win_body(i, carry):
            off = pl.multiple_of(i * QB, QB)
            next_chunk(scores(kw_ref[pl.ds(off, QB), :], chunk_pos(i)), jnp.zeros((QB, QB), f32), vwt_ref[i])
            return carry

        lax.fori_loop(0, c, win_body, 0)

    emit(2)

    for r in range(R):
        o_ref[:, r * dh:(r + 1) * dh] = out_scr[:, slabs[r]].T.astype(o_ref.dtype)


def _nsa_attention(qh, gates, slopes, cmp_kv, kvh):
    t_len = qh.shape[1]
    G, R, QB, dh = B_KV_HEADS, B_GROUP, Q_BLOCK, B_HEAD_DIM
    rows = R * QB
    nq = t_len // QB
    kc = cmp_kv[0]
    vc_t = cmp_kv[1].swapaxes(1, 2)
    ones_rows = jnp.zeros((G, nq, BF16_SUBLANES, QB), bf16).at[:, :, 0, :].set(1.0)

    def values_t(v):
        return jnp.concatenate([v.reshape(G, nq, QB, dh).swapaxes(2, 3), ones_rows], axis=2)

    vs_t = values_t(kvh[3 * G:4 * G])
    vw_t = values_t(kvh[5 * G:6 * G])
    dha = dh + BF16_SUBLANES

    def k_spec(base):
        return pl.BlockSpec((None, t_len, dh), lambda g, c: (base + g, 0, 0))

    vt_spec = pl.BlockSpec((None, nq, dha, QB), lambda g, c: (g, 0, 0, 0))
    return pl.pallas_call(
        _nsa_kernel,
        grid=(G, nq),
        in_specs=[
            pl.BlockSpec((R, QB, dh), lambda g, c: (g, c, 0)),
            pl.BlockSpec((QB, LANES), lambda g, c: (c, g)),
            pl.BlockSpec((None, 1, rows), lambda g, c: (g, 0, 0)),
            pl.BlockSpec((None, N_CMP_PAD, dh), lambda g, c: (g, 0, 0)),
            pl.BlockSpec((None, dh, N_CMP_PAD), lambda g, c: (g, 0, 0)),
            k_spec(2 * G), vt_spec, k_spec(4 * G), vt_spec,
        ],
        out_specs=pl.BlockSpec((QB, R * dh), lambda g, c: (c, g)),
        out_shape=jax.ShapeDtypeStruct((t_len, G * R * dh), bf16),
        scratch_shapes=[
            pltpu.VMEM((2 * dh, rows), bf16),
            pltpu.VMEM((N_SEL, QB), f32),
            pltpu.VMEM((1, rows), f32),
            pltpu.VMEM((dha, rows), f32),
            pltpu.VMEM((dh, rows), f32),
        ],
        compiler_params=_params("parallel", "arbitrary"),
        name="nsa_attention",
    )(qh, gates, slopes, kc, vc_t, kvh, vs_t, kvh, vw_t)


def _mlstm_layer(x, gain, w_in, layer, b_if, head_g, w_out):
    H, dk, dv = A_HEADS, A_DQK, A_DV
    n_main = 2 * H * dk + 2 * H * dv
    w_gate = jnp.pad(w_in[layer, :, n_main:], ((0, 0), (0, LANES - 2 * H))).astype(bf16)
    h, = _rmsnorm_bf16(x, [gain], tm=512, name="mlstm_norm")
    proj = _matmul_ws(h, w_in, layer, n_main, out_dtype=bf16, tm=1024, tn=512, name="mlstm_in")
    gates = _matmul(h, w_gate, out_dtype=f32, tm=1024, tn=LANES, name="mlstm_gates")
    gates_t = gates[:, :2 * H].T
    o = _mlstm(proj, gates_t, b_if, head_g)
    return _matmul_ws(o, w_out, layer, w_out.shape[2], residual=x, out_dtype=f32, tm=1024, tn=512, name="mlstm_out")


def _memattn_layer(x, gain, mem, mem_g, wq, wk, wv, wo, q_g, k_g):
    wkv = jnp.concatenate([wk, wv], axis=1).astype(bf16)
    kv = _matmul(mem, wkv, gain=mem_g, out_dtype=f32, tm=N_MEM, tn=512, name="mem_kv")
    return _memattn(x, gain, wq.astype(bf16), kv, q_g, k_g, wo.astype(bf16), tm=256)


def _alibi_slopes(n):
    return np.asarray([2.0 ** (-8.0 * (h + 1) / n) for h in range(n)], dtype=np.float32)


def _nsa_shared_kv(x, kv_norm_g, kv_w, cmp_pe, cmp_w1, cmp_w2, k_norm_g):
    G, dh = B_KV_HEADS, B_HEAD_DIM
    t_len = x.shape[0]
    ones = jnp.ones((G, 1, dh), f32)
    zeros = jnp.zeros((G, 1, dh), f32)
    head_gain = jnp.concatenate([ones, ones, ones * k_norm_g[1], ones, ones * k_norm_g[2], ones], axis=0)
    norm_flag = jnp.concatenate([zeros, zeros, ones, zeros, ones, zeros], axis=0)
    h, = _rmsnorm_bf16(x, [kv_norm_g], tm=512, name="nsa_kv_norm")
    kvh = _proj_heads(h, kv_w[None], 0, kv_w.shape[1], head_gain, norm_flag, scale=1.0, tm=1024, name="nsa_kv_proj")
    zr = kvh[:2 * G].reshape(2, G, t_len // CMP_STRIDE, CMP_STRIDE * dh)
    cmp_kv = _compress(zr, cmp_pe.reshape(2, 1, CMP_BLOCK * dh), cmp_w1.astype(bf16), cmp_w2.astype(bf16),
                       k_norm_g[0].reshape(1, dh))
    return cmp_kv, kvh


def _nsa_layer(x, gain, shared, w_in, layer, q_norm_g, w_out):
    cmp_kv, kvh = shared
    H, G, R, dh = B_HEADS, B_KV_HEADS, B_GROUP, B_HEAD_DIM
    ones = jnp.ones((H, 1, dh), f32)
    h, = _rmsnorm_bf16(x, [gain], tm=512, name="nsa_norm")
    qh = _proj_heads(h, w_in, layer, H * dh, ones * q_norm_g, ones, scale=dh ** -0.5 * LOG2E, tm=1024,
                     name="nsa_q_proj")
    wg = w_in[layer, :, H * dh:].reshape(-1, 3, G, R).transpose(0, 2, 1, 3).reshape(-1, G, 3 * R)
    wg = jnp.pad(wg, ((0, 0), (0, 0), (0, LANES - 3 * R))).reshape(-1, G * LANES).astype(bf16)
    gates = _matmul(h, wg, out_dtype=f32, tm=1024, tn=512, name="nsa_gates")
    slopes = jnp.asarray(np.repeat(_alibi_slopes(H).reshape(G, R, 1), Q_BLOCK, axis=2).reshape(G, 1, R * Q_BLOCK))
    o = _nsa_attention(qh, gates, slopes, cmp_kv, kvh)
    return _matmul_ws(o, w_out, layer, w_out.shape[2], residual=x, out_dtype=f32, tm=1024, tn=512, name="nsa_out")


def kernel(x, mem, norm_g, a_w_in, a_b_if, a_head_g, a_w_out, kv_norm_g, kv_w, cmp_pe, cmp_w1, cmp_w2, k_norm_g,
           b_w_in, b_q_norm_g, b_w_out, mem_norm_g, mem_wq, mem_wk, mem_wv, mem_wo, mem_q_g, mem_k_g, moe_wg,
           moe_bg, moe_we, moe_be, moe_w1, moe_w3, moe_w2):
    bsz, t_len, d = x.shape
    assert (bsz, t_len, d) == (1, SEQ, D_MODEL)
    xs = x.reshape(t_len, d)
    ms = mem.reshape(N_MEM, d)
    depth = norm_g.shape[0]
    n_a = depth - depth // 2
    shared = None
    for layer in range(depth):
        if layer < n_a:
            xs = _mlstm_layer(xs, norm_g[layer, 0], a_w_in, layer, a_b_if[layer], a_head_g[layer], a_w_out)
        else:
            if layer == n_a:
                shared = _nsa_shared_kv(xs, kv_norm_g, kv_w, cmp_pe, cmp_w1, cmp_w2, k_norm_g)
            j = layer - n_a
            xs = _nsa_layer(xs, norm_g[layer, 0], shared, b_w_in, j, b_q_norm_g[j], b_w_out)
        xs = _memattn_layer(xs, norm_g[layer, 1], ms, mem_norm_g[layer], mem_wq[layer], mem_wk[layer],
                            mem_wv[layer], mem_wo[layer], mem_q_g[layer], mem_k_g[layer])
        xs = _hier_moe(xs, norm_g[layer, 2], moe_wg[layer], moe_bg[layer], moe_we[layer], moe_be[layer],
                       moe_w1, moe_w3, moe_w2, layer)
    return xs.reshape(bsz, t_len, d)
```

```python
import functools

import numpy as np
import jax
import jax.numpy as jnp
from jax import lax
from jax.experimental import pallas as pl
from jax.experimental.pallas import tpu as pltpu

f32 = jnp.float32
bf16 = jnp.bfloat16

V7X_VMEM_BYTES = 64 * 1024 * 1024
VMEM_LIMIT = V7X_VMEM_BYTES - 8 * 1024 * 1024
LANES = 128
BF16_SUBLANES = 16
LOG2E = 1.4426950408889634

D_MODEL = 4096
SEQ = 8192
N_MEM = 256
RMS_EPS = 1e-6
NEG_INF = -1e30
A_HEADS = 8
A_DQK = 256
A_DV = 512
A_CHUNK = 128
B_HEADS = 32
B_KV_HEADS = 4
B_HEAD_DIM = 128
B_GROUP = 8
CMP_BLOCK = 32
CMP_STRIDE = 16
CMP_HIDDEN = 256
SEL_BLOCK = 64
SEL_TOPK = 16
WINDOW = 512
Q_BLOCK = 128
SEL_FORCE = 1e6
MEM_HEADS = 4
MEM_HEAD_DIM = 128
N_GROUPS = 4
EXPERTS_PER_GROUP = 4
N_EXPERTS = 16
D_EXPERT = 1024
EXPERT_TOPK = 2
N_CMP_PAD = SEQ // CMP_STRIDE
N_SEL = SEQ // SEL_BLOCK
MOE_TM = 256
MOE_ROWS = EXPERT_TOPK * SEQ + N_EXPERTS * MOE_TM
MOE_TILES = MOE_ROWS // MOE_TM
SEL_WIDE_CHUNKS = 4
MLSTM_HEADS_PER_STEP = 2
assert SEQ & (SEQ - 1) == 0
WS_TM, WS_TN = 512, 1024


def _params(*sem):
    return pltpu.CompilerParams(dimension_semantics=sem, vmem_limit_bytes=VMEM_LIMIT)


def _rms(x, g):
    ms = jnp.mean(x * x, axis=-1, keepdims=True)
    return x * lax.rsqrt(ms + RMS_EPS) * g


def _nt(a, b):
    return lax.dot_general(a, b, (((1,), (1,)), ((), ())), preferred_element_type=f32)


def _dot(a, b):
    return jnp.dot(a, b, preferred_element_type=f32)


def _split3(a):
    a1 = a.astype(bf16)
    r1 = a - a1.astype(f32)
    a2 = r1.astype(bf16)
    a3 = (r1 - a2.astype(f32)).astype(bf16)
    return a1, a2, a3


def _dot_f32_exactrhs(a, b_bf16):
    a1, a2, a3 = _split3(a)
    return _dot(a3, b_bf16) + _dot(a2, b_bf16) + _dot(a1, b_bf16)


def _mm_kernel(*refs, norm, residual):
    it = iter(refs)
    a_ref = next(it)
    g_ref = next(it) if norm else None
    w_ref = next(it)
    r_ref = next(it) if residual else None
    o_ref = next(it)
    h_scr = next(it) if norm else None
    if norm:
        @pl.when(pl.program_id(1) == 0)
        def _():
            h_scr[...] = _rms(a_ref[...], g_ref[...]).astype(bf16)
        a = h_scr[...]
    else:
        a = a_ref[...]
    acc = _dot(a, w_ref[...])
    if residual:
        acc = acc + r_ref[...]
    o_ref[...] = acc.astype(o_ref.dtype)


def _matmul(a, w, *, gain=None, residual=None, out_dtype, tm, tn, name):
    m, k = a.shape
    n = w.shape[1]
    norm = gain is not None
    ins = [a]
    specs = [pl.BlockSpec((tm, k), lambda i, j: (i, 0))]
    if norm:
        ins.append(gain.reshape(1, k))
        specs.append(pl.BlockSpec((1, k), lambda i, j: (0, 0)))
    ins.append(w)
    specs.append(pl.BlockSpec((k, tn), lambda i, j: (0, j)))
    if residual is not None:
        ins.append(residual)
        specs.append(pl.BlockSpec((tm, tn), lambda i, j: (i, j)))
    return pl.pallas_call(
        functools.partial(_mm_kernel, norm=norm, residual=residual is not None),
        grid=(m // tm, n // tn),
        in_specs=specs,
        out_specs=pl.BlockSpec((tm, tn), lambda i, j: (i, j)),
        out_shape=jax.ShapeDtypeStruct((m, n), out_dtype),
        scratch_shapes=[pltpu.VMEM((tm, k), bf16)] if norm else [],
        compiler_params=_params("parallel", "arbitrary"),
        name=name,
    )(*ins)


def _rmsnorm_kernel(x_ref, g_ref, *o_refs):
    x = x_ref[...]
    y = x * lax.rsqrt(jnp.mean(x * x, axis=-1, keepdims=True) + RMS_EPS)
    for n, o_ref in enumerate(o_refs):
        o_ref[...] = (y * g_ref[n:n + 1, :]).astype(o_ref.dtype)


def _rmsnorm_bf16(x, gains, *, tm, name):
    m, k = x.shape
    n = len(gains)
    row = lambda i: (i, 0)
    return pl.pallas_call(
        _rmsnorm_kernel,
        grid=(m // tm,),
        in_specs=[pl.BlockSpec((tm, k), row), pl.BlockSpec((n, k), lambda i: (0, 0))],
        out_specs=[pl.BlockSpec((tm, k), row)] * n,
        out_shape=[jax.ShapeDtypeStruct((m, k), bf16)] * n,
        compiler_params=_params("parallel"),
        name=name,
    )(x, jnp.stack(gains))


def _mm_ws_kernel(*refs, residual):
    it = iter(refs)
    a_ref = next(it)
    w_ref = next(it)
    r_ref = next(it) if residual else None
    o_ref = next(it)
    wb_scr = next(it)

    @pl.when(pl.program_id(1) == 0)
    def _():
        wb_scr[...] = w_ref[...].astype(bf16)

    acc = _dot(a_ref[...], wb_scr[...])
    if residual:
        acc = acc + r_ref[...]
    o_ref[...] = acc.astype(o_ref.dtype)


def _matmul_ws(a, w, layer, n_cols, *, residual=None, out_dtype, tm, tn, name):
    m, k = a.shape
    ins = [a, w]
    specs = [pl.BlockSpec((tm, k), lambda j, i: (i, 0)),
             pl.BlockSpec((None, k, tn), lambda j, i: (layer, 0, j), pipeline_mode=pl.Buffered(1))]
    if residual is not None:
        ins.append(residual)
        specs.append(pl.BlockSpec((tm, tn), lambda j, i: (i, j)))
    return pl.pallas_call(
        functools.partial(_mm_ws_kernel, residual=residual is not None),
        grid=(n_cols // tn, m // tm),
        in_specs=specs,
        out_specs=pl.BlockSpec((tm, tn), lambda j, i: (i, j)),
        out_shape=jax.ShapeDtypeStruct((m, n_cols), out_dtype),
        scratch_shapes=[pltpu.VMEM((k, tn), bf16)],
        compiler_params=_params("arbitrary", "arbitrary"),
        name=name,
    )(*ins)


def _proj_heads_kernel(a_ref, w_ref, hg_ref, nf_ref, o_ref, wb_scr, *, heads_per_tile, scale):
    @pl.when(pl.program_id(1) == 0)
    def _():
        wb_scr[...] = w_ref[...].astype(bf16)

    acc = _dot(a_ref[...], wb_scr[...])
    for hd in range(heads_per_tile):
        z = acc[:, hd * LANES:(hd + 1) * LANES]
        zn = _rms(z, hg_ref[hd]) * scale
        o_ref[hd] = jnp.where(nf_ref[hd] > 0.5, zn, z).astype(o_ref.dtype)


def _proj_heads(a, w, layer, n_cols, head_gain, norm_flag, *, scale, tm, name):
    m, k = a.shape
    hpt = WS_TN // LANES
    tn = WS_TN
    nh = n_cols // LANES
    return pl.pallas_call(
        functools.partial(_proj_heads_kernel, heads_per_tile=hpt, scale=scale),
        grid=(n_cols // tn, m // tm),
        in_specs=[
            pl.BlockSpec((tm, k), lambda j, i: (i, 0)),
            pl.BlockSpec((None, k, tn), lambda j, i: (layer, 0, j), pipeline_mode=pl.Buffered(1)),
            pl.BlockSpec((hpt, 1, LANES), lambda j, i: (j, 0, 0)),
            pl.BlockSpec((hpt, 1, LANES), lambda j, i: (j, 0, 0)),
        ],
        out_specs=pl.BlockSpec((hpt, tm, LANES), lambda j, i: (j, i, 0)),
        out_shape=jax.ShapeDtypeStruct((nh, m, LANES), bf16),
        scratch_shapes=[pltpu.VMEM((k, tn), bf16)],
        compiler_params=_params("arbitrary", "arbitrary"),
        name=name,
    )(a, w, head_gain, norm_flag)


def _mlstm_kernel(q_ref, k_ref, v_ref, og_ref, gt_ref, bif_ref, hg_ref, o_ref, c_scr, n_scr, m_scr):
    for sub in range(MLSTM_HEADS_PER_STEP):
        _mlstm_head(sub, q_ref, k_ref, v_ref, og_ref, gt_ref, bif_ref, hg_ref, o_ref, c_scr, n_scr, m_scr)


def _mlstm_head(sub, q_ref, k_ref, v_ref, og_ref, gt_ref, bif_ref, hg_ref, o_ref, c_scr, n_scr, m_scr):
    c = pl.program_id(0)
    h = pl.program_id(1) * MLSTM_HEADS_PER_STEP + sub
    L, dk, dv = A_CHUNK, A_DQK, A_DV
    qscale = dk ** -0.5

    @pl.when(c == 0)
    def _():
        c_scr[h] = jnp.zeros(c_scr.shape[1:], f32)
        n_scr[h] = jnp.zeros(n_scr.shape[1:], f32)
        m_scr[h] = jnp.zeros(m_scr.shape[1:], f32)

    q = q_ref[:, sub * dk:(sub + 1) * dk]
    k = k_ref[:, sub * dk:(sub + 1) * dk]
    v = v_ref[:, sub * dv:(sub + 1) * dv]
    li = gt_ref[pl.ds(h, 1), :] + bif_ref[pl.ds(h, 1), :]
    xf = gt_ref[pl.ds(h + A_HEADS, 1), :] + bif_ref[pl.ds(h + A_HEADS, 1), :]
    lf = jnp.minimum(xf, 0.0) - jnp.log1p(jnp.exp(-jnp.abs(xf)))

    ii = lax.broadcasted_iota(jnp.int32, (L, L), 0)
    jj = lax.broadcasted_iota(jnp.int32, (L, L), 1)
    upper = jnp.where(ii <= jj, 1.0, 0.0).astype(bf16)
    b_row = _dot_f32_exactrhs(jnp.broadcast_to(lf, (8, L)), upper)[0:1]
    eye = ii == jj

    def to_col(row):
        return jnp.sum(jnp.where(eye, row, 0.0), axis=1, keepdims=True)

    b_col = to_col(b_row)
    causal = jj <= ii
    dmat = b_col - b_row + li
    m_prev = m_scr[h]
    m_inter = b_col + m_prev
    m_t = jnp.maximum(m_inter, jnp.max(jnp.where(causal, dmat, -jnp.inf), axis=1, keepdims=True))
    dexp = jnp.where(causal, jnp.exp(dmat - m_t), 0.0)
    s = _nt(q, k) * qscale * dexp
    w_inter = jnp.exp(m_inter - m_t)
    ct = c_scr[h]
    n_row = n_scr[h]
    inter = _dot(q, ct.astype(bf16)) * qscale
    num = _dot(s.astype(bf16), v) + w_inter * inter
    qn = jnp.sum(q.astype(f32) * n_row, axis=1, keepdims=True) * qscale
    den = jnp.sum(s, axis=1, keepdims=True) + w_inter * qn
    hc = num / jnp.maximum(jnp.abs(den), jnp.exp(-m_t))

    b_end = b_row[:, L - 1:L]
    w_log = b_end - b_row + li
    m_new = jnp.maximum(b_end + m_prev, jnp.max(w_log, axis=1, keepdims=True))
    decay = jnp.exp(b_end + m_prev - m_new)
    w_wr = jnp.exp(w_log - m_new)
    vw = (v.astype(f32) * to_col(w_wr)).astype(bf16)
    kt = k.astype(f32).T.astype(bf16)
    c_scr[h] = decay * ct + _dot(kt, vw)
    n_scr[h] = decay * n_row + _dot(jnp.broadcast_to(w_wr, (8, L)).astype(bf16), k)[0:1]
    m_scr[h] = m_new

    hn = _rms(hc, hg_ref[sub])
    og = og_ref[:, sub * dv:(sub + 1) * dv].astype(f32)
    o_ref[:, sub * dv:(sub + 1) * dv] = (jax.nn.sigmoid(og) * hn).astype(o_ref.dtype)


def _mlstm(proj, gates_t, b_if, head_g):
    t_len = proj.shape[0]
    H, dk, dv, L = A_HEADS, A_DQK, A_DV, A_CHUNK
    hps = MLSTM_HEADS_PER_STEP
    wk, wv = hps * dk, hps * dv
    return pl.pallas_call(
        _mlstm_kernel,
        grid=(t_len // L, H // hps),
        in_specs=[
            pl.BlockSpec((L, wk), lambda c, h: (c, h)),
            pl.BlockSpec((L, wk), lambda c, h: (c, H * dk // wk + h)),
            pl.BlockSpec((L, wv), lambda c, h: (c, 2 * H * dk // wv + h)),
            pl.BlockSpec((L, wv), lambda c, h: (c, (2 * H * dk + H * dv) // wv + h)),
            pl.BlockSpec((2 * H, L), lambda c, h: (0, c)),
            pl.BlockSpec((2 * H, 1), lambda c, h: (0, 0)),
            pl.BlockSpec((hps, 1, dv), lambda c, h: (h, 0, 0)),
        ],
        out_specs=pl.BlockSpec((L, wv), lambda c, h: (c, h)),
        out_shape=jax.ShapeDtypeStruct((t_len, H * dv), bf16),
        scratch_shapes=[pltpu.VMEM((H, dk, dv), f32), pltpu.VMEM((H, 1, dk), f32), pltpu.VMEM((H, 1, 1), f32)],
        compiler_params=_params("arbitrary", "arbitrary"),
        name="mlstm",
    )(proj, proj, proj, proj, gates_t, b_if.reshape(2 * H, 1), head_g.reshape(H, 1, dv))


def _memattn_kernel(x_ref, g_ref, wq_ref, kv_ref, qg_ref, kg_ref, wo_ref, o_ref):
    x = x_ref[...]
    h = _rms(x, g_ref[...]).astype(bf16)
    qp = _dot(h, wq_ref[...])
    dh = MEM_HEAD_DIM
    nkv = MEM_HEADS * dh
    outs = []
    for hd in range(MEM_HEADS):
        qn = _rms(qp[:, hd * dh:(hd + 1) * dh], qg_ref[...]).astype(bf16)
        kn = _rms(kv_ref[:, hd * dh:(hd + 1) * dh], kg_ref[...]).astype(bf16)
        vh = kv_ref[:, nkv + hd * dh:nkv + (hd + 1) * dh].astype(bf16)
        s = _nt(qn, kn) * (dh ** -0.5)
        e = jnp.exp(s - jnp.max(s, axis=1, keepdims=True))
        p = e / jnp.sum(e, axis=1, keepdims=True)
        outs.append(_dot(p.astype(bf16), vh).astype(bf16))
    o = jnp.concatenate(outs, axis=1)
    o_ref[...] = x + _dot(o, wo_ref[...])


def _memattn(x, gain, wq, kv, qg, kg, wo, *, tm):
    t_len, d = x.shape
    md = MEM_HEADS * MEM_HEAD_DIM
    return pl.pallas_call(
        _memattn_kernel,
        grid=(t_len // tm,),
        in_specs=[
            pl.BlockSpec((tm, d), lambda i: (i, 0)),
            pl.BlockSpec((1, d), lambda i: (0, 0)),
            pl.BlockSpec((d, md), lambda i: (0, 0)),
            pl.BlockSpec((N_MEM, 2 * md), lambda i: (0, 0)),
            pl.BlockSpec((1, MEM_HEAD_DIM), lambda i: (0, 0)),
            pl.BlockSpec((1, MEM_HEAD_DIM), lambda i: (0, 0)),
            pl.BlockSpec((md, d), lambda i: (0, 0)),
        ],
        out_specs=pl.BlockSpec((tm, d), lambda i: (i, 0)),
        out_shape=jax.ShapeDtypeStruct((t_len, d), f32),
        compiler_params=_params("parallel"),
        name="memattn",
    )(x, gain.reshape(1, d), wq, kv, qg.reshape(1, -1), kg.reshape(1, -1), wo)


def _router_kernel(x_ref, g_ref, wr_ref, br_ref, h_ref, comb_ref, rank_ref, cnt_ref, carry_scr):
    tm = x_ref.shape[0]

    @pl.when(pl.program_id(0) == 0)
    def _():
        carry_scr[...] = jnp.zeros(carry_scr.shape, f32)

    hf = _rms(x_ref[...], g_ref[...])
    h_ref[...] = hf
    h1, h2, _ = _split3(hf)
    w1, w2, _ = _split3(wr_ref[...])
    logits = _dot(h2, w1) + _dot(h1, w2) + _dot(h1, w1) + br_ref[...]
    col = [logits[:, i:i + 1] for i in range(N_GROUPS + N_EXPERTS)]
    gl = col[:N_GROUPS]
    gm = functools.reduce(jnp.maximum, gl)
    ge = [jnp.exp(z - gm) for z in gl]
    gs = functools.reduce(jnp.add, ge)
    gp = [z / gs for z in ge]
    g_top = functools.reduce(jnp.maximum, gp)
    sel_g = []
    taken = jnp.zeros(g_top.shape, jnp.bool_)
    for z in gp:
        hit = (z == g_top) & jnp.logical_not(taken)
        sel_g.append(hit)
        taken = taken | hit
    E = EXPERTS_PER_GROUP
    el = []
    for e in range(E):
        acc = jnp.zeros_like(g_top)
        for g in range(N_GROUPS):
            acc = acc + jnp.where(sel_g[g], col[N_GROUPS + g * E + e], 0.0)
        el.append(acc)
    em = functools.reduce(jnp.maximum, el)
    ee = [jnp.exp(z - em) for z in el]
    es = functools.reduce(jnp.add, ee)
    ep = [z / es for z in ee]
    sel_e = []
    for e in range(E):
        rank = jnp.zeros(g_top.shape, jnp.int32)
        for e2 in range(E):
            if e2 == e:
                continue
            ahead = (ep[e2] > ep[e]) | ((ep[e2] == ep[e]) & (e2 < e))
            rank = rank + ahead.astype(jnp.int32)
        sel_e.append(rank < 2)
    top_sum = functools.reduce(jnp.add, [jnp.where(sel_e[e], ep[e], 0.0) for e in range(E)])
    lane = lax.broadcasted_iota(jnp.int32, comb_ref.shape, 1)
    comb = jnp.zeros(comb_ref.shape, f32)
    for g in range(N_GROUPS):
        for e in range(E):
            wge = jnp.where(sel_g[g] & sel_e[e], ep[e] / top_sum * g_top, 0.0)
            comb = jnp.where(lane == g * E + e, wge, comb)
    comb_ref[...] = comb

    member = jnp.where(comb > 0.0, 1.0, 0.0)
    r_i = lax.broadcasted_iota(jnp.int32, (tm, tm), 0)
    c_i = lax.broadcasted_iota(jnp.int32, (tm, tm), 1)
    earlier = jnp.where(c_i < r_i, 1.0, 0.0).astype(bf16)
    carry = carry_scr[...]
    rank_ref[...] = _dot(earlier, member.astype(bf16)) + carry
    carry = carry + jnp.sum(member, axis=0, keepdims=True)
    carry_scr[...] = carry
    cnt_ref[...] = carry


def _router(x, gain, w_r, b_r, *, tm):
    t_len, d = x.shape
    row = lambda i: (i, 0)
    return pl.pallas_call(
        _router_kernel,
        grid=(t_len // tm,),
        in_specs=[
            pl.BlockSpec((tm, d), row),
            pl.BlockSpec((1, d), lambda i: (0, 0)),
            pl.BlockSpec((d, LANES), lambda i: (0, 0)),
            pl.BlockSpec((1, LANES), lambda i: (0, 0)),
        ],
        out_specs=[pl.BlockSpec((tm, d), row), pl.BlockSpec((tm, LANES), row),
                   pl.BlockSpec((tm, LANES), row), pl.BlockSpec((1, LANES), lambda i: (0, 0))],
        out_shape=[jax.ShapeDtypeStruct((t_len, d), f32), jax.ShapeDtypeStruct((t_len, LANES), f32),
                   jax.ShapeDtypeStruct((t_len, LANES), f32), jax.ShapeDtypeStruct((1, LANES), f32)],
        scratch_shapes=[pltpu.VMEM((1, LANES), f32)],
        compiler_params=_params("arbitrary"),
        name="moe_router",
    )(x, gain.reshape(1, d), w_r, b_r)


def _row(ref, r):
    return ref.at[pl.ds(r, 1)]


def _meta_kernel(comb_ref, rank_ref, cnt_ref, pos_ref, w_ref, te_ref, nu_ref):
    comb = comb_ref[...]
    lane1 = lax.broadcasted_iota(jnp.int32, (1, LANES), 1)
    cnt = cnt_ref[...]
    gsz = jnp.ceil(cnt * (1.0 / MOE_TM)) * MOE_TM
    e_r = lax.broadcasted_iota(jnp.int32, (LANES, LANES), 0)
    e_c = lax.broadcasted_iota(jnp.int32, (LANES, LANES), 1)
    before = jnp.where(e_r < e_c, 1.0, 0.0).astype(bf16)
    goff = _dot_f32_exactrhs(jnp.broadcast_to(gsz, (8, LANES)), before)[0:1]
    member = comb > 0.0
    pos = jnp.where(member, goff + rank_ref[...], -1.0)
    pa = jnp.max(pos, axis=1, keepdims=True)
    is_a = member & (pos == pa)
    wa = jnp.sum(jnp.where(is_a, comb, 0.0), axis=1, keepdims=True)
    rest = member & jnp.logical_not(is_a)
    pb = jnp.max(jnp.where(rest, pos, -1.0), axis=1, keepdims=True)
    wb = jnp.sum(jnp.where(rest, comb, 0.0), axis=1, keepdims=True)
    lane = lax.broadcasted_iota(jnp.int32, comb.shape, 1)
    pos_ref[...] = jnp.where(lane == 0, pa, jnp.where(lane == 1, pb, -1.0)).astype(jnp.int32)
    w_ref[...] = jnp.where(lane == 0, wa, jnp.where(lane == 1, wb, 0.0))
    valid_e = lane1 < N_EXPERTS
    gend = goff + gsz
    tile_start = lax.broadcasted_iota(jnp.int32, (LANES, LANES), 0).astype(f32) * MOE_TM
    te = jnp.sum(jnp.where(valid_e & (gend <= tile_start), 1.0, 0.0), axis=1, keepdims=True)
    last = jnp.max(jnp.where(valid_e & (gsz > 0.0), lane1.astype(f32), 0.0), axis=1, keepdims=True)
    te_ref[...] = jnp.minimum(te, last).astype(jnp.int32)
    nu_ref[...] = (jnp.sum(jnp.where(valid_e, gsz, 0.0), axis=1, keepdims=True) * (1.0 / MOE_TM)).astype(jnp.int32)


def _moe_meta(comb, rank, cnt, *, tm):
    t_len = comb.shape[0]
    row = lambda i: (i, 0)
    fixed = lambda i: (0, 0)
    return pl.pallas_call(
        _meta_kernel,
        grid=(t_len // tm,),
        in_specs=[pl.BlockSpec((tm, LANES), row), pl.BlockSpec((tm, LANES), row), pl.BlockSpec((1, LANES), fixed)],
        out_specs=[pl.BlockSpec((tm, LANES), row), pl.BlockSpec((tm, LANES), row),
                   pl.BlockSpec((LANES, 1), fixed), pl.BlockSpec((1, 1), fixed)],
        out_shape=[jax.ShapeDtypeStruct((t_len, LANES), jnp.int32), jax.ShapeDtypeStruct((t_len, LANES), f32),
                   jax.ShapeDtypeStruct((LANES, 1), jnp.int32), jax.ShapeDtypeStruct((1, 1), jnp.int32)],
        compiler_params=_params("arbitrary"),
        name="moe_meta",
    )(comb, rank, cnt)


def _invmap_kernel(pos_ref, src_ref):
    n_tok = pos_ref.shape[0] // EXPERT_TOPK

    def fill(p, carry):
        src_ref[p] = p & (n_tok - 1)
        return carry

    lax.fori_loop(0, src_ref.shape[0], fill, 0, unroll=16)

    def body(k, carry):
        for slot in range(EXPERT_TOPK):
            p = pos_ref[EXPERT_TOPK * k + slot]

            @pl.when(p >= 0)
            def _():
                src_ref[p] = k
        return carry

    lax.fori_loop(0, pos_ref.shape[0] // EXPERT_TOPK, body, 0, unroll=8)


def _moe_invmap(pos_flat):
    return pl.pallas_call(
        _invmap_kernel,
        in_specs=[pl.BlockSpec(memory_space=pltpu.SMEM)],
        out_specs=pl.BlockSpec(memory_space=pltpu.SMEM),
        out_shape=jax.ShapeDtypeStruct((MOE_ROWS,), jnp.int32),
        name="moe_invmap",
    )(pos_flat)


def _gather_kernel(src_ref, nu_ref, h_ref, xs_ref, buf, sems):
    i = pl.program_id(0)
    n_used = nu_ref[0]

    def copy(tile, slot, k):
        return pltpu.make_async_copy(_row(h_ref, src_ref[tile * MOE_TM + k]), _row(buf.at[slot], k), sems.at[slot])

    def start_tile(tile, slot):
        def body(k, carry):
            copy(tile, slot, k).start()
            return carry

        lax.fori_loop(0, MOE_TM, body, 0, unroll=8)

    @pl.when((i == 0) & (n_used > 0))
    def _():
        start_tile(0, 0)

    for slot in range(2):
        @pl.when((i % 2 == slot) & (i < n_used))
        def _():
            @pl.when(i + 1 < n_used)
            def _():
                start_tile(i + 1, 1 - slot)

            def wait(k, carry):
                copy(i, slot, k).wait()
                return carry

            lax.fori_loop(0, MOE_TM, wait, 0, unroll=8)
            xs_ref[...] = buf[slot].astype(xs_ref.dtype)

    @pl.when(i >= n_used)
    def _():
        xs_ref[...] = jnp.zeros(xs_ref.shape, xs_ref.dtype)


def _moe_gather(src, nused, h):
    d = h.shape[1]
    return pl.pallas_call(
        _gather_kernel,
        grid_spec=pltpu.PrefetchScalarGridSpec(
            num_scalar_prefetch=2,
            grid=(MOE_TILES,),
            in_specs=[pl.BlockSpec(memory_space=pl.ANY)],
            out_specs=pl.BlockSpec((MOE_TM, d), lambda i, src, nu: (i, 0)),
            scratch_shapes=[pltpu.VMEM((2, MOE_TM, d), f32), pltpu.SemaphoreType.DMA((2,))],
        ),
        out_shape=jax.ShapeDtypeStruct((MOE_ROWS, d), bf16),
        compiler_params=pltpu.CompilerParams(dimension_semantics=("arbitrary",), vmem_limit_bytes=VMEM_LIMIT,
                                             disable_bounds_checks=True),
        name="moe_gather",
    )(src, nused, h)


def _expert_changed(te_ref, i):
    return (i == 0) | (te_ref[i] != te_ref[jnp.maximum(i - 1, 0)])


def _moe_hid_kernel(te_ref, nu_ref, xs_ref, w1_ref, w3_ref, o_ref, w1b, w3b):
    i = pl.program_id(1)

    @pl.when(_expert_changed(te_ref, i))
    def _():
        w1b[...] = w1_ref[...].astype(bf16)
        w3b[...] = w3_ref[...].astype(bf16)

    @pl.when(i < nu_ref[0])
    def _():
        x = xs_ref[...]
        u = _dot(x, w1b[...])
        g = _dot(x, w3b[...])
        o_ref[...] = (u * jax.nn.sigmoid(u) * g).astype(o_ref.dtype)

    @pl.when(i >= nu_ref[0])
    def _():
        o_ref[...] = jnp.zeros(o_ref.shape, o_ref.dtype)


def _moe_hid(te, nused, xs, w1, w3, layer, *, tn):
    _, ne, d, fdim = w1.shape
    used = lambda i, nu: jnp.where(i < nu[0], i, 0)
    return pl.pallas_call(
        _moe_hid_kernel,
        grid_spec=pltpu.PrefetchScalarGridSpec(
            num_scalar_prefetch=2,
            grid=(fdim // tn, MOE_TILES),
            in_specs=[
                pl.BlockSpec((MOE_TM, d), lambda j, i, te, nu: (used(i, nu), 0)),
                pl.BlockSpec((None, None, d, tn), lambda j, i, te, nu: (layer, te[i], 0, j)),
                pl.BlockSpec((None, None, d, tn), lambda j, i, te, nu: (layer, te[i], 0, j)),
            ],
            out_specs=pl.BlockSpec((MOE_TM, tn), lambda j, i, te, nu: (i, j)),
            scratch_shapes=[pltpu.VMEM((d, tn), bf16), pltpu.VMEM((d, tn), bf16)],
        ),
        out_shape=jax.ShapeDtypeStruct((MOE_ROWS, fdim), bf16),
        compiler_params=_params("arbitrary", "arbitrary"),
        name="moe_hid",
    )(te, nused, xs, w1, w3)


def _moe_w2_kernel(te_ref, nu_ref, hid_ref, w2_ref, o_ref, w2b):
    i = pl.program_id(0)

    @pl.when(_expert_changed(te_ref, i))
    def _():
        w2b[...] = w2_ref[...].astype(bf16)

    @pl.when(i < nu_ref[0])
    def _():
        o_ref[...] = _dot(hid_ref[...], w2b[...])

    @pl.when(i >= nu_ref[0])
    def _():
        o_ref[...] = jnp.zeros(o_ref.shape, o_ref.dtype)


def _moe_w2(te, nused, hid, w2, layer):
    _, ne, fdim, d = w2.shape
    used = lambda i, nu: jnp.where(i < nu[0], i, 0)
    return pl.pallas_call(
        _moe_w2_kernel,
        grid_spec=pltpu.PrefetchScalarGridSpec(
            num_scalar_prefetch=2,
            grid=(MOE_TILES,),
            in_specs=[
                pl.BlockSpec((MOE_TM, fdim), lambda i, te, nu: (used(i, nu), 0)),
                pl.BlockSpec((None, None, fdim, d), lambda i, te, nu: (layer, te[i], 0, 0),
                             pipeline_mode=pl.Buffered(1)),
            ],
            out_specs=pl.BlockSpec((MOE_TM, d), lambda i, te, nu: (i, 0)),
            scratch_shapes=[pltpu.VMEM((fdim, d), bf16)],
        ),
        out_shape=jax.ShapeDtypeStruct((MOE_ROWS, d), f32),
        compiler_params=_params("arbitrary"),
        name="moe_w2",
    )(te, nused, hid, w2)


def _moe_combine_kernel(pos_ref, x_ref, w_ref, ys_ref, o_ref, buf, sems):
    tm, d = x_ref.shape
    i = pl.program_id(0)

    def copies(tile, slot, k):
        out = []
        for which in range(EXPERT_TOPK):
            p = jnp.maximum(pos_ref[EXPERT_TOPK * (tile * tm + k) + which], 0)
            out.append(pltpu.make_async_copy(_row(ys_ref, p), _row(buf.at[slot, which], k), sems.at[slot]))
        return out

    def start_tile(tile, slot):
        def body(k, carry):
            for cp in copies(tile, slot, k):
                cp.start()
            return carry

        lax.fori_loop(0, tm, body, 0, unroll=8)

    @pl.when(i == 0)
    def _():
        start_tile(0, 0)

    for slot in range(2):
        @pl.when(i % 2 == slot)
        def _():
            @pl.when(i + 1 < pl.num_programs(0))
            def _():
                start_tile(i + 1, 1 - slot)

            def wait(k, carry):
                for cp in copies(i, slot, k):
                    cp.wait()
                return carry

            lax.fori_loop(0, tm, wait, 0, unroll=8)
            o_ref[...] = x_ref[...] + w_ref[:, 0:1] * buf[slot, 0] + w_ref[:, 1:2] * buf[slot, 1]


def _moe_combine(pos_flat, x, tokw, ys, *, tm):
    t_len, d = x.shape
    return pl.pallas_call(
        _moe_combine_kernel,
        grid_spec=pltpu.PrefetchScalarGridSpec(
            num_scalar_prefetch=1,
            grid=(t_len // tm,),
            in_specs=[
                pl.BlockSpec((tm, d), lambda i, pos: (i, 0)),
                pl.BlockSpec((tm, LANES), lambda i, pos: (i, 0)),
                pl.BlockSpec(memory_space=pl.ANY),
            ],
            out_specs=pl.BlockSpec((tm, d), lambda i, pos: (i, 0)),
            scratch_shapes=[pltpu.VMEM((2, EXPERT_TOPK, tm, d), f32), pltpu.SemaphoreType.DMA((2,))],
        ),
        out_shape=jax.ShapeDtypeStruct((t_len, d), f32),
        compiler_params=pltpu.CompilerParams(dimension_semantics=("arbitrary",), vmem_limit_bytes=VMEM_LIMIT,
                                             disable_bounds_checks=True),
        name="moe_combine",
    )(pos_flat, x, tokw, ys)


def _hier_moe(x, gain, wg, bg, we, be, w1, w3, w2, layer):
    d = x.shape[1]
    w_r = jnp.concatenate([wg, we.transpose(1, 0, 2).reshape(d, N_EXPERTS)], axis=1)
    w_r = jnp.pad(w_r, ((0, 0), (0, LANES - w_r.shape[1])))
    b_r = jnp.pad(jnp.concatenate([bg, be.reshape(-1)]), (0, LANES - N_GROUPS - N_EXPERTS)).reshape(1, LANES)
    h, comb, rank, cnt = _router(x, gain, w_r, b_r, tm=256)
    tokpos, tokw, te, nused = _moe_meta(comb, rank, cnt, tm=512)
    pos_flat = tokpos[:, :EXPERT_TOPK].reshape(-1)
    te = te.reshape(-1)
    nused = nused.reshape(-1)
    src = _moe_invmap(pos_flat)
    xs = _moe_gather(src, nused, h)
    hid = _moe_hid(te, nused, xs, w1, w3, layer, tn=512)
    ys = _moe_w2(te, nused, hid, w2, layer)
    return _moe_combine(pos_flat, x, tokw, ys, tm=256)


def _compress_kernel(z_ref, pe_ref, w1_ref, w2_ref, kg_ref, o_ref):
    half = CMP_STRIDE * B_HEAD_DIM
    z = z_ref[...].astype(f32)
    zt = (z + pe_ref[:, :half]).astype(bf16)
    zb = (z + pe_ref[:, half:]).astype(bf16)
    top = _dot(zt, w1_ref[:half, :])
    bot = _dot(zb, w1_ref[half:, :])
    pre = top + pltpu.roll(bot, N_CMP_PAD - 1, axis=0)
    act = pre * (0.5 * (1.0 + jnp.tanh(np.sqrt(2.0 / np.pi).astype(np.float32) * (pre + 0.044715 * (pre * pre * pre)))))
    out = _dot(act.astype(bf16), w2_ref[...])
    normed = _rms(out, kg_ref[...])
    o_ref[...] = jnp.where(pl.program_id(0) == 0, normed, out).astype(o_ref.dtype)


def _compress(zr, pe_flat, w1, w2, kg):
    G = B_KV_HEADS
    return pl.pallas_call(
        _compress_kernel,
        grid=(2, G),
        in_specs=[
            pl.BlockSpec((None, None, N_CMP_PAD, CMP_STRIDE * B_HEAD_DIM), lambda w, g: (w, g, 0, 0)),
            pl.BlockSpec((None, 1, CMP_BLOCK * B_HEAD_DIM), lambda w, g: (w, 0, 0)),
            pl.BlockSpec((None, CMP_BLOCK * B_HEAD_DIM, CMP_HIDDEN), lambda w, g: (w, 0, 0)),
            pl.BlockSpec((None, CMP_HIDDEN, B_HEAD_DIM), lambda w, g: (w, 0, 0)),
            pl.BlockSpec((1, B_HEAD_DIM), lambda w, g: (0, 0)),
        ],
        out_specs=pl.BlockSpec((None, None, N_CMP_PAD, B_HEAD_DIM), lambda w, g: (w, g, 0, 0)),
        out_shape=jax.ShapeDtypeStruct((2, G, N_CMP_PAD, B_HEAD_DIM), bf16),
        compiler_params=_params("arbitrary", "arbitrary"),
        name="nsa_compress",
    )(zr, pe_flat, w1, w2, kg)


def _key_extra(pos):
    lane = lax.broadcasted_iota(jnp.int32, pos.shape, 1)
    lo = pos & (LANES - 1)
    return jnp.where(lane < 3, lo.astype(f32), jnp.where(lane < 6, (pos - lo).astype(f32), 0.0)).astype(bf16)


def _nsa_kernel(q_ref, gate_ref, sl_ref, kc_ref, vct_ref, ks_ref, vst_ref, kw_ref, vwt_ref, o_ref,
                qa_scr, selb_scr, m_scr, acc_scr, out_scr):
    c = pl.program_id(1)
    R, QB, dh = B_GROUP, Q_BLOCK, B_HEAD_DIM
    rows = R * QB
    t0 = c * QB
    slabs = [slice(r * QB, (r + 1) * QB) for r in range(R)]

    qa_scr[0:dh, :] = jnp.concatenate([q_ref[r].astype(f32).T for r in range(R)], axis=1).astype(bf16)
    s1, s2, s3 = [p.astype(f32) for p in _split3(sl_ref[...] * LOG2E)]
    rid = lax.broadcasted_iota(jnp.int32, (dh, rows), 0)
    qa_scr[dh:2 * dh, :] = jnp.where((rid == 0) | (rid == 3), s1,
                                     jnp.where((rid == 1) | (rid == 4), s2,
                                               jnp.where((rid == 2) | (rid == 5), s3, 0.0))).astype(bf16)

    kidx = lax.broadcasted_iota(jnp.int32, (QB, QB), 0)
    tidx = lax.broadcasted_iota(jnp.int32, (QB, QB), 1)
    causal = kidx <= tidx

    gate_t = jax.nn.sigmoid(gate_ref[...]).T

    def scores(k_rows, pos):
        return _dot(jnp.concatenate([k_rows, _key_extra(pos)], axis=1), qa_scr[...])

    def first_chunk(s_t, ok, v_t):
        ps = []
        for r in range(R):
            s_r = jnp.where(ok, s_t[:, slabs[r]], NEG_INF)
            m = jnp.max(s_r, axis=0, keepdims=True)
            m_scr[:, slabs[r]] = m
            ps.append(jnp.where(ok, jnp.exp2(s_r - m), 0.0).astype(bf16))
        acc_scr[...] = _dot(v_t, jnp.concatenate(ps, axis=1))

    def next_chunk(s_t, mask_bias, v_t):
        ps, alphas = [], []
        for r in range(R):
            s_r = s_t[:, slabs[r]] + mask_bias
            m_old = m_scr[:, slabs[r]]
            m_new = jnp.maximum(m_old, jnp.max(s_r, axis=0, keepdims=True))
            m_scr[:, slabs[r]] = m_new
            alphas.append(jnp.exp2(m_old - m_new))
            ps.append(jnp.exp2(s_r - m_new).astype(bf16))
        pv = _dot(v_t, jnp.concatenate(ps, axis=1))
        for r in range(R):
            acc_scr[:, slabs[r]] = alphas[r] * acc_scr[:, slabs[r]] + pv[:, slabs[r]]

    def emit(branch):
        for r in range(R):
            w = gate_t[branch * R + r:branch * R + r + 1, :] / acc_scr[dh:dh + 1, slabs[r]]
            out_scr[:, slabs[r]] = out_scr[:, slabs[r]] + acc_scr[0:dh, slabs[r]] * w

    cend = lax.broadcasted_iota(jnp.int32, (N_CMP_PAD, QB), 0) * CMP_STRIDE + (CMP_BLOCK - 1)
    ok_c = cend <= t0 + lax.broadcasted_iota(jnp.int32, (N_CMP_PAD, QB), 1)
    s_c = scores(kc_ref[...], cend)
    p_sum = jnp.zeros((N_CMP_PAD, QB), f32)
    ps = []
    for r in range(R):
        s_r = jnp.where(ok_c, s_c[:, slabs[r]], NEG_INF)
        e = jnp.where(ok_c, jnp.exp2(s_r - jnp.max(s_r, axis=0, keepdims=True)), 0.0)
        l = jnp.sum(e, axis=0, keepdims=True)
        p = e / jnp.where(l > 0.0, l, 1.0)
        p_sum = p_sum + p
        ps.append(p.astype(bf16))
    oc_t = _dot(vct_ref[...], jnp.concatenate(ps, axis=1))
    for r in range(R):
        out_scr[:, slabs[r]] = oc_t[:, slabs[r]] * gate_t[r:r + 1, :]

    b_i = lax.broadcasted_iota(jnp.int32, (N_SEL, N_CMP_PAD), 0) * SEL_BLOCK
    n_i = lax.broadcasted_iota(jnp.int32, (N_SEL, N_CMP_PAD), 1) * CMP_STRIDE
    overlap_t = jnp.where((n_i < b_i + SEL_BLOCK) & (n_i + (CMP_BLOCK - 1) >= b_i), 1.0, 0.0).astype(bf16)
    p1, p2, p3 = _split3(p_sum)
    imp = _dot(overlap_t, p3) + _dot(overlap_t, p2) + _dot(overlap_t, p1)
    tq = t0 + tidx
    cur = tq // SEL_BLOCK
    forced = (kidx == 0) | (kidx == cur) | (kidx == cur - 1)
    imp = jnp.where(forced, SEL_FORCE, imp)
    imp = jnp.where(kidx * SEL_BLOCK <= tq, imp, -1.0)
    blk_f = kidx.astype(f32)
    sel = jnp.zeros((N_SEL, QB), f32)
    for _ in range(SEL_TOPK):
        mx = jnp.max(imp, axis=0, keepdims=True)
        first = jnp.min(jnp.where(imp == mx, blk_f, float(N_SEL)), axis=0, keepdims=True)
        pick = blk_f == first
        sel = jnp.where(pick & (mx >= 0.0), 1.0, sel)
        imp = jnp.where(pick, -jnp.inf, imp)
    selb_scr[...] = jnp.where(sel > 0.5, 0.0, NEG_INF)

    def sel_bias(first_block, n_blocks):
        return jnp.concatenate([jnp.broadcast_to(selb_scr[pl.ds(first_block + b, 1), :], (SEL_BLOCK, QB))
                                for b in range(n_blocks)], axis=0)

    def chunk_pos(i):
        return i * QB + kidx

    off_c = pl.multiple_of(t0, QB)
    first_chunk(scores(ks_ref[pl.ds(off_c, QB), :], chunk_pos(c)), causal & (sel_bias(2 * c, 2) > -1.0),
                vst_ref[c])
    wide = SEL_WIDE_CHUNKS
    pos_w = lax.broadcasted_iota(jnp.int32, (wide * QB, QB), 0)
    blocks_per_chunk = QB // SEL_BLOCK

    def sel_wide(j, carry):
        off = pl.multiple_of(j * wide * QB, wide * QB)
        v_t = jnp.concatenate([vst_ref[wide * j + n] for n in range(wide)], axis=1)
        next_chunk(scores(ks_ref[pl.ds(off, wide * QB), :], off + pos_w),
                   sel_bias(wide * blocks_per_chunk * j, wide * blocks_per_chunk), v_t)
        return carry

    def sel_narrow(i, carry):
        off = pl.multiple_of(i * QB, QB)
        next_chunk(scores(ks_ref[pl.ds(off, QB), :], chunk_pos(i)), sel_bias(blocks_per_chunk * i, blocks_per_chunk),
                   vst_ref[i])
        return carry

    lax.fori_loop(0, c // wide, sel_wide, 0)
    lax.fori_loop((c // wide) * wide, c, sel_narrow, 0)
    emit(1)

    first_chunk(scores(kw_ref[pl.ds(off_c, QB), :], chunk_pos(c)), causal, vwt_ref[c])
    n_back = WINDOW // QB

    @pl.when(c >= n_back)
    def _():
        first = c - n_back
        off = pl.multiple_of(first * QB, QB)
        pos = off + lax.broadcasted_iota(jnp.int32, (n_back * QB, QB), 0)
        bias = jnp.concatenate([jnp.where(kidx > tidx, 0.0, NEG_INF), jnp.zeros(((n_back - 1) * QB, QB), f32)], axis=0)
        v_t = jnp.concatenate([vwt_ref[first + n] for n in range(n_back)], axis=1)
        next_chunk(scores(kw_ref[pl.ds(off, n_back * QB), :], pos), bias, v_t)

    @pl.when(c < n_back)
    def _():
        def win_body(i, carry):
            off = pl.multiple_of(i * QB, QB)
            next_chunk(scores(kw_ref[pl.ds(off, QB), :], chunk_pos(i)), jnp.zeros((QB, QB), f32), vwt_ref[i])
            return carry

        lax.fori_loop(0, c, win_body, 0)

    emit(2)

    for r in range(R):
        o_ref[:, r * dh:(r + 1) * dh] = out_scr[:, slabs[r]].T.astype(o_ref.dtype)


def _nsa_attention(qh, gates, slopes, cmp_kv, kvh):
    t_len = qh.shape[1]
    G, R, QB, dh = B_KV_HEADS, B_GROUP, Q_BLOCK, B_HEAD_DIM
    rows = R * QB
    nq = t_len // QB
    kc = cmp_kv[0]
    vc_t = cmp_kv[1].swapaxes(1, 2)
    ones_rows = jnp.zeros((G, nq, BF16_SUBLANES, QB), bf16).at[:, :, 0, :].set(1.0)

    def values_t(v):
        return jnp.concatenate([v.reshape(G, nq, QB, dh).swapaxes(2, 3), ones_rows], axis=2)

    vs_t = values_t(kvh[3 * G:4 * G])
    vw_t = values_t(kvh[5 * G:6 * G])
    dha = dh + BF16_SUBLANES

    def k_spec(base):
        return pl.BlockSpec((None, t_len, dh), lambda g, c: (base + g, 0, 0))

    vt_spec = pl.BlockSpec((None, nq, dha, QB), lambda g, c: (g, 0, 0, 0))
    return pl.pallas_call(
        _nsa_kernel,
        grid=(G, nq),
        in_specs=[
            pl.BlockSpec((R, QB, dh), lambda g, c: (g, c, 0)),
            pl.BlockSpec((QB, LANES), lambda g, c: (c, g)),
            pl.BlockSpec((None, 1, rows), lambda g, c: (g, 0, 0)),
            pl.BlockSpec((None, N_CMP_PAD, dh), lambda g, c: (g, 0, 0)),
            pl.BlockSpec((None, dh, N_CMP_PAD), lambda g, c: (g, 0, 0)),
            k_spec(2 * G), vt_spec, k_spec(4 * G), vt_spec,
        ],
        out_specs=pl.BlockSpec((QB, R * dh), lambda g, c: (c, g)),
        out_shape=jax.ShapeDtypeStruct((t_len, G * R * dh), bf16),
        scratch_shapes=[
            pltpu.VMEM((2 * dh, rows), bf16),
            pltpu.VMEM((N_SEL, QB), f32),
            pltpu.VMEM((1, rows), f32),
            pltpu.VMEM((dha, rows), f32),
            pltpu.VMEM((dh, rows), f32),
        ],
        compiler_params=_params("parallel", "arbitrary"),
        name="nsa_attention",
    )(qh, gates, slopes, kc, vc_t, kvh, vs_t, kvh, vw_t)


def _mlstm_layer(x, gain, w_in, layer, b_if, head_g, w_out):
    H, dk, dv = A_HEADS, A_DQK, A_DV
    n_main = 2 * H * dk + 2 * H * dv
    w_gate = jnp.pad(w_in[layer, :, n_main:], ((0, 0), (0, LANES - 2 * H))).astype(bf16)
    h, = _rmsnorm_bf16(x, [gain], tm=512, name="mlstm_norm")
    proj = _matmul_ws(h, w_in, layer, n_main, out_dtype=bf16, tm=WS_TM, tn=WS_TN, name="mlstm_in")
    gates = _matmul(h, w_gate, out_dtype=f32, tm=1024, tn=LANES, name="mlstm_gates")
    gates_t = gates[:, :2 * H].T
    o = _mlstm(proj, gates_t, b_if, head_g)
    return _matmul_ws(o, w_out, layer, w_out.shape[2], residual=x, out_dtype=f32, tm=WS_TM, tn=WS_TN,
                      name="mlstm_out")


def _memattn_layer(x, gain, mem, mem_g, wq, wk, wv, wo, q_g, k_g):
    wkv = jnp.concatenate([wk, wv], axis=1).astype(bf16)
    kv = _matmul(mem, wkv, gain=mem_g, out_dtype=f32, tm=N_MEM, tn=512, name="mem_kv")
    return _memattn(x, gain, wq.astype(bf16), kv, q_g, k_g, wo.astype(bf16), tm=256)


def _alibi_slopes(n):
    return np.asarray([2.0 ** (-8.0 * (h + 1) / n) for h in range(n)], dtype=np.float32)


def _nsa_shared_kv(x, kv_norm_g, kv_w, cmp_pe, cmp_w1, cmp_w2, k_norm_g):
    G, dh = B_KV_HEADS, B_HEAD_DIM
    t_len = x.shape[0]
    ones = jnp.ones((G, 1, dh), f32)
    zeros = jnp.zeros((G, 1, dh), f32)
    head_gain = jnp.concatenate([ones, ones, ones * k_norm_g[1], ones, ones * k_norm_g[2], ones], axis=0)
    norm_flag = jnp.concatenate([zeros, zeros, ones, zeros, ones, zeros], axis=0)
    h, = _rmsnorm_bf16(x, [kv_norm_g], tm=512, name="nsa_kv_norm")
    kvh = _proj_heads(h, kv_w[None], 0, kv_w.shape[1], head_gain, norm_flag, scale=1.0, tm=WS_TM, name="nsa_kv_proj")
    zr = kvh[:2 * G].reshape(2, G, t_len // CMP_STRIDE, CMP_STRIDE * dh)
    cmp_kv = _compress(zr, cmp_pe.reshape(2, 1, CMP_BLOCK * dh), cmp_w1.astype(bf16), cmp_w2.astype(bf16),
                       k_norm_g[0].reshape(1, dh))
    return cmp_kv, kvh


def _nsa_layer(x, gain, shared, w_in, layer, q_norm_g, w_out):
    cmp_kv, kvh = shared
    H, G, R, dh = B_HEADS, B_KV_HEADS, B_GROUP, B_HEAD_DIM
    ones = jnp.ones((H, 1, dh), f32)
    h, = _rmsnorm_bf16(x, [gain], tm=512, name="nsa_norm")
    qh = _proj_heads(h, w_in, layer, H * dh, ones * q_norm_g, ones, scale=dh ** -0.5 * LOG2E, tm=WS_TM,
                     name="nsa_q_proj")
    wg = w_in[layer, :, H * dh:].reshape(-1, 3, G, R).transpose(0, 2, 1, 3).reshape(-1, G, 3 * R)
    wg = jnp.pad(wg, ((0, 0), (0, 0), (0, LANES - 3 * R))).reshape(-1, G * LANES).astype(bf16)
    gates = _matmul(h, wg, out_dtype=f32, tm=1024, tn=512, name="nsa_gates")
    slopes = jnp.asarray(np.repeat(_alibi_slopes(H).reshape(G, R, 1), Q_BLOCK, axis=2).reshape(G, 1, R * Q_BLOCK))
    o = _nsa_attention(qh, gates, slopes, cmp_kv, kvh)
    return _matmul_ws(o, w_out, layer, w_out.shape[2], residual=x, out_dtype=f32, tm=WS_TM, tn=WS_TN, name="nsa_out")


def kernel(x, mem, norm_g, a_w_in, a_b_if, a_head_g, a_w_out, kv_norm_g, kv_w, cmp_pe, cmp_w1, cmp_w2, k_norm_g,
           b_w_in, b_q_norm_g, b_w_out, mem_norm_g, mem_wq, mem_wk, mem_wv, mem_wo, mem_q_g, mem_k_g, moe_wg,
           moe_bg, moe_we, moe_be, moe_w1, moe_w3, moe_w2):
    bsz, t_len, d = x.shape
    assert (bsz, t_len, d) == (1, SEQ, D_MODEL)
    xs = x.reshape(t_len, d)
    ms = mem.reshape(N_MEM, d)
    depth = norm_g.shape[0]
    n_a = depth - depth // 2
    shared = None
    for layer in range(depth):
        if layer < n_a:
            xs = _mlstm_layer(xs, norm_g[layer, 0], a_w_in, layer, a_b_if[layer], a_head_g[layer], a_w_out)
        else:
            if layer == n_a:
                shared = _nsa_shared_kv(xs, kv_norm_g, kv_w, cmp_pe, cmp_w1, cmp_w2, k_norm_g)
            j = layer - n_a
            xs = _nsa_layer(xs, norm_g[layer, 0], shared, b_w_in, j, b_q_norm_g[j], b_w_out)
        xs = _memattn_layer(xs, norm_g[layer, 1], ms, mem_norm_g[layer], mem_wq[layer], mem_wk[layer],
                            mem_wv[layer], mem_wo[layer], mem_q_g[layer], mem_k_g[layer])
        xs = _hier_moe(xs, norm_g[layer, 2], moe_wg[layer], moe_bg[layer], moe_we[layer], moe_be[layer],
                       moe_w1, moe_w3, moe_w2, layer)
    return xs.reshape(bsz, t_len, d)
```

```python
import functools

import numpy as np
import jax
import jax.numpy as jnp
from jax import lax
from jax.experimental import pallas as pl
from jax.experimental.pallas import tpu as pltpu

f32 = jnp.float32
bf16 = jnp.bfloat16

V7X_VMEM_BYTES = 64 * 1024 * 1024
VMEM_LIMIT = V7X_VMEM_BYTES - 8 * 1024 * 1024
LANES = 128
BF16_SUBLANES = 16
LOG2E = 1.4426950408889634

D_MODEL = 4096
SEQ = 8192
N_MEM = 256
RMS_EPS = 1e-6
NEG_INF = -1e30
A_HEADS = 8
A_DQK = 256
A_DV = 512
A_CHUNK = 128
B_HEADS = 32
B_KV_HEADS = 4
B_HEAD_DIM = 128
B_GROUP = 8
CMP_BLOCK = 32
CMP_STRIDE = 16
CMP_HIDDEN = 256
SEL_BLOCK = 64
SEL_TOPK = 16
WINDOW = 512
Q_BLOCK = 128
SEL_FORCE = 1e6
MEM_HEADS = 4
MEM_HEAD_DIM = 128
N_GROUPS = 4
EXPERTS_PER_GROUP = 4
N_EXPERTS = 16
D_EXPERT = 1024
EXPERT_TOPK = 2
N_CMP_PAD = SEQ // CMP_STRIDE
N_SEL = SEQ // SEL_BLOCK
MOE_TM = 256
MOE_ROWS = EXPERT_TOPK * SEQ + N_EXPERTS * MOE_TM
MOE_TILES = MOE_ROWS // MOE_TM
SEL_WIDE_CHUNKS = 8
MLSTM_HEADS_PER_STEP = 2
NSA_HEADS_PER_GROUP = 8
assert SEQ & (SEQ - 1) == 0
WS_TM, WS_TN = 512, 1024


def _params(*sem):
    return pltpu.CompilerParams(dimension_semantics=sem, vmem_limit_bytes=VMEM_LIMIT)


def _rms(x, g):
    ms = jnp.mean(x * x, axis=-1, keepdims=True)
    return x * lax.rsqrt(ms + RMS_EPS) * g


def _nt(a, b):
    return lax.dot_general(a, b, (((1,), (1,)), ((), ())), preferred_element_type=f32)


def _dot(a, b):
    return jnp.dot(a, b, preferred_element_type=f32)


def _split3(a):
    a1 = a.astype(bf16)
    r1 = a - a1.astype(f32)
    a2 = r1.astype(bf16)
    a3 = (r1 - a2.astype(f32)).astype(bf16)
    return a1, a2, a3


def _dot_f32_exactrhs(a, b_bf16):
    a1, a2, a3 = _split3(a)
    return _dot(a3, b_bf16) + _dot(a2, b_bf16) + _dot(a1, b_bf16)


def _mm_kernel(*refs, norm, residual):
    it = iter(refs)
    a_ref = next(it)
    g_ref = next(it) if norm else None
    w_ref = next(it)
    r_ref = next(it) if residual else None
    o_ref = next(it)
    h_scr = next(it) if norm else None
    if norm:
        @pl.when(pl.program_id(1) == 0)
        def _():
            h_scr[...] = _rms(a_ref[...], g_ref[...]).astype(bf16)
        a = h_scr[...]
    else:
        a = a_ref[...]
    acc = _dot(a, w_ref[...])
    if residual:
        acc = acc + r_ref[...]
    o_ref[...] = acc.astype(o_ref.dtype)


def _matmul(a, w, *, gain=None, residual=None, out_dtype, tm, tn, name):
    m, k = a.shape
    n = w.shape[1]
    norm = gain is not None
    ins = [a]
    specs = [pl.BlockSpec((tm, k), lambda i, j: (i, 0))]
    if norm:
        ins.append(gain.reshape(1, k))
        specs.append(pl.BlockSpec((1, k), lambda i, j: (0, 0)))
    ins.append(w)
    specs.append(pl.BlockSpec((k, tn), lambda i, j: (0, j)))
    if residual is not None:
        ins.append(residual)
        specs.append(pl.BlockSpec((tm, tn), lambda i, j: (i, j)))
    return pl.pallas_call(
        functools.partial(_mm_kernel, norm=norm, residual=residual is not None),
        grid=(m // tm, n // tn),
        in_specs=specs,
        out_specs=pl.BlockSpec((tm, tn), lambda i, j: (i, j)),
        out_shape=jax.ShapeDtypeStruct((m, n), out_dtype),
        scratch_shapes=[pltpu.VMEM((tm, k), bf16)] if norm else [],
        compiler_params=_params("parallel", "arbitrary"),
        name=name,
    )(*ins)


def _rmsnorm_kernel(x_ref, g_ref, *o_refs):
    x = x_ref[...]
    y = x * lax.rsqrt(jnp.mean(x * x, axis=-1, keepdims=True) + RMS_EPS)
    for n, o_ref in enumerate(o_refs):
        o_ref[...] = (y * g_ref[n:n + 1, :]).astype(o_ref.dtype)


def _rmsnorm_bf16(x, gains, *, tm, name):
    m, k = x.shape
    n = len(gains)
    row = lambda i: (i, 0)
    return pl.pallas_call(
        _rmsnorm_kernel,
        grid=(m // tm,),
        in_specs=[pl.BlockSpec((tm, k), row), pl.BlockSpec((n, k), lambda i: (0, 0))],
        out_specs=[pl.BlockSpec((tm, k), row)] * n,
        out_shape=[jax.ShapeDtypeStruct((m, k), bf16)] * n,
        compiler_params=_params("parallel"),
        name=name,
    )(x, jnp.stack(gains))


def _mm_ws_kernel(*refs, residual):
    it = iter(refs)
    a_ref = next(it)
    w_ref = next(it)
    r_ref = next(it) if residual else None
    o_ref = next(it)
    wb_scr = next(it)

    @pl.when(pl.program_id(1) == 0)
    def _():
        wb_scr[...] = w_ref[...].astype(bf16)

    acc = _dot(a_ref[...], wb_scr[...])
    if residual:
        acc = acc + r_ref[...]
    o_ref[...] = acc.astype(o_ref.dtype)


def _matmul_ws(a, w, layer, n_cols, *, residual=None, out_dtype, tm, tn, name):
    m, k = a.shape
    ins = [a, w]
    specs = [pl.BlockSpec((tm, k), lambda j, i: (i, 0)),
             pl.BlockSpec((None, k, tn), lambda j, i: (layer, 0, j), pipeline_mode=pl.Buffered(1))]
    if residual is not None:
        ins.append(residual)
        specs.append(pl.BlockSpec((tm, tn), lambda j, i: (i, j)))
    return pl.pallas_call(
        functools.partial(_mm_ws_kernel, residual=residual is not None),
        grid=(n_cols // tn, m // tm),
        in_specs=specs,
        out_specs=pl.BlockSpec((tm, tn), lambda j, i: (i, j)),
        out_shape=jax.ShapeDtypeStruct((m, n_cols), out_dtype),
        scratch_shapes=[pltpu.VMEM((k, tn), bf16)],
        compiler_params=_params("arbitrary", "arbitrary"),
        name=name,
    )(*ins)


def _proj_heads_kernel(a_ref, w_ref, hg_ref, nf_ref, o_ref, wb_scr, *, heads_per_tile, scale):
    @pl.when(pl.program_id(1) == 0)
    def _():
        wb_scr[...] = w_ref[...].astype(bf16)

    acc = _dot(a_ref[...], wb_scr[...])
    for hd in range(heads_per_tile):
        z = acc[:, hd * LANES:(hd + 1) * LANES]
        zn = _rms(z, hg_ref[hd]) * scale
        o_ref[hd] = jnp.where(nf_ref[hd] > 0.5, zn, z).astype(o_ref.dtype)


def _proj_heads(a, w, layer, n_cols, head_gain, norm_flag, *, scale, tm, name):
    m, k = a.shape
    hpt = WS_TN // LANES
    tn = WS_TN
    nh = n_cols // LANES
    return pl.pallas_call(
        functools.partial(_proj_heads_kernel, heads_per_tile=hpt, scale=scale),
        grid=(n_cols // tn, m // tm),
        in_specs=[
            pl.BlockSpec((tm, k), lambda j, i: (i, 0)),
            pl.BlockSpec((None, k, tn), lambda j, i: (layer, 0, j), pipeline_mode=pl.Buffered(1)),
            pl.BlockSpec((hpt, 1, LANES), lambda j, i: (j, 0, 0)),
            pl.BlockSpec((hpt, 1, LANES), lambda j, i: (j, 0, 0)),
        ],
        out_specs=pl.BlockSpec((hpt, tm, LANES), lambda j, i: (j, i, 0)),
        out_shape=jax.ShapeDtypeStruct((nh, m, LANES), bf16),
        scratch_shapes=[pltpu.VMEM((k, tn), bf16)],
        compiler_params=_params("arbitrary", "arbitrary"),
        name=name,
    )(a, w, head_gain, norm_flag)


def _mlstm_kernel(q_ref, k_ref, v_ref, og_ref, gt_ref, bif_ref, hg_ref, o_ref, c_scr, n_scr, m_scr):
    for sub in range(MLSTM_HEADS_PER_STEP):
        _mlstm_head(sub, q_ref, k_ref, v_ref, og_ref, gt_ref, bif_ref, hg_ref, o_ref, c_scr, n_scr, m_scr)


def _mlstm_head(sub, q_ref, k_ref, v_ref, og_ref, gt_ref, bif_ref, hg_ref, o_ref, c_scr, n_scr, m_scr):
    c = pl.program_id(0)
    h = pl.program_id(1) * MLSTM_HEADS_PER_STEP + sub
    L, dk, dv = A_CHUNK, A_DQK, A_DV
    qscale = dk ** -0.5

    @pl.when(c == 0)
    def _():
        c_scr[h] = jnp.zeros(c_scr.shape[1:], f32)
        n_scr[h] = jnp.zeros(n_scr.shape[1:], f32)
        m_scr[h] = jnp.zeros(m_scr.shape[1:], f32)

    q = q_ref[:, sub * dk:(sub + 1) * dk]
    k = k_ref[:, sub * dk:(sub + 1) * dk]
    v = v_ref[:, sub * dv:(sub + 1) * dv]
    li = gt_ref[pl.ds(h, 1), :] + bif_ref[pl.ds(h, 1), :]
    xf = gt_ref[pl.ds(h + A_HEADS, 1), :] + bif_ref[pl.ds(h + A_HEADS, 1), :]
    lf = jnp.minimum(xf, 0.0) - jnp.log1p(jnp.exp(-jnp.abs(xf)))

    ii = lax.broadcasted_iota(jnp.int32, (L, L), 0)
    jj = lax.broadcasted_iota(jnp.int32, (L, L), 1)
    upper = jnp.where(ii <= jj, 1.0, 0.0).astype(bf16)
    b_row = _dot_f32_exactrhs(jnp.broadcast_to(lf, (8, L)), upper)[0:1]
    eye = ii == jj

    def to_col(row):
        return jnp.sum(jnp.where(eye, row, 0.0), axis=1, keepdims=True)

    b_col = to_col(b_row)
    causal = jj <= ii
    dmat = b_col - b_row + li
    m_prev = m_scr[h]
    m_inter = b_col + m_prev
    m_t = jnp.maximum(m_inter, jnp.max(jnp.where(causal, dmat, -jnp.inf), axis=1, keepdims=True))
    dexp = jnp.where(causal, jnp.exp(dmat - m_t), 0.0)
    s = _nt(q, k) * qscale * dexp
    w_inter = jnp.exp(m_inter - m_t)
    ct = c_scr[h]
    n_row = n_scr[h]
    inter = _dot(q, ct.astype(bf16)) * qscale
    num = _dot(s.astype(bf16), v) + w_inter * inter
    qn = jnp.sum(q.astype(f32) * n_row, axis=1, keepdims=True) * qscale
    den = jnp.sum(s, axis=1, keepdims=True) + w_inter * qn
    hc = num / jnp.maximum(jnp.abs(den), jnp.exp(-m_t))

    b_end = b_row[:, L - 1:L]
    w_log = b_end - b_row + li
    m_new = jnp.maximum(b_end + m_prev, jnp.max(w_log, axis=1, keepdims=True))
    decay = jnp.exp(b_end + m_prev - m_new)
    w_wr = jnp.exp(w_log - m_new)
    vw = (v.astype(f32) * to_col(w_wr)).astype(bf16)
    kt = k.astype(f32).T.astype(bf16)
    c_scr[h] = decay * ct + _dot(kt, vw)
    n_scr[h] = decay * n_row + _dot(jnp.broadcast_to(w_wr, (8, L)).astype(bf16), k)[0:1]
    m_scr[h] = m_new

    hn = _rms(hc, hg_ref[sub])
    og = og_ref[:, sub * dv:(sub + 1) * dv].astype(f32)
    o_ref[:, sub * dv:(sub + 1) * dv] = (jax.nn.sigmoid(og) * hn).astype(o_ref.dtype)


def _mlstm(proj, gates_t, b_if, head_g):
    t_len = proj.shape[0]
    H, dk, dv, L = A_HEADS, A_DQK, A_DV, A_CHUNK
    hps = MLSTM_HEADS_PER_STEP
    wk, wv = hps * dk, hps * dv
    return pl.pallas_call(
        _mlstm_kernel,
        grid=(t_len // L, H // hps),
        in_specs=[
            pl.BlockSpec((L, wk), lambda c, h: (c, h)),
            pl.BlockSpec((L, wk), lambda c, h: (c, H * dk // wk + h)),
            pl.BlockSpec((L, wv), lambda c, h: (c, 2 * H * dk // wv + h)),
            pl.BlockSpec((L, wv), lambda c, h: (c, (2 * H * dk + H * dv) // wv + h)),
            pl.BlockSpec((2 * H, L), lambda c, h: (0, c)),
            pl.BlockSpec((2 * H, 1), lambda c, h: (0, 0)),
            pl.BlockSpec((hps, 1, dv), lambda c, h: (h, 0, 0)),
        ],
        out_specs=pl.BlockSpec((L, wv), lambda c, h: (c, h)),
        out_shape=jax.ShapeDtypeStruct((t_len, H * dv), bf16),
        scratch_shapes=[pltpu.VMEM((H, dk, dv), f32), pltpu.VMEM((H, 1, dk), f32), pltpu.VMEM((H, 1, 1), f32)],
        compiler_params=_params("arbitrary", "arbitrary"),
        name="mlstm",
    )(proj, proj, proj, proj, gates_t, b_if.reshape(2 * H, 1), head_g.reshape(H, 1, dv))


def _memattn_kernel(x_ref, g_ref, wq_ref, kv_ref, qg_ref, kg_ref, wo_ref, o_ref):
    x = x_ref[...]
    h = _rms(x, g_ref[...]).astype(bf16)
    qp = _dot(h, wq_ref[...])
    dh = MEM_HEAD_DIM
    nkv = MEM_HEADS * dh
    outs = []
    for hd in range(MEM_HEADS):
        qn = _rms(qp[:, hd * dh:(hd + 1) * dh], qg_ref[...]).astype(bf16)
        kn = _rms(kv_ref[:, hd * dh:(hd + 1) * dh], kg_ref[...]).astype(bf16)
        vh = kv_ref[:, nkv + hd * dh:nkv + (hd + 1) * dh].astype(bf16)
        s = _nt(qn, kn) * (dh ** -0.5)
        e = jnp.exp(s - jnp.max(s, axis=1, keepdims=True))
        p = e / jnp.sum(e, axis=1, keepdims=True)
        outs.append(_dot(p.astype(bf16), vh).astype(bf16))
    o = jnp.concatenate(outs, axis=1)
    o_ref[...] = x + _dot(o, wo_ref[...])


def _memattn(x, gain, wq, kv, qg, kg, wo, *, tm):
    t_len, d = x.shape
    md = MEM_HEADS * MEM_HEAD_DIM
    return pl.pallas_call(
        _memattn_kernel,
        grid=(t_len // tm,),
        in_specs=[
            pl.BlockSpec((tm, d), lambda i: (i, 0)),
            pl.BlockSpec((1, d), lambda i: (0, 0)),
            pl.BlockSpec((d, md), lambda i: (0, 0)),
            pl.BlockSpec((N_MEM, 2 * md), lambda i: (0, 0)),
            pl.BlockSpec((1, MEM_HEAD_DIM), lambda i: (0, 0)),
            pl.BlockSpec((1, MEM_HEAD_DIM), lambda i: (0, 0)),
            pl.BlockSpec((md, d), lambda i: (0, 0)),
        ],
        out_specs=pl.BlockSpec((tm, d), lambda i: (i, 0)),
        out_shape=jax.ShapeDtypeStruct((t_len, d), f32),
        compiler_params=_params("parallel"),
        name="memattn",
    )(x, gain.reshape(1, d), wq, kv, qg.reshape(1, -1), kg.reshape(1, -1), wo)


def _router_kernel(x_ref, g_ref, wr_ref, br_ref, h_ref, comb_ref, rank_ref, cnt_ref, carry_scr):
    tm = x_ref.shape[0]

    @pl.when(pl.program_id(0) == 0)
    def _():
        carry_scr[...] = jnp.zeros(carry_scr.shape, f32)

    hf = _rms(x_ref[...], g_ref[...])
    h_ref[...] = hf
    h1, h2, _ = _split3(hf)
    w1, w2, _ = _split3(wr_ref[...])
    logits = _dot(h2, w1) + _dot(h1, w2) + _dot(h1, w1) + br_ref[...]
    col = [logits[:, i:i + 1] for i in range(N_GROUPS + N_EXPERTS)]
    gl = col[:N_GROUPS]
    gm = functools.reduce(jnp.maximum, gl)
    ge = [jnp.exp(z - gm) for z in gl]
    gs = functools.reduce(jnp.add, ge)
    gp = [z / gs for z in ge]
    g_top = functools.reduce(jnp.maximum, gp)
    sel_g = []
    taken = jnp.zeros(g_top.shape, jnp.bool_)
    for z in gp:
        hit = (z == g_top) & jnp.logical_not(taken)
        sel_g.append(hit)
        taken = taken | hit
    E = EXPERTS_PER_GROUP
    el = []
    for e in range(E):
        acc = jnp.zeros_like(g_top)
        for g in range(N_GROUPS):
            acc = acc + jnp.where(sel_g[g], col[N_GROUPS + g * E + e], 0.0)
        el.append(acc)
    em = functools.reduce(jnp.maximum, el)
    ee = [jnp.exp(z - em) for z in el]
    es = functools.reduce(jnp.add, ee)
    ep = [z / es for z in ee]
    sel_e = []
    for e in range(E):
        rank = jnp.zeros(g_top.shape, jnp.int32)
        for e2 in range(E):
            if e2 == e:
                continue
            ahead = (ep[e2] > ep[e]) | ((ep[e2] == ep[e]) & (e2 < e))
            rank = rank + ahead.astype(jnp.int32)
        sel_e.append(rank < 2)
    top_sum = functools.reduce(jnp.add, [jnp.where(sel_e[e], ep[e], 0.0) for e in range(E)])
    lane = lax.broadcasted_iota(jnp.int32, comb_ref.shape, 1)
    comb = jnp.zeros(comb_ref.shape, f32)
    for g in range(N_GROUPS):
        for e in range(E):
            wge = jnp.where(sel_g[g] & sel_e[e], ep[e] / top_sum * g_top, 0.0)
            comb = jnp.where(lane == g * E + e, wge, comb)
    comb_ref[...] = comb

    member = jnp.where(comb > 0.0, 1.0, 0.0)
    r_i = lax.broadcasted_iota(jnp.int32, (tm, tm), 0)
    c_i = lax.broadcasted_iota(jnp.int32, (tm, tm), 1)
    earlier = jnp.where(c_i < r_i, 1.0, 0.0).astype(bf16)
    carry = carry_scr[...]
    rank_ref[...] = _dot(earlier, member.astype(bf16)) + carry
    carry = carry + jnp.sum(member, axis=0, keepdims=True)
    carry_scr[...] = carry
    cnt_ref[...] = carry


def _router(x, gain, w_r, b_r, *, tm):
    t_len, d = x.shape
    row = lambda i: (i, 0)
    return pl.pallas_call(
        _router_kernel,
        grid=(t_len // tm,),
        in_specs=[
            pl.BlockSpec((tm, d), row),
            pl.BlockSpec((1, d), lambda i: (0, 0)),
            pl.BlockSpec((d, LANES), lambda i: (0, 0)),
            pl.BlockSpec((1, LANES), lambda i: (0, 0)),
        ],
        out_specs=[pl.BlockSpec((tm, d), row), pl.BlockSpec((tm, LANES), row),
                   pl.BlockSpec((tm, LANES), row), pl.BlockSpec((1, LANES), lambda i: (0, 0))],
        out_shape=[jax.ShapeDtypeStruct((t_len, d), f32), jax.ShapeDtypeStruct((t_len, LANES), f32),
                   jax.ShapeDtypeStruct((t_len, LANES), f32), jax.ShapeDtypeStruct((1, LANES), f32)],
        scratch_shapes=[pltpu.VMEM((1, LANES), f32)],
        compiler_params=_params("arbitrary"),
        name="moe_router",
    )(x, gain.reshape(1, d), w_r, b_r)


def _row(ref, r):
    return ref.at[pl.ds(r, 1)]


def _meta_kernel(comb_ref, rank_ref, cnt_ref, pos_ref, w_ref, te_ref, nu_ref):
    comb = comb_ref[...]
    lane1 = lax.broadcasted_iota(jnp.int32, (1, LANES), 1)
    cnt = cnt_ref[...]
    gsz = jnp.ceil(cnt * (1.0 / MOE_TM)) * MOE_TM
    e_r = lax.broadcasted_iota(jnp.int32, (LANES, LANES), 0)
    e_c = lax.broadcasted_iota(jnp.int32, (LANES, LANES), 1)
    before = jnp.where(e_r < e_c, 1.0, 0.0).astype(bf16)
    goff = _dot_f32_exactrhs(jnp.broadcast_to(gsz, (8, LANES)), before)[0:1]
    member = comb > 0.0
    pos = jnp.where(member, goff + rank_ref[...], -1.0)
    pa = jnp.max(pos, axis=1, keepdims=True)
    is_a = member & (pos == pa)
    wa = jnp.sum(jnp.where(is_a, comb, 0.0), axis=1, keepdims=True)
    rest = member & jnp.logical_not(is_a)
    pb = jnp.max(jnp.where(rest, pos, -1.0), axis=1, keepdims=True)
    wb = jnp.sum(jnp.where(rest, comb, 0.0), axis=1, keepdims=True)
    lane = lax.broadcasted_iota(jnp.int32, comb.shape, 1)
    pos_ref[...] = jnp.where(lane == 0, pa, jnp.where(lane == 1, pb, -1.0)).astype(jnp.int32)
    w_ref[...] = jnp.where(lane == 0, wa, jnp.where(lane == 1, wb, 0.0))
    valid_e = lane1 < N_EXPERTS
    gend = goff + gsz
    tile_start = lax.broadcasted_iota(jnp.int32, (LANES, LANES), 0).astype(f32) * MOE_TM
    te = jnp.sum(jnp.where(valid_e & (gend <= tile_start), 1.0, 0.0), axis=1, keepdims=True)
    last = jnp.max(jnp.where(valid_e & (gsz > 0.0), lane1.astype(f32), 0.0), axis=1, keepdims=True)
    te_ref[...] = jnp.minimum(te, last).astype(jnp.int32)
    nu_ref[...] = (jnp.sum(jnp.where(valid_e, gsz, 0.0), axis=1, keepdims=True) * (1.0 / MOE_TM)).astype(jnp.int32)


def _moe_meta(comb, rank, cnt, *, tm):
    t_len = comb.shape[0]
    row = lambda i: (i, 0)
    fixed = lambda i: (0, 0)
    return pl.pallas_call(
        _meta_kernel,
        grid=(t_len // tm,),
        in_specs=[pl.BlockSpec((tm, LANES), row), pl.BlockSpec((tm, LANES), row), pl.BlockSpec((1, LANES), fixed)],
        out_specs=[pl.BlockSpec((tm, LANES), row), pl.BlockSpec((tm, LANES), row),
                   pl.BlockSpec((LANES, 1), fixed), pl.BlockSpec((1, 1), fixed)],
        out_shape=[jax.ShapeDtypeStruct((t_len, LANES), jnp.int32), jax.ShapeDtypeStruct((t_len, LANES), f32),
                   jax.ShapeDtypeStruct((LANES, 1), jnp.int32), jax.ShapeDtypeStruct((1, 1), jnp.int32)],
        compiler_params=_params("arbitrary"),
        name="moe_meta",
    )(comb, rank, cnt)


def _invmap_kernel(pos_ref, src_ref):
    n_tok = pos_ref.shape[0] // EXPERT_TOPK

    def fill(p, carry):
        src_ref[p] = p & (n_tok - 1)
        return carry

    lax.fori_loop(0, src_ref.shape[0], fill, 0, unroll=16)

    def body(k, carry):
        for slot in range(EXPERT_TOPK):
            p = pos_ref[EXPERT_TOPK * k + slot]

            @pl.when(p >= 0)
            def _():
                src_ref[p] = k
        return carry

    lax.fori_loop(0, pos_ref.shape[0] // EXPERT_TOPK, body, 0, unroll=8)


def _moe_invmap(pos_flat):
    return pl.pallas_call(
        _invmap_kernel,
        in_specs=[pl.BlockSpec(memory_space=pltpu.SMEM)],
        out_specs=pl.BlockSpec(memory_space=pltpu.SMEM),
        out_shape=jax.ShapeDtypeStruct((MOE_ROWS,), jnp.int32),
        name="moe_invmap",
    )(pos_flat)


def _gather_kernel(src_ref, nu_ref, h_ref, xs_ref, buf, sems):
    i = pl.program_id(0)
    n_used = nu_ref[0]

    def copy(tile, slot, k):
        return pltpu.make_async_copy(_row(h_ref, src_ref[tile * MOE_TM + k]), _row(buf.at[slot], k), sems.at[slot])

    def start_tile(tile, slot):
        def body(k, carry):
            copy(tile, slot, k).start()
            return carry

        lax.fori_loop(0, MOE_TM, body, 0, unroll=8)

    @pl.when((i == 0) & (n_used > 0))
    def _():
        start_tile(0, 0)

    for slot in range(2):
        @pl.when((i % 2 == slot) & (i < n_used))
        def _():
            @pl.when(i + 1 < n_used)
            def _():
                start_tile(i + 1, 1 - slot)

            def wait(k, carry):
                copy(i, slot, k).wait()
                return carry

            lax.fori_loop(0, MOE_TM, wait, 0, unroll=8)
            xs_ref[...] = buf[slot].astype(xs_ref.dtype)

    @pl.when(i >= n_used)
    def _():
        xs_ref[...] = jnp.zeros(xs_ref.shape, xs_ref.dtype)


def _moe_gather(src, nused, h):
    d = h.shape[1]
    return pl.pallas_call(
        _gather_kernel,
        grid_spec=pltpu.PrefetchScalarGridSpec(
            num_scalar_prefetch=2,
            grid=(MOE_TILES,),
            in_specs=[pl.BlockSpec(memory_space=pl.ANY)],
            out_specs=pl.BlockSpec((MOE_TM, d), lambda i, src, nu: (i, 0)),
            scratch_shapes=[pltpu.VMEM((2, MOE_TM, d), f32), pltpu.SemaphoreType.DMA((2,))],
        ),
        out_shape=jax.ShapeDtypeStruct((MOE_ROWS, d), bf16),
        compiler_params=pltpu.CompilerParams(dimension_semantics=("arbitrary",), vmem_limit_bytes=VMEM_LIMIT,
                                             disable_bounds_checks=True),
        name="moe_gather",
    )(src, nused, h)


def _expert_changed(te_ref, i):
    return (i == 0) | (te_ref[i] != te_ref[jnp.maximum(i - 1, 0)])


def _moe_hid_kernel(te_ref, nu_ref, xs_ref, w1_ref, w3_ref, o_ref, w1b, w3b):
    i = pl.program_id(1)

    @pl.when(_expert_changed(te_ref, i))
    def _():
        w1b[...] = w1_ref[...].astype(bf16)
        w3b[...] = w3_ref[...].astype(bf16)

    @pl.when(i < nu_ref[0])
    def _():
        x = xs_ref[...]
        u = _dot(x, w1b[...])
        g = _dot(x, w3b[...])
        o_ref[...] = (u * jax.nn.sigmoid(u) * g).astype(o_ref.dtype)

    @pl.when(i >= nu_ref[0])
    def _():
        o_ref[...] = jnp.zeros(o_ref.shape, o_ref.dtype)


def _moe_hid(te, nused, xs, w1, w3, layer, *, tn):
    _, ne, d, fdim = w1.shape
    used = lambda i, nu: jnp.where(i < nu[0], i, 0)
    return pl.pallas_call(
        _moe_hid_kernel,
        grid_spec=pltpu.PrefetchScalarGridSpec(
            num_scalar_prefetch=2,
            grid=(fdim // tn, MOE_TILES),
            in_specs=[
                pl.BlockSpec((MOE_TM, d), lambda j, i, te, nu: (used(i, nu), 0)),
                pl.BlockSpec((None, None, d, tn), lambda j, i, te, nu: (layer, te[i], 0, j)),
                pl.BlockSpec((None, None, d, tn), lambda j, i, te, nu: (layer, te[i], 0, j)),
            ],
            out_specs=pl.BlockSpec((MOE_TM, tn), lambda j, i, te, nu: (i, j)),
            scratch_shapes=[pltpu.VMEM((d, tn), bf16), pltpu.VMEM((d, tn), bf16)],
        ),
        out_shape=jax.ShapeDtypeStruct((MOE_ROWS, fdim), bf16),
        compiler_params=_params("arbitrary", "arbitrary"),
        name="moe_hid",
    )(te, nused, xs, w1, w3)


def _moe_w2_kernel(te_ref, nu_ref, hid_ref, w2_ref, o_ref, w2b):
    i = pl.program_id(0)

    @pl.when(_expert_changed(te_ref, i))
    def _():
        w2b[...] = w2_ref[...].astype(bf16)

    @pl.when(i < nu_ref[0])
    def _():
        o_ref[...] = _dot(hid_ref[...], w2b[...])

    @pl.when(i >= nu_ref[0])
    def _():
        o_ref[...] = jnp.zeros(o_ref.shape, o_ref.dtype)


def _moe_w2(te, nused, hid, w2, layer):
    _, ne, fdim, d = w2.shape
    used = lambda i, nu: jnp.where(i < nu[0], i, 0)
    return pl.pallas_call(
        _moe_w2_kernel,
        grid_spec=pltpu.PrefetchScalarGridSpec(
            num_scalar_prefetch=2,
            grid=(MOE_TILES,),
            in_specs=[
                pl.BlockSpec((MOE_TM, fdim), lambda i, te, nu: (used(i, nu), 0)),
                pl.BlockSpec((None, None, fdim, d), lambda i, te, nu: (layer, te[i], 0, 0),
                             pipeline_mode=pl.Buffered(1)),
            ],
            out_specs=pl.BlockSpec((MOE_TM, d), lambda i, te, nu: (i, 0)),
            scratch_shapes=[pltpu.VMEM((fdim, d), bf16)],
        ),
        out_shape=jax.ShapeDtypeStruct((MOE_ROWS, d), f32),
        compiler_params=_params("arbitrary"),
        name="moe_w2",
    )(te, nused, hid, w2)


def _moe_combine_kernel(pos_ref, x_ref, w_ref, ys_ref, o_ref, buf, sems):
    tm, d = x_ref.shape
    i = pl.program_id(0)

    def copies(tile, slot, k):
        out = []
        for which in range(EXPERT_TOPK):
            p = jnp.maximum(pos_ref[EXPERT_TOPK * (tile * tm + k) + which], 0)
            out.append(pltpu.make_async_copy(_row(ys_ref, p), _row(buf.at[slot, which], k), sems.at[slot]))
        return out

    def start_tile(tile, slot):
        def body(k, carry):
            for cp in copies(tile, slot, k):
                cp.start()
            return carry

        lax.fori_loop(0, tm, body, 0, unroll=8)

    @pl.when(i == 0)
    def _():
        start_tile(0, 0)

    for slot in range(2):
        @pl.when(i % 2 == slot)
        def _():
            @pl.when(i + 1 < pl.num_programs(0))
            def _():
                start_tile(i + 1, 1 - slot)

            def wait(k, carry):
                for cp in copies(i, slot, k):
                    cp.wait()
                return carry

            lax.fori_loop(0, tm, wait, 0, unroll=8)
            o_ref[...] = x_ref[...] + w_ref[:, 0:1] * buf[slot, 0] + w_ref[:, 1:2] * buf[slot, 1]


def _moe_combine(pos_flat, x, tokw, ys, *, tm):
    t_len, d = x.shape
    return pl.pallas_call(
        _moe_combine_kernel,
        grid_spec=pltpu.PrefetchScalarGridSpec(
            num_scalar_prefetch=1,
            grid=(t_len // tm,),
            in_specs=[
                pl.BlockSpec((tm, d), lambda i, pos: (i, 0)),
                pl.BlockSpec((tm, LANES), lambda i, pos: (i, 0)),
                pl.BlockSpec(memory_space=pl.ANY),
            ],
            out_specs=pl.BlockSpec((tm, d), lambda i, pos: (i, 0)),
            scratch_shapes=[pltpu.VMEM((2, EXPERT_TOPK, tm, d), f32), pltpu.SemaphoreType.DMA((2,))],
        ),
        out_shape=jax.ShapeDtypeStruct((t_len, d), f32),
        compiler_params=pltpu.CompilerParams(dimension_semantics=("arbitrary",), vmem_limit_bytes=VMEM_LIMIT,
                                             disable_bounds_checks=True),
        name="moe_combine",
    )(pos_flat, x, tokw, ys)


def _hier_moe(x, gain, wg, bg, we, be, w1, w3, w2, layer):
    d = x.shape[1]
    w_r = jnp.concatenate([wg, we.transpose(1, 0, 2).reshape(d, N_EXPERTS)], axis=1)
    w_r = jnp.pad(w_r, ((0, 0), (0, LANES - w_r.shape[1])))
    b_r = jnp.pad(jnp.concatenate([bg, be.reshape(-1)]), (0, LANES - N_GROUPS - N_EXPERTS)).reshape(1, LANES)
    h, comb, rank, cnt = _router(x, gain, w_r, b_r, tm=256)
    tokpos, tokw, te, nused = _moe_meta(comb, rank, cnt, tm=512)
    pos_flat = tokpos[:, :EXPERT_TOPK].reshape(-1)
    te = te.reshape(-1)
    nused = nused.reshape(-1)
    src = _moe_invmap(pos_flat)
    xs = _moe_gather(src, nused, h)
    hid = _moe_hid(te, nused, xs, w1, w3, layer, tn=512)
    ys = _moe_w2(te, nused, hid, w2, layer)
    return _moe_combine(pos_flat, x, tokw, ys, tm=256)


def _compress_kernel(z_ref, pe_ref, w1_ref, w2_ref, kg_ref, o_ref):
    half = CMP_STRIDE * B_HEAD_DIM
    z = z_ref[...].astype(f32)
    zt = (z + pe_ref[:, :half]).astype(bf16)
    zb = (z + pe_ref[:, half:]).astype(bf16)
    top = _dot(zt, w1_ref[:half, :])
    bot = _dot(zb, w1_ref[half:, :])
    pre = top + pltpu.roll(bot, N_CMP_PAD - 1, axis=0)
    act = pre * (0.5 * (1.0 + jnp.tanh(np.sqrt(2.0 / np.pi).astype(np.float32) * (pre + 0.044715 * (pre * pre * pre)))))
    out = _dot(act.astype(bf16), w2_ref[...])
    normed = _rms(out, kg_ref[...])
    o_ref[...] = jnp.where(pl.program_id(0) == 0, normed, out).astype(o_ref.dtype)


def _compress(zr, pe_flat, w1, w2, kg):
    G = B_KV_HEADS
    return pl.pallas_call(
        _compress_kernel,
        grid=(2, G),
        in_specs=[
            pl.BlockSpec((None, None, N_CMP_PAD, CMP_STRIDE * B_HEAD_DIM), lambda w, g: (w, g, 0, 0)),
            pl.BlockSpec((None, 1, CMP_BLOCK * B_HEAD_DIM), lambda w, g: (w, 0, 0)),
            pl.BlockSpec((None, CMP_BLOCK * B_HEAD_DIM, CMP_HIDDEN), lambda w, g: (w, 0, 0)),
            pl.BlockSpec((None, CMP_HIDDEN, B_HEAD_DIM), lambda w, g: (w, 0, 0)),
            pl.BlockSpec((1, B_HEAD_DIM), lambda w, g: (0, 0)),
        ],
        out_specs=pl.BlockSpec((None, None, N_CMP_PAD, B_HEAD_DIM), lambda w, g: (w, g, 0, 0)),
        out_shape=jax.ShapeDtypeStruct((2, G, N_CMP_PAD, B_HEAD_DIM), bf16),
        compiler_params=_params("arbitrary", "arbitrary"),
        name="nsa_compress",
    )(zr, pe_flat, w1, w2, kg)


def _key_extra(pos):
    lane = lax.broadcasted_iota(jnp.int32, pos.shape, 1)
    lo = pos & (LANES - 1)
    return jnp.where(lane < 3, lo.astype(f32), jnp.where(lane < 6, (pos - lo).astype(f32), 0.0)).astype(bf16)


def _nsa_kernel(q_ref, gate_ref, sl_ref, kc_ref, vct_ref, ks_ref, vst_ref, kw_ref, vwt_ref, o_ref,
                qa_scr, selb_scr, m_scr, acc_scr, out_scr):
    c = pl.program_id(1)
    R, QB, dh = B_GROUP, Q_BLOCK, B_HEAD_DIM
    rows = R * QB
    t0 = c * QB
    slabs = [slice(r * QB, (r + 1) * QB) for r in range(R)]

    qa_scr[0:dh, :] = jnp.concatenate([q_ref[r].astype(f32).T for r in range(R)], axis=1).astype(bf16)
    s1, s2, s3 = [p.astype(f32) for p in _split3(sl_ref[...] * LOG2E)]
    rid = lax.broadcasted_iota(jnp.int32, (dh, rows), 0)
    qa_scr[dh:2 * dh, :] = jnp.where((rid == 0) | (rid == 3), s1,
                                     jnp.where((rid == 1) | (rid == 4), s2,
                                               jnp.where((rid == 2) | (rid == 5), s3, 0.0))).astype(bf16)

    kidx = lax.broadcasted_iota(jnp.int32, (QB, QB), 0)
    tidx = lax.broadcasted_iota(jnp.int32, (QB, QB), 1)
    causal = kidx <= tidx

    gate_t = jax.nn.sigmoid(gate_ref[...]).T

    def scores(k_rows, pos):
        return jnp.concatenate([k_rows, _key_extra(pos)], axis=1)

    hpg = NSA_HEADS_PER_GROUP
    groups = [(slice(g0 * QB, (g0 + hpg) * QB), list(range(g0, g0 + hpg))) for g0 in range(0, R, hpg)]

    def first_chunk(ka, ok, v_t):
        for cols, heads in groups:
            s_g = _dot(ka, qa_scr[:, cols])
            ps = []
            for n, r in enumerate(heads):
                s_r = jnp.where(ok, s_g[:, n * QB:(n + 1) * QB], NEG_INF)
                m = jnp.max(s_r, axis=0, keepdims=True)
                m_scr[:, slabs[r]] = m
                ps.append(jnp.where(ok, jnp.exp2(s_r - m), 0.0).astype(bf16))
            acc_scr[:, cols] = _dot(v_t, jnp.concatenate(ps, axis=1))

    def next_chunk(ka, mask_bias, v_t):
        for cols, heads in groups:
            s_g = _dot(ka, qa_scr[:, cols])
            ps, alphas = [], []
            for n, r in enumerate(heads):
                s_r = s_g[:, n * QB:(n + 1) * QB] + mask_bias
                m_old = m_scr[:, slabs[r]]
                m_new = jnp.maximum(m_old, jnp.max(s_r, axis=0, keepdims=True))
                m_scr[:, slabs[r]] = m_new
                alphas.append(jnp.exp2(m_old - m_new))
                ps.append(jnp.exp2(s_r - m_new).astype(bf16))
            pv = _dot(v_t, jnp.concatenate(ps, axis=1))
            for n, r in enumerate(heads):
                acc_scr[:, slabs[r]] = alphas[n] * acc_scr[:, slabs[r]] + pv[:, n * QB:(n + 1) * QB]

    def emit(branch):
        for r in range(R):
            w = gate_t[branch * R + r:branch * R + r + 1, :] / acc_scr[dh:dh + 1, slabs[r]]
            out_scr[:, slabs[r]] = out_scr[:, slabs[r]] + acc_scr[0:dh, slabs[r]] * w

    cend = lax.broadcasted_iota(jnp.int32, (N_CMP_PAD, QB), 0) * CMP_STRIDE + (CMP_BLOCK - 1)
    ok_c = cend <= t0 + lax.broadcasted_iota(jnp.int32, (N_CMP_PAD, QB), 1)
    s_c = _dot(scores(kc_ref[...], cend), qa_scr[...])
    p_sum = jnp.zeros((N_CMP_PAD, QB), f32)
    ps = []
    for r in range(R):
        s_r = jnp.where(ok_c, s_c[:, slabs[r]], NEG_INF)
        e = jnp.where(ok_c, jnp.exp2(s_r - jnp.max(s_r, axis=0, keepdims=True)), 0.0)
        l = jnp.sum(e, axis=0, keepdims=True)
        p = e / jnp.where(l > 0.0, l, 1.0)
        p_sum = p_sum + p
        ps.append(p.astype(bf16))
    oc_t = _dot(vct_ref[...], jnp.concatenate(ps, axis=1))
    for r in range(R):
        out_scr[:, slabs[r]] = oc_t[:, slabs[r]] * gate_t[r:r + 1, :]

    b_i = lax.broadcasted_iota(jnp.int32, (N_SEL, N_CMP_PAD), 0) * SEL_BLOCK
    n_i = lax.broadcasted_iota(jnp.int32, (N_SEL, N_CMP_PAD), 1) * CMP_STRIDE
    overlap_t = jnp.where((n_i < b_i + SEL_BLOCK) & (n_i + (CMP_BLOCK - 1) >= b_i), 1.0, 0.0).astype(bf16)
    p1, p2, p3 = _split3(p_sum)
    imp = _dot(overlap_t, p3) + _dot(overlap_t, p2) + _dot(overlap_t, p1)
    tq = t0 + tidx
    cur = tq // SEL_BLOCK
    forced = (kidx == 0) | (kidx == cur) | (kidx == cur - 1)
    imp = jnp.where(forced, SEL_FORCE, imp)
    imp = jnp.where(kidx * SEL_BLOCK <= tq, imp, -1.0)
    blk_f = kidx.astype(f32)
    sel = jnp.zeros((N_SEL, QB), f32)
    for _ in range(SEL_TOPK):
        mx = jnp.max(imp, axis=0, keepdims=True)
        first = jnp.min(jnp.where(imp == mx, blk_f, float(N_SEL)), axis=0, keepdims=True)
        pick = blk_f == first
        sel = jnp.where(pick & (mx >= 0.0), 1.0, sel)
        imp = jnp.where(pick, -jnp.inf, imp)
    selb_scr[...] = jnp.where(sel > 0.5, 0.0, NEG_INF)

    def sel_bias(first_block, n_blocks):
        return jnp.concatenate([jnp.broadcast_to(selb_scr[pl.ds(first_block + b, 1), :], (SEL_BLOCK, QB))
                                for b in range(n_blocks)], axis=0)

    def chunk_pos(i):
        return i * QB + kidx

    off_c = pl.multiple_of(t0, QB)
    first_chunk(scores(ks_ref[pl.ds(off_c, QB), :], chunk_pos(c)), causal & (sel_bias(2 * c, 2) > -1.0),
                vst_ref[c])
    wide = SEL_WIDE_CHUNKS
    blocks_per_chunk = QB // SEL_BLOCK

    def sel_step(first, n):
        off = pl.multiple_of(first * QB, QB)
        pos = off + lax.broadcasted_iota(jnp.int32, (n * QB, QB), 0)
        v_t = vst_ref[first] if n == 1 else jnp.concatenate([vst_ref[first + m] for m in range(n)], axis=1)
        next_chunk(scores(ks_ref[pl.ds(off, n * QB), :], pos),
                   sel_bias(blocks_per_chunk * first, blocks_per_chunk * n), v_t)

    def sel_wide(j, carry):
        sel_step(j * wide, wide)
        return carry

    def sel_narrow(i, carry):
        sel_step(i, 1)
        return carry

    half = wide // 2
    done = (c // wide) * wide
    lax.fori_loop(0, c // wide, sel_wide, 0)

    @pl.when(c - done >= half)
    def _():
        sel_step(done, half)

    lax.fori_loop(jnp.where(c - done >= half, done + half, done), c, sel_narrow, 0)
    emit(1)

    first_chunk(scores(kw_ref[pl.ds(off_c, QB), :], chunk_pos(c)), causal, vwt_ref[c])
    n_back = WINDOW // QB

    @pl.when(c >= n_back)
    def _():
        first = c - n_back
        off = pl.multiple_of(first * QB, QB)
        pos = off + lax.broadcasted_iota(jnp.int32, (n_back * QB, QB), 0)
        bias = jnp.concatenate([jnp.where(kidx > tidx, 0.0, NEG_INF), jnp.zeros(((n_back - 1) * QB, QB), f32)], axis=0)
        v_t = jnp.concatenate([vwt_ref[first + n] for n in range(n_back)], axis=1)
        next_chunk(scores(kw_ref[pl.ds(off, n_back * QB), :], pos), bias, v_t)

    @pl.when(c < n_back)
    def _():
        def win_body(i, carry):
            off = pl.multiple_of(i * QB, QB)
            next_chunk(scores(kw_ref[pl.ds(off, QB), :], chunk_pos(i)), jnp.zeros((QB, QB), f32), vwt_ref[i])
            return carry

        lax.fori_loop(0, c, win_body, 0)

    emit(2)

    for r in range(R):
        o_ref[:, r * dh:(r + 1) * dh] = out_scr[:, slabs[r]].T.astype(o_ref.dtype)


def _nsa_attention(qh, gates, slopes, cmp_kv, kvh):
    t_len = qh.shape[1]
    G, R, QB, dh = B_KV_HEADS, B_GROUP, Q_BLOCK, B_HEAD_DIM
    rows = R * QB
    nq = t_len // QB
    kc = cmp_kv[0]
    vc_t = cmp_kv[1].swapaxes(1, 2)
    ones_rows = jnp.zeros((G, nq, BF16_SUBLANES, QB), bf16).at[:, :, 0, :].set(1.0)

    def values_t(v):
        return jnp.concatenate([v.reshape(G, nq, QB, dh).swapaxes(2, 3), ones_rows], axis=2)

    vs_t = values_t(kvh[3 * G:4 * G])
    vw_t = values_t(kvh[5 * G:6 * G])
    dha = dh + BF16_SUBLANES

    def k_spec(base):
        return pl.BlockSpec((None, t_len, dh), lambda g, c: (base + g, 0, 0))

    vt_spec = pl.BlockSpec((None, nq, dha, QB), lambda g, c: (g, 0, 0, 0))
    return pl.pallas_call(
        _nsa_kernel,
        grid=(G, nq),
        in_specs=[
            pl.BlockSpec((R, QB, dh), lambda g, c: (g, c, 0)),
            pl.BlockSpec((QB, LANES), lambda g, c: (c, g)),
            pl.BlockSpec((None, 1, rows), lambda g, c: (g, 0, 0)),
            pl.BlockSpec((None, N_CMP_PAD, dh), lambda g, c: (g, 0, 0)),
            pl.BlockSpec((None, dh, N_CMP_PAD), lambda g, c: (g, 0, 0)),
            k_spec(2 * G), vt_spec, k_spec(4 * G), vt_spec,
        ],
        out_specs=pl.BlockSpec((QB, R * dh), lambda g, c: (c, g)),
        out_shape=jax.ShapeDtypeStruct((t_len, G * R * dh), bf16),
        scratch_shapes=[
            pltpu.VMEM((2 * dh, rows), bf16),
            pltpu.VMEM((N_SEL, QB), f32),
            pltpu.VMEM((1, rows), f32),
            pltpu.VMEM((dha, rows), f32),
            pltpu.VMEM((dh, rows), f32),
        ],
        compiler_params=_params("parallel", "arbitrary"),
        name="nsa_attention",
    )(qh, gates, slopes, kc, vc_t, kvh, vs_t, kvh, vw_t)


def _mlstm_layer(x, gain, w_in, layer, b_if, head_g, w_out):
    H, dk, dv = A_HEADS, A_DQK, A_DV
    n_main = 2 * H * dk + 2 * H * dv
    w_gate = jnp.pad(w_in[layer, :, n_main:], ((0, 0), (0, LANES - 2 * H))).astype(bf16)
    h, = _rmsnorm_bf16(x, [gain], tm=512, name="mlstm_norm")
    proj = _matmul_ws(h, w_in, layer, n_main, out_dtype=bf16, tm=WS_TM, tn=WS_TN, name="mlstm_in")
    gates = _matmul(h, w_gate, out_dtype=f32, tm=1024, tn=LANES, name="mlstm_gates")
    gates_t = gates[:, :2 * H].T
    o = _mlstm(proj, gates_t, b_if, head_g)
    return _matmul_ws(o, w_out, layer, w_out.shape[2], residual=x, out_dtype=f32, tm=WS_TM, tn=WS_TN,
                      name="mlstm_out")


def _memattn_layer(x, gain, mem, mem_g, wq, wk, wv, wo, q_g, k_g):
    wkv = jnp.concatenate([wk, wv], axis=1).astype(bf16)
    kv = _matmul(mem, wkv, gain=mem_g, out_dtype=f32, tm=N_MEM, tn=512, name="mem_kv")
    return _memattn(x, gain, wq.astype(bf16), kv, q_g, k_g, wo.astype(bf16), tm=256)


def _alibi_slopes(n):
    return np.asarray([2.0 ** (-8.0 * (h + 1) / n) for h in range(n)], dtype=np.float32)


def _nsa_shared_kv(x, kv_norm_g, kv_w, cmp_pe, cmp_w1, cmp_w2, k_norm_g):
    G, dh = B_KV_HEADS, B_HEAD_DIM
    t_len = x.shape[0]
    ones = jnp.ones((G, 1, dh), f32)
    zeros = jnp.zeros((G, 1, dh), f32)
    head_gain = jnp.concatenate([ones, ones, ones * k_norm_g[1], ones, ones * k_norm_g[2], ones], axis=0)
    norm_flag = jnp.concatenate([zeros, zeros, ones, zeros, ones, zeros], axis=0)
    h, = _rmsnorm_bf16(x, [kv_norm_g], tm=512, name="nsa_kv_norm")
    kvh = _proj_heads(h, kv_w[None], 0, kv_w.shape[1], head_gain, norm_flag, scale=1.0, tm=WS_TM, name="nsa_kv_proj")
    zr = kvh[:2 * G].reshape(2, G, t_len // CMP_STRIDE, CMP_STRIDE * dh)
    cmp_kv = _compress(zr, cmp_pe.reshape(2, 1, CMP_BLOCK * dh), cmp_w1.astype(bf16), cmp_w2.astype(bf16),
                       k_norm_g[0].reshape(1, dh))
    return cmp_kv, kvh


def _nsa_layer(x, gain, shared, w_in, layer, q_norm_g, w_out):
    cmp_kv, kvh = shared
    H, G, R, dh = B_HEADS, B_KV_HEADS, B_GROUP, B_HEAD_DIM
    ones = jnp.ones((H, 1, dh), f32)
    h, = _rmsnorm_bf16(x, [gain], tm=512, name="nsa_norm")
    qh = _proj_heads(h, w_in, layer, H * dh, ones * q_norm_g, ones, scale=dh ** -0.5 * LOG2E, tm=WS_TM,
                     name="nsa_q_proj")
    wg = w_in[layer, :, H * dh:].reshape(-1, 3, G, R).transpose(0, 2, 1, 3).reshape(-1, G, 3 * R)
    wg = jnp.pad(wg, ((0, 0), (0, 0), (0, LANES - 3 * R))).reshape(-1, G * LANES).astype(bf16)
    gates = _matmul(h, wg, out_dtype=f32, tm=1024, tn=512, name="nsa_gates")
    slopes = jnp.asarray(np.repeat(_alibi_slopes(H).reshape(G, R, 1), Q_BLOCK, axis=2).reshape(G, 1, R * Q_BLOCK))
    o = _nsa_attention(qh, gates, slopes, cmp_kv, kvh)
    return _matmul_ws(o, w_out, layer, w_out.shape[2], residual=x, out_dtype=f32, tm=WS_TM, tn=WS_TN, name="nsa_out")


def kernel(x, mem, norm_g, a_w_in, a_b_if, a_head_g, a_w_out, kv_norm_g, kv_w, cmp_pe, cmp_w1, cmp_w2, k_norm_g,
           b_w_in, b_q_norm_g, b_w_out, mem_norm_g, mem_wq, mem_wk, mem_wv, mem_wo, mem_q_g, mem_k_g, moe_wg,
           moe_bg, moe_we, moe_be, moe_w1, moe_w3, moe_w2):
    bsz, t_len, d = x.shape
    assert (bsz, t_len, d) == (1, SEQ, D_MODEL)
    xs = x.reshape(t_len, d)
    ms = mem.reshape(N_MEM, d)
    depth = norm_g.shape[0]
    n_a = depth - depth // 2
    shared = None
    for layer in range(depth):
        if layer < n_a:
            xs = _mlstm_layer(xs, norm_g[layer, 0], a_w_in, layer, a_b_if[layer], a_head_g[layer], a_w_out)
        else:
            if layer == n_a:
                shared = _nsa_shared_kv(xs, kv_norm_g, kv_w, cmp_pe, cmp_w1, cmp_w2, k_norm_g)
            j = layer - n_a
            xs = _nsa_layer(xs, norm_g[layer, 0], shared, b_w_in, j, b_q_norm_g[j], b_w_out)
        xs = _memattn_layer(xs, norm_g[layer, 1], ms, mem_norm_g[layer], mem_wq[layer], mem_wk[layer],
                            mem_wv[layer], mem_wo[layer], mem_q_g[layer], mem_k_g[layer])
        xs = _hier_moe(xs, norm_g[layer, 2], moe_wg[layer], moe_bg[layer], moe_we[layer], moe_be[layer],
                       moe_w1, moe_w3, moe_w2, layer)
    return xs.reshape(bsz, t_len, d)
```

```python
import functools

import numpy as np
import jax
import jax.numpy as jnp
from jax import lax
from jax.experimental import pallas as pl
from jax.experimental.pallas import tpu as pltpu

f32 = jnp.float32
bf16 = jnp.bfloat16

V7X_VMEM_BYTES = 64 * 1024 * 1024
VMEM_LIMIT = V7X_VMEM_BYTES - 8 * 1024 * 1024
LANES = 128
BF16_SUBLANES = 16
LOG2E = 1.4426950408889634

D_MODEL = 4096
SEQ = 8192
N_MEM = 256
RMS_EPS = 1e-6
NEG_INF = -1e30
A_HEADS = 8
A_DQK = 256
A_DV = 512
A_CHUNK = 128
B_HEADS = 32
B_KV_HEADS = 4
B_HEAD_DIM = 128
B_GROUP = 8
CMP_BLOCK = 32
CMP_STRIDE = 16
CMP_HIDDEN = 256
SEL_BLOCK = 64
SEL_TOPK = 16
WINDOW = 512
Q_BLOCK = 128
SEL_FORCE = 1e6
MEM_HEADS = 4
MEM_HEAD_DIM = 128
N_GROUPS = 4
EXPERTS_PER_GROUP = 4
N_EXPERTS = 16
D_EXPERT = 1024
EXPERT_TOPK = 2
N_CMP_PAD = SEQ // CMP_STRIDE
N_SEL = SEQ // SEL_BLOCK
MOE_TM = 256
MOE_ROWS = EXPERT_TOPK * SEQ + N_EXPERTS * MOE_TM
MOE_TILES = MOE_ROWS // MOE_TM
SEL_WIDE_CHUNKS = 8
MLSTM_HEADS_PER_STEP = 2
NSA_HEADS_PER_GROUP = 8
assert SEQ & (SEQ - 1) == 0
WS_TM, WS_TN = 512, 1024


def _params(*sem):
    return pltpu.CompilerParams(dimension_semantics=sem, vmem_limit_bytes=VMEM_LIMIT)


def _rms(x, g):
    ms = jnp.mean(x * x, axis=-1, keepdims=True)
    return x * lax.rsqrt(ms + RMS_EPS) * g


def _nt(a, b):
    return lax.dot_general(a, b, (((1,), (1,)), ((), ())), preferred_element_type=f32)


def _dot(a, b):
    return jnp.dot(a, b, preferred_element_type=f32)


def _split3(a):
    a1 = a.astype(bf16)
    r1 = a - a1.astype(f32)
    a2 = r1.astype(bf16)
    a3 = (r1 - a2.astype(f32)).astype(bf16)
    return a1, a2, a3


def _dot_f32_exactrhs(a, b_bf16):
    a1, a2, a3 = _split3(a)
    return _dot(a3, b_bf16) + _dot(a2, b_bf16) + _dot(a1, b_bf16)


def _mm_kernel(*refs, norm, residual):
    it = iter(refs)
    a_ref = next(it)
    g_ref = next(it) if norm else None
    w_ref = next(it)
    r_ref = next(it) if residual else None
    o_ref = next(it)
    h_scr = next(it) if norm else None
    if norm:
        @pl.when(pl.program_id(1) == 0)
        def _():
            h_scr[...] = _rms(a_ref[...], g_ref[...]).astype(bf16)
        a = h_scr[...]
    else:
        a = a_ref[...]
    acc = _dot(a, w_ref[...])
    if residual:
        acc = acc + r_ref[...]
    o_ref[...] = acc.astype(o_ref.dtype)


def _matmul(a, w, *, gain=None, residual=None, out_dtype, tm, tn, name):
    m, k = a.shape
    n = w.shape[1]
    norm = gain is not None
    ins = [a]
    specs = [pl.BlockSpec((tm, k), lambda i, j: (i, 0))]
    if norm:
        ins.append(gain.reshape(1, k))
        specs.append(pl.BlockSpec((1, k), lambda i, j: (0, 0)))
    ins.append(w)
    specs.append(pl.BlockSpec((k, tn), lambda i, j: (0, j)))
    if residual is not None:
        ins.append(residual)
        specs.append(pl.BlockSpec((tm, tn), lambda i, j: (i, j)))
    return pl.pallas_call(
        functools.partial(_mm_kernel, norm=norm, residual=residual is not None),
        grid=(m // tm, n // tn),
        in_specs=specs,
        out_specs=pl.BlockSpec((tm, tn), lambda i, j: (i, j)),
        out_shape=jax.ShapeDtypeStruct((m, n), out_dtype),
        scratch_shapes=[pltpu.VMEM((tm, k), bf16)] if norm else [],
        compiler_params=_params("parallel", "arbitrary"),
        name=name,
    )(*ins)


def _rmsnorm_kernel(x_ref, g_ref, *o_refs):
    x = x_ref[...]
    y = x * lax.rsqrt(jnp.mean(x * x, axis=-1, keepdims=True) + RMS_EPS)
    for n, o_ref in enumerate(o_refs):
        o_ref[...] = (y * g_ref[n:n + 1, :]).astype(o_ref.dtype)


def _rmsnorm_bf16(x, gains, *, tm, name):
    m, k = x.shape
    n = len(gains)
    row = lambda i: (i, 0)
    return pl.pallas_call(
        _rmsnorm_kernel,
        grid=(m // tm,),
        in_specs=[pl.BlockSpec((tm, k), row), pl.BlockSpec((n, k), lambda i: (0, 0))],
        out_specs=[pl.BlockSpec((tm, k), row)] * n,
        out_shape=[jax.ShapeDtypeStruct((m, k), bf16)] * n,
        compiler_params=_params("parallel"),
        name=name,
    )(x, jnp.stack(gains))


def _mm_ws_kernel(*refs, residual):
    it = iter(refs)
    a_ref = next(it)
    w_ref = next(it)
    r_ref = next(it) if residual else None
    o_ref = next(it)
    wb_scr = next(it)

    @pl.when(pl.program_id(1) == 0)
    def _():
        wb_scr[...] = w_ref[...].astype(bf16)

    acc = _dot(a_ref[...], wb_scr[...])
    if residual:
        acc = acc + r_ref[...]
    o_ref[...] = acc.astype(o_ref.dtype)


def _matmul_ws(a, w, layer, n_cols, *, residual=None, out_dtype, tm, tn, name):
    m, k = a.shape
    ins = [a, w]
    specs = [pl.BlockSpec((tm, k), lambda j, i: (i, 0)),
             pl.BlockSpec((None, k, tn), lambda j, i: (layer, 0, j), pipeline_mode=pl.Buffered(1))]
    if residual is not None:
        ins.append(residual)
        specs.append(pl.BlockSpec((tm, tn), lambda j, i: (i, j)))
    return pl.pallas_call(
        functools.partial(_mm_ws_kernel, residual=residual is not None),
        grid=(n_cols // tn, m // tm),
        in_specs=specs,
        out_specs=pl.BlockSpec((tm, tn), lambda j, i: (i, j)),
        out_shape=jax.ShapeDtypeStruct((m, n_cols), out_dtype),
        scratch_shapes=[pltpu.VMEM((k, tn), bf16)],
        compiler_params=_params("arbitrary", "arbitrary"),
        name=name,
    )(*ins)


def _proj_heads_kernel(a_ref, w_ref, hg_ref, nf_ref, o_ref, wb_scr, *, heads_per_tile, scale):
    @pl.when(pl.program_id(1) == 0)
    def _():
        wb_scr[...] = w_ref[...].astype(bf16)

    acc = _dot(a_ref[...], wb_scr[...])
    for hd in range(heads_per_tile):
        z = acc[:, hd * LANES:(hd + 1) * LANES]
        zn = _rms(z, hg_ref[hd]) * scale
        o_ref[hd] = jnp.where(nf_ref[hd] > 0.5, zn, z).astype(o_ref.dtype)


def _proj_heads(a, w, layer, n_cols, head_gain, norm_flag, *, scale, tm, name):
    m, k = a.shape
    hpt = WS_TN // LANES
    tn = WS_TN
    nh = n_cols // LANES
    return pl.pallas_call(
        functools.partial(_proj_heads_kernel, heads_per_tile=hpt, scale=scale),
        grid=(n_cols // tn, m // tm),
        in_specs=[
            pl.BlockSpec((tm, k), lambda j, i: (i, 0)),
            pl.BlockSpec((None, k, tn), lambda j, i: (layer, 0, j), pipeline_mode=pl.Buffered(1)),
            pl.BlockSpec((hpt, 1, LANES), lambda j, i: (j, 0, 0)),
            pl.BlockSpec((hpt, 1, LANES), lambda j, i: (j, 0, 0)),
        ],
        out_specs=pl.BlockSpec((hpt, tm, LANES), lambda j, i: (j, i, 0)),
        out_shape=jax.ShapeDtypeStruct((nh, m, LANES), bf16),
        scratch_shapes=[pltpu.VMEM((k, tn), bf16)],
        compiler_params=_params("arbitrary", "arbitrary"),
        name=name,
    )(a, w, head_gain, norm_flag)


def _mlstm_kernel(q_ref, k_ref, v_ref, og_ref, gt_ref, bif_ref, hg_ref, o_ref, c_scr, n_scr, m_scr):
    for sub in range(MLSTM_HEADS_PER_STEP):
        _mlstm_head(sub, q_ref, k_ref, v_ref, og_ref, gt_ref, bif_ref, hg_ref, o_ref, c_scr, n_scr, m_scr)


def _mlstm_head(sub, q_ref, k_ref, v_ref, og_ref, gt_ref, bif_ref, hg_ref, o_ref, c_scr, n_scr, m_scr):
    c = pl.program_id(0)
    h = pl.program_id(1) * MLSTM_HEADS_PER_STEP + sub
    L, dk, dv = A_CHUNK, A_DQK, A_DV
    qscale = dk ** -0.5

    @pl.when(c == 0)
    def _():
        c_scr[h] = jnp.zeros(c_scr.shape[1:], f32)
        n_scr[h] = jnp.zeros(n_scr.shape[1:], f32)
        m_scr[h] = jnp.zeros(m_scr.shape[1:], f32)

    q = q_ref[:, sub * dk:(sub + 1) * dk]
    k = k_ref[:, sub * dk:(sub + 1) * dk]
    v = v_ref[:, sub * dv:(sub + 1) * dv]
    li = gt_ref[pl.ds(h, 1), :] + bif_ref[pl.ds(h, 1), :]
    xf = gt_ref[pl.ds(h + A_HEADS, 1), :] + bif_ref[pl.ds(h + A_HEADS, 1), :]
    lf = jnp.minimum(xf, 0.0) - jnp.log1p(jnp.exp(-jnp.abs(xf)))

    ii = lax.broadcasted_iota(jnp.int32, (L, L), 0)
    jj = lax.broadcasted_iota(jnp.int32, (L, L), 1)
    upper = jnp.where(ii <= jj, 1.0, 0.0).astype(bf16)
    b_row = _dot_f32_exactrhs(jnp.broadcast_to(lf, (8, L)), upper)[0:1]
    eye = ii == jj

    def to_col(row):
        return jnp.sum(jnp.where(eye, row, 0.0), axis=1, keepdims=True)

    b_col = to_col(b_row)
    causal = jj <= ii
    dmat = b_col - b_row + li
    m_prev = m_scr[h]
    m_inter = b_col + m_prev
    m_t = jnp.maximum(m_inter, jnp.max(jnp.where(causal, dmat, -jnp.inf), axis=1, keepdims=True))
    dexp = jnp.where(causal, jnp.exp(dmat - m_t), 0.0)
    s = _nt(q, k) * qscale * dexp
    w_inter = jnp.exp(m_inter - m_t)
    ct = c_scr[h]
    n_row = n_scr[h]
    inter = _dot(q, ct.astype(bf16)) * qscale
    num = _dot(s.astype(bf16), v) + w_inter * inter
    qn = jnp.sum(q.astype(f32) * n_row, axis=1, keepdims=True) * qscale
    den = jnp.sum(s, axis=1, keepdims=True) + w_inter * qn
    hc = num / jnp.maximum(jnp.abs(den), jnp.exp(-m_t))

    b_end = b_row[:, L - 1:L]
    w_log = b_end - b_row + li
    m_new = jnp.maximum(b_end + m_prev, jnp.max(w_log, axis=1, keepdims=True))
    decay = jnp.exp(b_end + m_prev - m_new)
    w_wr = jnp.exp(w_log - m_new)
    vw = (v.astype(f32) * to_col(w_wr)).astype(bf16)
    kt = k.astype(f32).T.astype(bf16)
    c_scr[h] = decay * ct + _dot(kt, vw)
    n_scr[h] = decay * n_row + _dot(jnp.broadcast_to(w_wr, (8, L)).astype(bf16), k)[0:1]
    m_scr[h] = m_new

    hn = _rms(hc, hg_ref[sub])
    og = og_ref[:, sub * dv:(sub + 1) * dv].astype(f32)
    o_ref[:, sub * dv:(sub + 1) * dv] = (jax.nn.sigmoid(og) * hn).astype(o_ref.dtype)


def _mlstm(proj, gates_t, b_if, head_g):
    t_len = proj.shape[0]
    H, dk, dv, L = A_HEADS, A_DQK, A_DV, A_CHUNK
    hps = MLSTM_HEADS_PER_STEP
    wk, wv = hps * dk, hps * dv
    return pl.pallas_call(
        _mlstm_kernel,
        grid=(t_len // L, H // hps),
        in_specs=[
            pl.BlockSpec((L, wk), lambda c, h: (c, h)),
            pl.BlockSpec((L, wk), lambda c, h: (c, H * dk // wk + h)),
            pl.BlockSpec((L, wv), lambda c, h: (c, 2 * H * dk // wv + h)),
            pl.BlockSpec((L, wv), lambda c, h: (c, (2 * H * dk + H * dv) // wv + h)),
            pl.BlockSpec((2 * H, L), lambda c, h: (0, c)),
            pl.BlockSpec((2 * H, 1), lambda c, h: (0, 0)),
            pl.BlockSpec((hps, 1, dv), lambda c, h: (h, 0, 0)),
        ],
        out_specs=pl.BlockSpec((L, wv), lambda c, h: (c, h)),
        out_shape=jax.ShapeDtypeStruct((t_len, H * dv), bf16),
        scratch_shapes=[pltpu.VMEM((H, dk, dv), f32), pltpu.VMEM((H, 1, dk), f32), pltpu.VMEM((H, 1, 1), f32)],
        compiler_params=_params("arbitrary", "arbitrary"),
        name="mlstm",
    )(proj, proj, proj, proj, gates_t, b_if.reshape(2 * H, 1), head_g.reshape(H, 1, dv))


def _memattn_kernel(x_ref, g_ref, wq_ref, kv_ref, qg_ref, kg_ref, wo_ref, o_ref):
    x = x_ref[...]
    h = _rms(x, g_ref[...]).astype(bf16)
    qp = _dot(h, wq_ref[...])
    dh = MEM_HEAD_DIM
    nkv = MEM_HEADS * dh
    outs = []
    for hd in range(MEM_HEADS):
        qn = _rms(qp[:, hd * dh:(hd + 1) * dh], qg_ref[...]).astype(bf16)
        kn = _rms(kv_ref[:, hd * dh:(hd + 1) * dh], kg_ref[...]).astype(bf16)
        vh = kv_ref[:, nkv + hd * dh:nkv + (hd + 1) * dh].astype(bf16)
        s = _nt(qn, kn) * (dh ** -0.5)
        e = jnp.exp(s - jnp.max(s, axis=1, keepdims=True))
        p = e / jnp.sum(e, axis=1, keepdims=True)
        outs.append(_dot(p.astype(bf16), vh).astype(bf16))
    o = jnp.concatenate(outs, axis=1)
    o_ref[...] = x + _dot(o, wo_ref[...])


def _memattn(x, gain, wq, kv, qg, kg, wo, *, tm):
    t_len, d = x.shape
    md = MEM_HEADS * MEM_HEAD_DIM
    return pl.pallas_call(
        _memattn_kernel,
        grid=(t_len // tm,),
        in_specs=[
            pl.BlockSpec((tm, d), lambda i: (i, 0)),
            pl.BlockSpec((1, d), lambda i: (0, 0)),
            pl.BlockSpec((d, md), lambda i: (0, 0)),
            pl.BlockSpec((N_MEM, 2 * md), lambda i: (0, 0)),
            pl.BlockSpec((1, MEM_HEAD_DIM), lambda i: (0, 0)),
            pl.BlockSpec((1, MEM_HEAD_DIM), lambda i: (0, 0)),
            pl.BlockSpec((md, d), lambda i: (0, 0)),
        ],
        out_specs=pl.BlockSpec((tm, d), lambda i: (i, 0)),
        out_shape=jax.ShapeDtypeStruct((t_len, d), f32),
        compiler_params=_params("parallel"),
        name="memattn",
    )(x, gain.reshape(1, d), wq, kv, qg.reshape(1, -1), kg.reshape(1, -1), wo)


def _router_kernel(x_ref, g_ref, wr_ref, br_ref, h_ref, comb_ref, rank_ref, cnt_ref, carry_scr):
    tm = x_ref.shape[0]

    @pl.when(pl.program_id(0) == 0)
    def _():
        carry_scr[...] = jnp.zeros(carry_scr.shape, f32)

    hf = _rms(x_ref[...], g_ref[...])
    h_ref[...] = hf
    h1, h2, _ = _split3(hf)
    w1, w2, _ = _split3(wr_ref[...])
    logits = _dot(h2, w1) + _dot(h1, w2) + _dot(h1, w1) + br_ref[...]
    col = [logits[:, i:i + 1] for i in range(N_GROUPS + N_EXPERTS)]
    gl = col[:N_GROUPS]
    gm = functools.reduce(jnp.maximum, gl)
    ge = [jnp.exp(z - gm) for z in gl]
    gs = functools.reduce(jnp.add, ge)
    gp = [z / gs for z in ge]
    g_top = functools.reduce(jnp.maximum, gp)
    sel_g = []
    taken = jnp.zeros(g_top.shape, jnp.bool_)
    for z in gp:
        hit = (z == g_top) & jnp.logical_not(taken)
        sel_g.append(hit)
        taken = taken | hit
    E = EXPERTS_PER_GROUP
    el = []
    for e in range(E):
        acc = jnp.zeros_like(g_top)
        for g in range(N_GROUPS):
            acc = acc + jnp.where(sel_g[g], col[N_GROUPS + g * E + e], 0.0)
        el.append(acc)
    em = functools.reduce(jnp.maximum, el)
    ee = [jnp.exp(z - em) for z in el]
    es = functools.reduce(jnp.add, ee)
    ep = [z / es for z in ee]
    sel_e = []
    for e in range(E):
        rank = jnp.zeros(g_top.shape, jnp.int32)
        for e2 in range(E):
            if e2 == e:
                continue
            ahead = (ep[e2] > ep[e]) | ((ep[e2] == ep[e]) & (e2 < e))
            rank = rank + ahead.astype(jnp.int32)
        sel_e.append(rank < 2)
    top_sum = functools.reduce(jnp.add, [jnp.where(sel_e[e], ep[e], 0.0) for e in range(E)])
    lane = lax.broadcasted_iota(jnp.int32, comb_ref.shape, 1)
    comb = jnp.zeros(comb_ref.shape, f32)
    for g in range(N_GROUPS):
        for e in range(E):
            wge = jnp.where(sel_g[g] & sel_e[e], ep[e] / top_sum * g_top, 0.0)
            comb = jnp.where(lane == g * E + e, wge, comb)
    comb_ref[...] = comb

    member = jnp.where(comb > 0.0, 1.0, 0.0)
    r_i = lax.broadcasted_iota(jnp.int32, (tm, tm), 0)
    c_i = lax.broadcasted_iota(jnp.int32, (tm, tm), 1)
    earlier = jnp.where(c_i < r_i, 1.0, 0.0).astype(bf16)
    carry = carry_scr[...]
    rank_ref[...] = _dot(earlier, member.astype(bf16)) + carry
    carry = carry + jnp.sum(member, axis=0, keepdims=True)
    carry_scr[...] = carry
    cnt_ref[...] = carry


def _router(x, gain, w_r, b_r, *, tm):
    t_len, d = x.shape
    row = lambda i: (i, 0)
    return pl.pallas_call(
        _router_kernel,
        grid=(t_len // tm,),
        in_specs=[
            pl.BlockSpec((tm, d), row),
            pl.BlockSpec((1, d), lambda i: (0, 0)),
            pl.BlockSpec((d, LANES), lambda i: (0, 0)),
            pl.BlockSpec((1, LANES), lambda i: (0, 0)),
        ],
        out_specs=[pl.BlockSpec((tm, d), row), pl.BlockSpec((tm, LANES), row),
                   pl.BlockSpec((tm, LANES), row), pl.BlockSpec((1, LANES), lambda i: (0, 0))],
        out_shape=[jax.ShapeDtypeStruct((t_len, d), f32), jax.ShapeDtypeStruct((t_len, LANES), f32),
                   jax.ShapeDtypeStruct((t_len, LANES), f32), jax.ShapeDtypeStruct((1, LANES), f32)],
        scratch_shapes=[pltpu.VMEM((1, LANES), f32)],
        compiler_params=_params("arbitrary"),
        name="moe_router",
    )(x, gain.reshape(1, d), w_r, b_r)


def _row(ref, r):
    return ref.at[pl.ds(r, 1)]


def _meta_kernel(comb_ref, rank_ref, cnt_ref, pos_ref, w_ref, te_ref, nu_ref):
    comb = comb_ref[...]
    lane1 = lax.broadcasted_iota(jnp.int32, (1, LANES), 1)
    cnt = cnt_ref[...]
    gsz = jnp.ceil(cnt * (1.0 / MOE_TM)) * MOE_TM
    e_r = lax.broadcasted_iota(jnp.int32, (LANES, LANES), 0)
    e_c = lax.broadcasted_iota(jnp.int32, (LANES, LANES), 1)
    before = jnp.where(e_r < e_c, 1.0, 0.0).astype(bf16)
    goff = _dot_f32_exactrhs(jnp.broadcast_to(gsz, (8, LANES)), before)[0:1]
    member = comb > 0.0
    pos = jnp.where(member, goff + rank_ref[...], -1.0)
    pa = jnp.max(pos, axis=1, keepdims=True)
    is_a = member & (pos == pa)
    wa = jnp.sum(jnp.where(is_a, comb, 0.0), axis=1, keepdims=True)
    rest = member & jnp.logical_not(is_a)
    pb = jnp.max(jnp.where(rest, pos, -1.0), axis=1, keepdims=True)
    wb = jnp.sum(jnp.where(rest, comb, 0.0), axis=1, keepdims=True)
    lane = lax.broadcasted_iota(jnp.int32, comb.shape, 1)
    pos_ref[...] = jnp.where(lane == 0, pa, jnp.where(lane == 1, pb, -1.0)).astype(jnp.int32)
    w_ref[...] = jnp.where(lane == 0, wa, jnp.where(lane == 1, wb, 0.0))
    valid_e = lane1 < N_EXPERTS
    gend = goff + gsz
    tile_start = lax.broadcasted_iota(jnp.int32, (LANES, LANES), 0).astype(f32) * MOE_TM
    te = jnp.sum(jnp.where(valid_e & (gend <= tile_start), 1.0, 0.0), axis=1, keepdims=True)
    last = jnp.max(jnp.where(valid_e & (gsz > 0.0), lane1.astype(f32), 0.0), axis=1, keepdims=True)
    te_ref[...] = jnp.minimum(te, last).astype(jnp.int32)
    nu_ref[...] = (jnp.sum(jnp.where(valid_e, gsz, 0.0), axis=1, keepdims=True) * (1.0 / MOE_TM)).astype(jnp.int32)


def _moe_meta(comb, rank, cnt, *, tm):
    t_len = comb.shape[0]
    row = lambda i: (i, 0)
    fixed = lambda i: (0, 0)
    return pl.pallas_call(
        _meta_kernel,
        grid=(t_len // tm,),
        in_specs=[pl.BlockSpec((tm, LANES), row), pl.BlockSpec((tm, LANES), row), pl.BlockSpec((1, LANES), fixed)],
        out_specs=[pl.BlockSpec((tm, LANES), row), pl.BlockSpec((tm, LANES), row),
                   pl.BlockSpec((LANES, 1), fixed), pl.BlockSpec((1, 1), fixed)],
        out_shape=[jax.ShapeDtypeStruct((t_len, LANES), jnp.int32), jax.ShapeDtypeStruct((t_len, LANES), f32),
                   jax.ShapeDtypeStruct((LANES, 1), jnp.int32), jax.ShapeDtypeStruct((1, 1), jnp.int32)],
        compiler_params=_params("arbitrary"),
        name="moe_meta",
    )(comb, rank, cnt)


def _invmap_kernel(pos_ref, src_ref):
    n_tok = pos_ref.shape[0] // EXPERT_TOPK

    def fill(p, carry):
        src_ref[p] = p & (n_tok - 1)
        return carry

    lax.fori_loop(0, src_ref.shape[0], fill, 0, unroll=16)

    def body(k, carry):
        for slot in range(EXPERT_TOPK):
            p = pos_ref[EXPERT_TOPK * k + slot]

            @pl.when(p >= 0)
            def _():
                src_ref[p] = k
        return carry

    lax.fori_loop(0, pos_ref.shape[0] // EXPERT_TOPK, body, 0, unroll=8)


def _moe_invmap(pos_flat):
    return pl.pallas_call(
        _invmap_kernel,
        in_specs=[pl.BlockSpec(memory_space=pltpu.SMEM)],
        out_specs=pl.BlockSpec(memory_space=pltpu.SMEM),
        out_shape=jax.ShapeDtypeStruct((MOE_ROWS,), jnp.int32),
        name="moe_invmap",
    )(pos_flat)


def _gather_kernel(src_ref, nu_ref, h_ref, xs_ref, buf, sems):
    i = pl.program_id(0)
    n_used = nu_ref[0]

    def copy(tile, slot, k):
        return pltpu.make_async_copy(_row(h_ref, src_ref[tile * MOE_TM + k]), _row(buf.at[slot], k), sems.at[slot])

    def start_tile(tile, slot):
        def body(k, carry):
            copy(tile, slot, k).start()
            return carry

        lax.fori_loop(0, MOE_TM, body, 0, unroll=8)

    @pl.when((i == 0) & (n_used > 0))
    def _():
        start_tile(0, 0)

    for slot in range(2):
        @pl.when((i % 2 == slot) & (i < n_used))
        def _():
            @pl.when(i + 1 < n_used)
            def _():
                start_tile(i + 1, 1 - slot)

            def wait(k, carry):
                copy(i, slot, k).wait()
                return carry

            lax.fori_loop(0, MOE_TM, wait, 0, unroll=8)
            xs_ref[...] = buf[slot].astype(xs_ref.dtype)

    @pl.when(i >= n_used)
    def _():
        xs_ref[...] = jnp.zeros(xs_ref.shape, xs_ref.dtype)


def _moe_gather(src, nused, h):
    d = h.shape[1]
    return pl.pallas_call(
        _gather_kernel,
        grid_spec=pltpu.PrefetchScalarGridSpec(
            num_scalar_prefetch=2,
            grid=(MOE_TILES,),
            in_specs=[pl.BlockSpec(memory_space=pl.ANY)],
            out_specs=pl.BlockSpec((MOE_TM, d), lambda i, src, nu: (i, 0)),
            scratch_shapes=[pltpu.VMEM((2, MOE_TM, d), f32), pltpu.SemaphoreType.DMA((2,))],
        ),
        out_shape=jax.ShapeDtypeStruct((MOE_ROWS, d), bf16),
        compiler_params=pltpu.CompilerParams(dimension_semantics=("arbitrary",), vmem_limit_bytes=VMEM_LIMIT,
                                             disable_bounds_checks=True),
        name="moe_gather",
    )(src, nused, h)


def _expert_changed(te_ref, i):
    return (i == 0) | (te_ref[i] != te_ref[jnp.maximum(i - 1, 0)])


def _moe_hid_kernel(te_ref, nu_ref, xs_ref, w1_ref, w3_ref, o_ref, w1b, w3b):
    i = pl.program_id(1)

    @pl.when(_expert_changed(te_ref, i))
    def _():
        w1b[...] = w1_ref[...].astype(bf16)
        w3b[...] = w3_ref[...].astype(bf16)

    @pl.when(i < nu_ref[0])
    def _():
        x = xs_ref[...]
        u = _dot(x, w1b[...])
        g = _dot(x, w3b[...])
        o_ref[...] = (u * jax.nn.sigmoid(u) * g).astype(o_ref.dtype)

    @pl.when(i >= nu_ref[0])
    def _():
        o_ref[...] = jnp.zeros(o_ref.shape, o_ref.dtype)


def _moe_hid(te, nused, xs, w1, w3, layer, *, tn):
    _, ne, d, fdim = w1.shape
    used = lambda i, nu: jnp.where(i < nu[0], i, 0)
    return pl.pallas_call(
        _moe_hid_kernel,
        grid_spec=pltpu.PrefetchScalarGridSpec(
            num_scalar_prefetch=2,
            grid=(fdim // tn, MOE_TILES),
            in_specs=[
                pl.BlockSpec((MOE_TM, d), lambda j, i, te, nu: (used(i, nu), 0)),
                pl.BlockSpec((None, None, d, tn), lambda j, i, te, nu: (layer, te[i], 0, j)),
                pl.BlockSpec((None, None, d, tn), lambda j, i, te, nu: (layer, te[i], 0, j)),
            ],
            out_specs=pl.BlockSpec((MOE_TM, tn), lambda j, i, te, nu: (i, j)),
            scratch_shapes=[pltpu.VMEM((d, tn), bf16), pltpu.VMEM((d, tn), bf16)],
        ),
        out_shape=jax.ShapeDtypeStruct((MOE_ROWS, fdim), bf16),
        compiler_params=_params("arbitrary", "arbitrary"),
        name="moe_hid",
    )(te, nused, xs, w1, w3)


def _moe_w2_kernel(te_ref, nu_ref, hid_ref, w2_ref, o_ref, w2b):
    i = pl.program_id(0)

    @pl.when(_expert_changed(te_ref, i))
    def _():
        w2b[...] = w2_ref[...].astype(bf16)

    @pl.when(i < nu_ref[0])
    def _():
        o_ref[...] = _dot(hid_ref[...], w2b[...])

    @pl.when(i >= nu_ref[0])
    def _():
        o_ref[...] = jnp.zeros(o_ref.shape, o_ref.dtype)


def _moe_w2(te, nused, hid, w2, layer):
    _, ne, fdim, d = w2.shape
    used = lambda i, nu: jnp.where(i < nu[0], i, 0)
    return pl.pallas_call(
        _moe_w2_kernel,
        grid_spec=pltpu.PrefetchScalarGridSpec(
            num_scalar_prefetch=2,
            grid=(MOE_TILES,),
            in_specs=[
                pl.BlockSpec((MOE_TM, fdim), lambda i, te, nu: (used(i, nu), 0)),
                pl.BlockSpec((None, None, fdim, d), lambda i, te, nu: (layer, te[i], 0, 0),
                             pipeline_mode=pl.Buffered(1)),
            ],
            out_specs=pl.BlockSpec((MOE_TM, d), lambda i, te, nu: (i, 0)),
            scratch_shapes=[pltpu.VMEM((fdim, d), bf16)],
        ),
        out_shape=jax.ShapeDtypeStruct((MOE_ROWS, d), f32),
        compiler_params=_params("arbitrary"),
        name="moe_w2",
    )(te, nused, hid, w2)


def _moe_combine_kernel(pos_ref, x_ref, w_ref, ys_ref, o_ref, buf, sems):
    tm, d = x_ref.shape
    i = pl.program_id(0)

    def copies(tile, slot, k):
        out = []
        for which in range(EXPERT_TOPK):
            p = jnp.maximum(pos_ref[EXPERT_TOPK * (tile * tm + k) + which], 0)
            out.append(pltpu.make_async_copy(_row(ys_ref, p), _row(buf.at[slot, which], k), sems.at[slot]))
        return out

    def start_tile(tile, slot):
        def body(k, carry):
            for cp in copies(tile, slot, k):
                cp.start()
            return carry

        lax.fori_loop(0, tm, body, 0, unroll=8)

    @pl.when(i == 0)
    def _():
        start_tile(0, 0)

    for slot in range(2):
        @pl.when(i % 2 == slot)
        def _():
            @pl.when(i + 1 < pl.num_programs(0))
            def _():
                start_tile(i + 1, 1 - slot)

            def wait(k, carry):
                for cp in copies(i, slot, k):
                    cp.wait()
                return carry

            lax.fori_loop(0, tm, wait, 0, unroll=8)
            o_ref[...] = x_ref[...] + w_ref[:, 0:1] * buf[slot, 0] + w_ref[:, 1:2] * buf[slot, 1]


def _moe_combine(pos_flat, x, tokw, ys, *, tm):
    t_len, d = x.shape
    return pl.pallas_call(
        _moe_combine_kernel,
        grid_spec=pltpu.PrefetchScalarGridSpec(
            num_scalar_prefetch=1,
            grid=(t_len // tm,),
            in_specs=[
                pl.BlockSpec((tm, d), lambda i, pos: (i, 0)),
                pl.BlockSpec((tm, LANES), lambda i, pos: (i, 0)),
                pl.BlockSpec(memory_space=pl.ANY),
            ],
            out_specs=pl.BlockSpec((tm, d), lambda i, pos: (i, 0)),
            scratch_shapes=[pltpu.VMEM((2, EXPERT_TOPK, tm, d), f32), pltpu.SemaphoreType.DMA((2,))],
        ),
        out_shape=jax.ShapeDtypeStruct((t_len, d), f32),
        compiler_params=pltpu.CompilerParams(dimension_semantics=("arbitrary",), vmem_limit_bytes=VMEM_LIMIT,
                                             disable_bounds_checks=True),
        name="moe_combine",
    )(pos_flat, x, tokw, ys)


def _hier_moe(x, gain, wg, bg, we, be, w1, w3, w2, layer):
    d = x.shape[1]
    w_r = jnp.concatenate([wg, we.transpose(1, 0, 2).reshape(d, N_EXPERTS)], axis=1)
    w_r = jnp.pad(w_r, ((0, 0), (0, LANES - w_r.shape[1])))
    b_r = jnp.pad(jnp.concatenate([bg, be.reshape(-1)]), (0, LANES - N_GROUPS - N_EXPERTS)).reshape(1, LANES)
    h, comb, rank, cnt = _router(x, gain, w_r, b_r, tm=256)
    tokpos, tokw, te, nused = _moe_meta(comb, rank, cnt, tm=512)
    pos_flat = tokpos[:, :EXPERT_TOPK].reshape(-1)
    te = te.reshape(-1)
    nused = nused.reshape(-1)
    src = _moe_invmap(pos_flat)
    xs = _moe_gather(src, nused, h)
    hid = _moe_hid(te, nused, xs, w1, w3, layer, tn=512)
    ys = _moe_w2(te, nused, hid, w2, layer)
    return _moe_combine(pos_flat, x, tokw, ys, tm=256)


def _compress_kernel(z_ref, pe_ref, w1_ref, w2_ref, kg_ref, o_ref):
    half = CMP_STRIDE * B_HEAD_DIM
    z = z_ref[...].astype(f32)
    zt = (z + pe_ref[:, :half]).astype(bf16)
    zb = (z + pe_ref[:, half:]).astype(bf16)
    top = _dot(zt, w1_ref[:half, :])
    bot = _dot(zb, w1_ref[half:, :])
    pre = top + pltpu.roll(bot, N_CMP_PAD - 1, axis=0)
    act = pre * (0.5 * (1.0 + jnp.tanh(np.sqrt(2.0 / np.pi).astype(np.float32) * (pre + 0.044715 * (pre * pre * pre)))))
    out = _dot(act.astype(bf16), w2_ref[...])
    normed = _rms(out, kg_ref[...])
    o_ref[...] = jnp.where(pl.program_id(0) == 0, normed, out).astype(o_ref.dtype)


def _compress(zr, pe_flat, w1, w2, kg):
    G = B_KV_HEADS
    return pl.pallas_call(
        _compress_kernel,
        grid=(2, G),
        in_specs=[
            pl.BlockSpec((None, None, N_CMP_PAD, CMP_STRIDE * B_HEAD_DIM), lambda w, g: (w, g, 0, 0)),
            pl.BlockSpec((None, 1, CMP_BLOCK * B_HEAD_DIM), lambda w, g: (w, 0, 0)),
            pl.BlockSpec((None, CMP_BLOCK * B_HEAD_DIM, CMP_HIDDEN), lambda w, g: (w, 0, 0)),
            pl.BlockSpec((None, CMP_HIDDEN, B_HEAD_DIM), lambda w, g: (w, 0, 0)),
            pl.BlockSpec((1, B_HEAD_DIM), lambda w, g: (0, 0)),
        ],
        out_specs=pl.BlockSpec((None, None, N_CMP_PAD, B_HEAD_DIM), lambda w, g: (w, g, 0, 0)),
        out_shape=jax.ShapeDtypeStruct((2, G, N_CMP_PAD, B_HEAD_DIM), bf16),
        compiler_params=_params("arbitrary", "arbitrary"),
        name="nsa_compress",
    )(zr, pe_flat, w1, w2, kg)


def _key_extra(pos):
    lane = lax.broadcasted_iota(jnp.int32, pos.shape, 1)
    lo = pos & (LANES - 1)
    return jnp.where(lane < 3, lo.astype(f32), jnp.where(lane < 6, (pos - lo).astype(f32), 0.0)).astype(bf16)


def _nsa_kernel(q_ref, gate_ref, sl_ref, kc_ref, vct_ref, ks_ref, vs_ref, kw_ref, vw_ref, o_ref,
                qa_scr, selb_scr, m_scr, acc_scr, out_scr):
    c = pl.program_id(1)
    R, QB, dh = B_GROUP, Q_BLOCK, B_HEAD_DIM
    rows = R * QB
    t0 = c * QB
    slabs = [slice(r * QB, (r + 1) * QB) for r in range(R)]

    qa_scr[0:dh, :] = jnp.concatenate([q_ref[r].astype(f32).T for r in range(R)], axis=1).astype(bf16)
    s1, s2, s3 = [p.astype(f32) for p in _split3(sl_ref[...] * LOG2E)]
    rid = lax.broadcasted_iota(jnp.int32, (dh, rows), 0)
    qa_scr[dh:2 * dh, :] = jnp.where((rid == 0) | (rid == 3), s1,
                                     jnp.where((rid == 1) | (rid == 4), s2,
                                               jnp.where((rid == 2) | (rid == 5), s3, 0.0))).astype(bf16)

    kidx = lax.broadcasted_iota(jnp.int32, (QB, QB), 0)
    tidx = lax.broadcasted_iota(jnp.int32, (QB, QB), 1)
    causal = kidx <= tidx

    gate_t = jax.nn.sigmoid(gate_ref[...]).T

    def scores(k_rows, pos):
        return jnp.concatenate([k_rows, _key_extra(pos)], axis=1)

    hpg = NSA_HEADS_PER_GROUP
    groups = [(slice(g0 * QB, (g0 + hpg) * QB), list(range(g0, g0 + hpg))) for g0 in range(0, R, hpg)]

    def first_chunk(ka, ok, v_t):
        for cols, heads in groups:
            s_g = _dot(ka, qa_scr[:, cols])
            ps = []
            for n, r in enumerate(heads):
                s_r = jnp.where(ok, s_g[:, n * QB:(n + 1) * QB], NEG_INF)
                m = jnp.max(s_r, axis=0, keepdims=True)
                m_scr[:, slabs[r]] = m
                ps.append(jnp.where(ok, jnp.exp2(s_r - m), 0.0).astype(bf16))
            acc_scr[:, cols] = _dot(v_t, jnp.concatenate(ps, axis=1))

    def next_chunk(ka, mask_bias, v_t):
        for cols, heads in groups:
            s_g = _dot(ka, qa_scr[:, cols])
            ps, alphas = [], []
            for n, r in enumerate(heads):
                s_r = s_g[:, n * QB:(n + 1) * QB] + mask_bias
                m_old = m_scr[:, slabs[r]]
                m_new = jnp.maximum(m_old, jnp.max(s_r, axis=0, keepdims=True))
                m_scr[:, slabs[r]] = m_new
                alphas.append(jnp.exp2(m_old - m_new))
                ps.append(jnp.exp2(s_r - m_new).astype(bf16))
            pv = _dot(v_t, jnp.concatenate(ps, axis=1))
            for n, r in enumerate(heads):
                acc_scr[:, slabs[r]] = alphas[n] * acc_scr[:, slabs[r]] + pv[:, n * QB:(n + 1) * QB]

    def emit(branch):
        for r in range(R):
            w = gate_t[branch * R + r:branch * R + r + 1, :] / acc_scr[dh:dh + 1, slabs[r]]
            out_scr[:, slabs[r]] = out_scr[:, slabs[r]] + acc_scr[0:dh, slabs[r]] * w

    cend = lax.broadcasted_iota(jnp.int32, (N_CMP_PAD, QB), 0) * CMP_STRIDE + (CMP_BLOCK - 1)
    ok_c = cend <= t0 + lax.broadcasted_iota(jnp.int32, (N_CMP_PAD, QB), 1)
    s_c = _dot(scores(kc_ref[...], cend), qa_scr[...])
    p_sum = jnp.zeros((N_CMP_PAD, QB), f32)
    ps = []
    for r in range(R):
        s_r = jnp.where(ok_c, s_c[:, slabs[r]], NEG_INF)
        e = jnp.where(ok_c, jnp.exp2(s_r - jnp.max(s_r, axis=0, keepdims=True)), 0.0)
        l = jnp.sum(e, axis=0, keepdims=True)
        p = e / jnp.where(l > 0.0, l, 1.0)
        p_sum = p_sum + p
        ps.append(p.astype(bf16))
    oc_t = _dot(vct_ref[...], jnp.concatenate(ps, axis=1))
    for r in range(R):
        out_scr[:, slabs[r]] = oc_t[:, slabs[r]] * gate_t[r:r + 1, :]

    b_i = lax.broadcasted_iota(jnp.int32, (N_SEL, N_CMP_PAD), 0) * SEL_BLOCK
    n_i = lax.broadcasted_iota(jnp.int32, (N_SEL, N_CMP_PAD), 1) * CMP_STRIDE
    overlap_t = jnp.where((n_i < b_i + SEL_BLOCK) & (n_i + (CMP_BLOCK - 1) >= b_i), 1.0, 0.0).astype(bf16)
    p1, p2, p3 = _split3(p_sum)
    imp = _dot(overlap_t, p3) + _dot(overlap_t, p2) + _dot(overlap_t, p1)
    tq = t0 + tidx
    cur = tq // SEL_BLOCK
    forced = (kidx == 0) | (kidx == cur) | (kidx == cur - 1)
    imp = jnp.where(forced, SEL_FORCE, imp)
    imp = jnp.where(kidx * SEL_BLOCK <= tq, imp, -1.0)
    blk_f = kidx.astype(f32)
    sel = jnp.zeros((N_SEL, QB), f32)
    for _ in range(SEL_TOPK):
        mx = jnp.max(imp, axis=0, keepdims=True)
        first = jnp.min(jnp.where(imp == mx, blk_f, float(N_SEL)), axis=0, keepdims=True)
        pick = blk_f == first
        sel = jnp.where(pick & (mx >= 0.0), 1.0, sel)
        imp = jnp.where(pick, -jnp.inf, imp)
    selb_scr[...] = jnp.where(sel > 0.5, 0.0, NEG_INF)

    def sel_bias(first_block, n_blocks):
        return jnp.concatenate([jnp.broadcast_to(selb_scr[pl.ds(first_block + b, 1), :], (SEL_BLOCK, QB))
                                for b in range(n_blocks)], axis=0)

    def chunk_pos(i):
        return i * QB + kidx

    off_c = pl.multiple_of(t0, QB)
    ones_rows = jnp.where(lax.broadcasted_iota(jnp.int32, (BF16_SUBLANES, QB), 0) == 0, 1.0, 0.0).astype(bf16)

    def values_t(v_ref, first, n):
        cols = []
        for m in range(n):
            v = v_ref[pl.ds(pl.multiple_of((first + m) * QB, QB), QB), :]
            cols.append(jnp.concatenate([v.astype(f32).T.astype(bf16), ones_rows], axis=0))
        return cols[0] if n == 1 else jnp.concatenate(cols, axis=1)

    first_chunk(scores(ks_ref[pl.ds(off_c, QB), :], chunk_pos(c)), causal & (sel_bias(2 * c, 2) > -1.0),
                values_t(vs_ref, c, 1))
    wide = SEL_WIDE_CHUNKS
    blocks_per_chunk = QB // SEL_BLOCK

    def sel_step(first, n):
        off = pl.multiple_of(first * QB, QB)
        pos = off + lax.broadcasted_iota(jnp.int32, (n * QB, QB), 0)
        next_chunk(scores(ks_ref[pl.ds(off, n * QB), :], pos),
                   sel_bias(blocks_per_chunk * first, blocks_per_chunk * n), values_t(vs_ref, first, n))

    def sel_wide(j, carry):
        sel_step(j * wide, wide)
        return carry

    def sel_narrow(i, carry):
        sel_step(i, 1)
        return carry

    half = wide // 2
    done = (c // wide) * wide
    lax.fori_loop(0, c // wide, sel_wide, 0)

    @pl.when(c - done >= half)
    def _():
        sel_step(done, half)

    lax.fori_loop(jnp.where(c - done >= half, done + half, done), c, sel_narrow, 0)
    emit(1)

    first_chunk(scores(kw_ref[pl.ds(off_c, QB), :], chunk_pos(c)), causal, values_t(vw_ref, c, 1))
    n_back = WINDOW // QB

    @pl.when(c >= n_back)
    def _():
        first = c - n_back
        off = pl.multiple_of(first * QB, QB)
        pos = off + lax.broadcasted_iota(jnp.int32, (n_back * QB, QB), 0)
        bias = jnp.concatenate([jnp.where(kidx > tidx, 0.0, NEG_INF), jnp.zeros(((n_back - 1) * QB, QB), f32)], axis=0)
        next_chunk(scores(kw_ref[pl.ds(off, n_back * QB), :], pos), bias, values_t(vw_ref, first, n_back))

    @pl.when(c < n_back)
    def _():
        def win_body(i, carry):
            off = pl.multiple_of(i * QB, QB)
            next_chunk(scores(kw_ref[pl.ds(off, QB), :], chunk_pos(i)), jnp.zeros((QB, QB), f32),
                       values_t(vw_ref, i, 1))
            return carry

        lax.fori_loop(0, c, win_body, 0)

    emit(2)

    for r in range(R):
        o_ref[:, r * dh:(r + 1) * dh] = out_scr[:, slabs[r]].T.astype(o_ref.dtype)


def _nsa_attention(qh, gates, slopes, cmp_kv, kvh):
    t_len = qh.shape[1]
    G, R, QB, dh = B_KV_HEADS, B_GROUP, Q_BLOCK, B_HEAD_DIM
    rows = R * QB
    nq = t_len // QB
    kc = cmp_kv[0]
    vc_t = cmp_kv[1].swapaxes(1, 2)
    dha = dh + BF16_SUBLANES

    def kv_spec(base):
        return pl.BlockSpec((None, t_len, dh), lambda g, c: (base + g, 0, 0))
    return pl.pallas_call(
        _nsa_kernel,
        grid=(G, nq),
        in_specs=[
            pl.BlockSpec((R, QB, dh), lambda g, c: (g, c, 0)),
            pl.BlockSpec((QB, LANES), lambda g, c: (c, g)),
            pl.BlockSpec((None, 1, rows), lambda g, c: (g, 0, 0)),
            pl.BlockSpec((None, N_CMP_PAD, dh), lambda g, c: (g, 0, 0)),
            pl.BlockSpec((None, dh, N_CMP_PAD), lambda g, c: (g, 0, 0)),
            kv_spec(2 * G), kv_spec(3 * G), kv_spec(4 * G), kv_spec(5 * G),
        ],
        out_specs=pl.BlockSpec((QB, R * dh), lambda g, c: (c, g)),
        out_shape=jax.ShapeDtypeStruct((t_len, G * R * dh), bf16),
        scratch_shapes=[
            pltpu.VMEM((2 * dh, rows), bf16),
            pltpu.VMEM((N_SEL, QB), f32),
            pltpu.VMEM((1, rows), f32),
            pltpu.VMEM((dha, rows), f32),
            pltpu.VMEM((dh, rows), f32),
        ],
        compiler_params=_params("parallel", "arbitrary"),
        name="nsa_attention",
    )(qh, gates, slopes, kc, vc_t, kvh, kvh, kvh, kvh)


def _mlstm_layer(x, gain, w_in, layer, b_if, head_g, w_out):
    H, dk, dv = A_HEADS, A_DQK, A_DV
    n_main = 2 * H * dk + 2 * H * dv
    w_gate = jnp.pad(w_in[layer, :, n_main:], ((0, 0), (0, LANES - 2 * H))).astype(bf16)
    h, = _rmsnorm_bf16(x, [gain], tm=512, name="mlstm_norm")
    proj = _matmul_ws(h, w_in, layer, n_main, out_dtype=bf16, tm=WS_TM, tn=WS_TN, name="mlstm_in")
    gates = _matmul(h, w_gate, out_dtype=f32, tm=1024, tn=LANES, name="mlstm_gates")
    gates_t = gates[:, :2 * H].T
    o = _mlstm(proj, gates_t, b_if, head_g)
    return _matmul_ws(o, w_out, layer, w_out.shape[2], residual=x, out_dtype=f32, tm=WS_TM, tn=WS_TN,
                      name="mlstm_out")


def _memattn_layer(x, gain, mem, mem_g, wq, wk, wv, wo, q_g, k_g):
    wkv = jnp.concatenate([wk, wv], axis=1).astype(bf16)
    kv = _matmul(mem, wkv, gain=mem_g, out_dtype=f32, tm=N_MEM, tn=512, name="mem_kv")
    return _memattn(x, gain, wq.astype(bf16), kv, q_g, k_g, wo.astype(bf16), tm=256)


def _alibi_slopes(n):
    return np.asarray([2.0 ** (-8.0 * (h + 1) / n) for h in range(n)], dtype=np.float32)


def _nsa_shared_kv(x, kv_norm_g, kv_w, cmp_pe, cmp_w1, cmp_w2, k_norm_g):
    G, dh = B_KV_HEADS, B_HEAD_DIM
    t_len = x.shape[0]
    ones = jnp.ones((G, 1, dh), f32)
    zeros = jnp.zeros((G, 1, dh), f32)
    head_gain = jnp.concatenate([ones, ones, ones * k_norm_g[1], ones, ones * k_norm_g[2], ones], axis=0)
    norm_flag = jnp.concatenate([zeros, zeros, ones, zeros, ones, zeros], axis=0)
    h, = _rmsnorm_bf16(x, [kv_norm_g], tm=512, name="nsa_kv_norm")
    kvh = _proj_heads(h, kv_w[None], 0, kv_w.shape[1], head_gain, norm_flag, scale=1.0, tm=WS_TM, name="nsa_kv_proj")
    zr = kvh[:2 * G].reshape(2, G, t_len // CMP_STRIDE, CMP_STRIDE * dh)
    cmp_kv = _compress(zr, cmp_pe.reshape(2, 1, CMP_BLOCK * dh), cmp_w1.astype(bf16), cmp_w2.astype(bf16),
                       k_norm_g[0].reshape(1, dh))
    return cmp_kv, kvh


def _nsa_layer(x, gain, shared, w_in, layer, q_norm_g, w_out):
    cmp_kv, kvh = shared
    H, G, R, dh = B_HEADS, B_KV_HEADS, B_GROUP, B_HEAD_DIM
    ones = jnp.ones((H, 1, dh), f32)
    h, = _rmsnorm_bf16(x, [gain], tm=512, name="nsa_norm")
    qh = _proj_heads(h, w_in, layer, H * dh, ones * q_norm_g, ones, scale=dh ** -0.5 * LOG2E, tm=WS_TM,
                     name="nsa_q_proj")
    wg = w_in[layer, :, H * dh:].reshape(-1, 3, G, R).transpose(0, 2, 1, 3).reshape(-1, G, 3 * R)
    wg = jnp.pad(wg, ((0, 0), (0, 0), (0, LANES - 3 * R))).reshape(-1, G * LANES).astype(bf16)
    gates = _matmul(h, wg, out_dtype=f32, tm=1024, tn=512, name="nsa_gates")
    slopes = jnp.asarray(np.repeat(_alibi_slopes(H).reshape(G, R, 1), Q_BLOCK, axis=2).reshape(G, 1, R * Q_BLOCK))
    o = _nsa_attention(qh, gates, slopes, cmp_kv, kvh)
    return _matmul_ws(o, w_out, layer, w_out.shape[2], residual=x, out_dtype=f32, tm=WS_TM, tn=WS_TN, name="nsa_out")


def kernel(x, mem, norm_g, a_w_in, a_b_if, a_head_g, a_w_out, kv_norm_g, kv_w, cmp_pe, cmp_w1, cmp_w2, k_norm_g,
           b_w_in, b_q_norm_g, b_w_out, mem_norm_g, mem_wq, mem_wk, mem_wv, mem_wo, mem_q_g, mem_k_g, moe_wg,
           moe_bg, moe_we, moe_be, moe_w1, moe_w3, moe_w2):
    bsz, t_len, d = x.shape
    assert (bsz, t_len, d) == (1, SEQ, D_MODEL)
    xs = x.reshape(t_len, d)
    ms = mem.reshape(N_MEM, d)
    depth = norm_g.shape[0]
    n_a = depth - depth // 2
    shared = None
    for layer in range(depth):
        if layer < n_a:
            xs = _mlstm_layer(xs, norm_g[layer, 0], a_w_in, layer, a_b_if[layer], a_head_g[layer], a_w_out)
        else:
            if layer == n_a:
                shared = _nsa_shared_kv(xs, kv_norm_g, kv_w, cmp_pe, cmp_w1, cmp_w2, k_norm_g)
            j = layer - n_a
            xs = _nsa_layer(xs, norm_g[layer, 0], shared, b_w_in, j, b_q_norm_g[j], b_w_out)
        xs = _memattn_layer(xs, norm_g[layer, 1], ms, mem_norm_g[layer], mem_wq[layer], mem_wk[layer],
                            mem_wv[layer], mem_wo[layer], mem_q_g[layer], mem_k_g[layer])
        xs = _hier_moe(xs, norm_g[layer, 2], moe_wg[layer], moe_bg[layer], moe_we[layer], moe_be[layer],
                       moe_w1, moe_w3, moe_w2, layer)
    return xs.reshape(bsz, t_len, d)
```
